```python
import math
import jax
import jax.numpy as jnp
from jax import lax
import numpy as np

D_MODEL = 2048
BATCH = 8
SEQ = 2048
DEPTH = 2

CTX_LEN = 256
GRID_W = 64
EPS = 1e-6
ROPE_BASE = 10000.0
Q_BLOCK = 128
N_BRANCH = 4
BRANCH_W = 512

DA_HEADS = 4
DA_HEAD = 64
DA_V = 2 * DA_HEAD
RT_HEADS = 4
RT_DK = 128
RT_DV = 128
RT_CHUNK = 128
NA_HEADS = 8
NA_HEAD = 64
NA_KH = 8
NA_KW = 16
MLA_HEADS = 4
MLA_Q_LORA = 512
MLA_KV_LORA = 256
MLA_NOPE = 128
MLA_ROPE = 64
MLA_V = 128
PEER_HEADS = 8
PEER_NKEYS = 128
PEER_EXPERTS = PEER_NKEYS * PEER_NKEYS
PEER_DQ = 256
PEER_TOPK = 16
PEER_BLOCK = 128

IN_SPLITS = (
    DA_HEADS * 2 * DA_HEAD, DA_HEADS * 2 * DA_HEAD, DA_HEADS * DA_V,
    RT_HEADS * RT_DK, RT_HEADS * RT_DK, RT_HEADS * RT_DV, RT_HEADS * RT_DV,
    NA_HEADS * NA_HEAD, NA_HEADS * NA_HEAD, NA_HEADS * NA_HEAD,
    MLA_Q_LORA, MLA_KV_LORA, MLA_ROPE)
IN_WIDTH = sum(IN_SPLITS)

kernel_name = 'hybrid_diffusion_gated_block'


def _rms(x, g):
    xf = x.astype(jnp.float32)
    y = xf * lax.rsqrt(jnp.mean(xf * xf, axis=-1, keepdims=True) + EPS)
    return (y * g.astype(jnp.float32)).astype(x.dtype)


def _modulate(x, g, shift, scale):
    return _rms(x, g) * (1 + scale) + shift


def _split_in(p):
    offs = np.cumsum(np.array(IN_SPLITS))[:-1].tolist()
    return jnp.split(p, offs, axis=-1)


def _rope_tables(n_tok, dim):
    m = dim // 2
    inv = ROPE_BASE ** (-jnp.arange(0, m, 2, dtype=jnp.float32) / m)
    t = jnp.arange(n_tok)
    row = (t // GRID_W).astype(jnp.float32)
    col = (t % GRID_W).astype(jnp.float32)
    ar = row[:, None] * inv
    ac = col[:, None] * inv
    ang = jnp.concatenate([ar, ar, ac, ac], axis=-1)
    return jnp.cos(ang), jnp.sin(ang)


def _rope(x, cos, sin):
    shape = (x.shape[1],) + (1,) * (x.ndim - 3) + (x.shape[-1],)
    a, b, c, d = jnp.split(x, 4, axis=-1)
    rot = jnp.concatenate([-b, a, -d, c], axis=-1)
    return (x * cos.reshape(shape) + rot * sin.reshape(shape)).astype(x.dtype)


def _attn(q, k, v, scale):
    s = jnp.einsum('bhqd,bhkd->bhqk', q, k).astype(jnp.float32) * scale
    p = jax.nn.softmax(s, axis=-1)
    return jnp.einsum('bhqk,bhkd->bhqd', p.astype(v.dtype), v)


def _map_query_blocks(fn, q, axis):
    n = q.shape[axis]
    nb = n // Q_BLOCK
    qb = q.reshape(q.shape[:axis] + (nb, Q_BLOCK) + q.shape[axis + 1:])
    out = lax.map(fn, jnp.moveaxis(qb, axis, 0))
    out = jnp.moveaxis(out, 0, -3)
    return out.reshape(out.shape[:-3] + (n, out.shape[-1]))


def _diff_heads(q, k, v, qk_g, rope):
    B, T, _ = q.shape
    q = _rms(q.reshape(B, T, DA_HEADS, 2, DA_HEAD), qk_g[0])
    k = _rms(k.reshape(B, T, DA_HEADS, 2, DA_HEAD), qk_g[1])
    if rope is not None:
        q = _rope(q, *rope)
        k = _rope(k, *rope)
    v = v.reshape(B, T, DA_HEADS, DA_V)
    return q.transpose(0, 2, 3, 1, 4), k.transpose(0, 2, 3, 1, 4), v.transpose(0, 2, 1, 3)


def _diff_core(q, k, v, lam):
    s = jnp.einsum('bhnqd,bhnkd->bhnqk', q, k).astype(jnp.float32) * (DA_HEAD ** -0.5)
    p = jax.nn.softmax(s, axis=-1)
    a = p[:, :, 0] - lam * p[:, :, 1]
    return jnp.einsum('bhqk,bhkd->bhqd', a.astype(v.dtype), v)


def _mix_diff(ql, kl, vl, qc, kc, vc, lam_vec, qk_g, sub_g, lam_init, rope, ctx_out):
    lv = lam_vec.astype(jnp.float32)
    lam = jnp.exp(jnp.sum(lv[0] * lv[1])) - jnp.exp(jnp.sum(lv[2] * lv[3])) + lam_init
    Ql, Kl, Vl = _diff_heads(ql, kl, vl, qk_g, rope)
    Qc, Kc, Vc = _diff_heads(qc, kc, vc, qk_g, None)
    K = jnp.concatenate([Kc, Kl], axis=3)
    V = jnp.concatenate([Vc, Vl], axis=2)
    yl = _map_query_blocks(lambda qb: _diff_core(qb, K, V, lam), Ql, axis=3)

    def post(y):
        B, H, T, dv = y.shape
        y = _rms(y, sub_g) * (1.0 - lam_init)
        return y.transpose(0, 2, 1, 3).reshape(B, T, H * dv)

    yc = post(_diff_core(Qc, Kc, Vc, lam)) if ctx_out else None
    return post(yl), yc


def _retention_scan(q, k, v, log_g, state0):
    C = RT_CHUNK
    B, H, T, _ = q.shape
    nc = T // C
    n = jnp.arange(C, dtype=jnp.float32)
    diff = n[:, None] - n[None, :]
    dmat = jnp.where(diff >= 0, jnp.exp(log_g[:, None, None] * jnp.maximum(diff, 0.0)), 0.0)
    xi = jnp.exp(log_g[:, None] * (n + 1.0))
    zeta = jnp.exp(log_g[:, None] * (C - 1.0 - n))
    g_c = jnp.exp(log_g * C)

    def chunks(a):
        return jnp.moveaxis(a.reshape(B, H, nc, C, a.shape[-1]), 2, 0)

    def step(R, xs):
        qc, kc, vc = xs
        inner = jnp.einsum('bhnm,bhme->bhne', jnp.einsum('bhnd,bhmd->bhnm', qc, kc) * dmat, vc)
        cross = jnp.einsum('bhnd,bhde->bhne', qc, R) * xi[..., None]
        R = g_c[:, None, None] * R + jnp.einsum('bhmd,bhme->bhde', kc * zeta[..., None], vc)
        return R, inner + cross

    R, out = lax.scan(step, state0, (chunks(q), chunks(k), chunks(v)))
    out = jnp.moveaxis(out, 0, 2).reshape(B, H, T, v.shape[-1])
    return out, R


def _retention_state(k, v, log_g):
    L = k.shape[2]
    m = jnp.arange(L, dtype=jnp.float32)
    w = jnp.exp(log_g[:, None] * (L - 1.0 - m))
    return jnp.einsum('bhmd,bhme->bhde', k * w[..., None], v)


def _ret_heads(q, k, v, rope):
    B, T, _ = q.shape
    q = q.reshape(B, T, RT_HEADS, RT_DK)
    k = k.reshape(B, T, RT_HEADS, RT_DK) * (RT_DK ** -0.5)
    if rope is not None:
        q = _rope(q, *rope)
        k = _rope(k, *rope)
    v = v.reshape(B, T, RT_HEADS, RT_DV)
    return [a.transpose(0, 2, 1, 3).astype(jnp.float32) for a in (q, k, v)]


def _mix_ret(ql, kl, vl, gl, qc, kc, vc, gc, decay_logit, norm_g, rope, ctx_out):
    Ql, Kl, Vl = _ret_heads(ql, kl, vl, rope)
    Qc, Kc, Vc = _ret_heads(qc, kc, vc, None)
    log_g = jax.nn.log_sigmoid(decay_logit.astype(jnp.float32))
    B, H = Ql.shape[:2]
    yl = 0.0
    yc = 0.0
    for d in range(2):
        if d == 0:
            t = lambda a: a
        else:
            t = lambda a: jnp.flip(a, axis=2)
        if ctx_out:
            oc, R = _retention_scan(t(Qc), t(Kc), t(Vc), log_g[d],
                                    jnp.zeros((B, H, RT_DK, RT_DV), jnp.float32))
            yc = yc + t(oc)
        else:
            R = _retention_state(t(Kc), t(Vc), log_g[d])
        ol, _ = _retention_scan(t(Ql), t(Kl), t(Vl), log_g[d], R)
        yl = yl + t(ol)

    def post(y, g):
        Bq, Hq, T, dv = y.shape
        y = _rms(y.transpose(0, 2, 1, 3), norm_g).reshape(Bq, T, Hq * dv)
        return (y * jax.nn.silu(g.astype(jnp.float32))).astype(g.dtype)

    return post(yl, gl), (post(yc, gc) if ctx_out else None)


def _na_heads(q, k, v, qk_g):
    B, T, _ = q.shape
    q = _rms(q.reshape(B, T, NA_HEADS, NA_HEAD), qk_g[0])
    k = _rms(k.reshape(B, T, NA_HEADS, NA_HEAD), qk_g[1])
    return q, k, v.reshape(B, T, NA_HEADS, NA_HEAD)


def _na_latent(q, k, v, kc, vc, rpb, rows):
    B, S, H, d = q.shape
    kh = min(NA_KH, rows)
    scale = NA_HEAD ** -0.5
    qg = q.reshape(B, rows, GRID_W, H, d)
    kg = k.reshape(B, rows, GRID_W, H, d)
    vg = v.reshape(B, rows, GRID_W, H, d)
    cols = np.arange(GRID_W)
    col_start = np.clip(cols - NA_KW // 2, 0, GRID_W - NA_KW)
    col_idx = col_start[:, None] + np.arange(NA_KW)[None, :]
    dc = col_idx - cols[:, None] + (NA_KW - 1)
    rpb_c = rpb[:, :, dc]

    def row_fn(args):
        r, q_row = args
        rs = jnp.clip(r - kh // 2, 0, rows - kh)
        k_win = lax.dynamic_slice_in_dim(kg, rs, kh, axis=1)[:, :, col_idx]
        v_win = lax.dynamic_slice_in_dim(vg, rs, kh, axis=1)[:, :, col_idx]
        dr = rs + jnp.arange(kh) - r + (NA_KH - 1)
        bias = jnp.take(rpb_c, dr, axis=1).transpose(0, 2, 1, 3)
        s_loc = jnp.einsum('bchd,bicjhd->bhcij', q_row, k_win).astype(jnp.float32) * scale
        s_loc = s_loc + bias[None].astype(jnp.float32)
        s_ctx = jnp.einsum('bchd,blhd->bhcl', q_row, kc).astype(jnp.float32) * scale
        n_loc = kh * NA_KW
        s = jnp.concatenate([s_loc.reshape(B, H, GRID_W, n_loc), s_ctx], axis=-1)
        p = jax.nn.softmax(s, axis=-1).astype(v.dtype)
        p_loc = p[..., :n_loc].reshape(B, H, GRID_W, kh, NA_KW)
        p_ctx = p[..., n_loc:]
        return (jnp.einsum('bhcij,bicjhd->bchd', p_loc, v_win)
                + jnp.einsum('bhcl,blhd->bchd', p_ctx, vc))

    out = lax.map(row_fn, (jnp.arange(rows), jnp.moveaxis(qg, 1, 0)))
    return jnp.moveaxis(out, 0, 1).reshape(B, S, H * d)


def _mix_na(ql, kl, vl, qc, kc, vc, qk_g, rpb, rows, ctx_out):
    Ql, Kl, Vl = _na_heads(ql, kl, vl, qk_g)
    Qc, Kc, Vc = _na_heads(qc, kc, vc, qk_g)
    yl = _na_latent(Ql, Kl, Vl, Kc, Vc, rpb, rows)
    yc = None
    if ctx_out:
        B, L = Qc.shape[:2]
        o = _attn(Qc.transpose(0, 2, 1, 3), Kc.transpose(0, 2, 1, 3), Vc.transpose(0, 2, 1, 3),
                  NA_HEAD ** -0.5)
        yc = o.transpose(0, 2, 1, 3).reshape(B, L, NA_HEADS * NA_HEAD)
    return yl, yc


def _mla_heads(cq, ckv, kr, q_norm_g, kv_norm_g, w_uq, w_ukv, qk_g, rope):
    B, T, _ = cq.shape
    q = (_rms(cq, q_norm_g) @ w_uq).reshape(B, T, MLA_HEADS, MLA_NOPE + MLA_ROPE)
    kv = (_rms(ckv, kv_norm_g) @ w_ukv).reshape(B, T, MLA_HEADS, MLA_NOPE + MLA_V)
    k_nope, v = kv[..., :MLA_NOPE], kv[..., MLA_NOPE:]
    k = jnp.concatenate([k_nope, jnp.broadcast_to(kr[:, :, None, :], (B, T, MLA_HEADS, MLA_ROPE))], axis=-1)
    q = _rms(q, qk_g[0])
    k = _rms(k, qk_g[1])
    if rope is not None:
        q = jnp.concatenate([q[..., :MLA_NOPE], _rope(q[..., MLA_NOPE:], *rope)], axis=-1)
        k = jnp.concatenate([k[..., :MLA_NOPE], _rope(k[..., MLA_NOPE:], *rope)], axis=-1)
    return q.transpose(0, 2, 1, 3), k.transpose(0, 2, 1, 3), v.transpose(0, 2, 1, 3)


def _mix_mla(cql, ckvl, krl, cqc, ckvc, krc, q_norm_g, kv_norm_g, w_uq, w_ukv, qk_g, rope, ctx_out):
    scale = (MLA_NOPE + MLA_ROPE) ** -0.5
    Ql, Kl, Vl = _mla_heads(cql, ckvl, krl, q_norm_g, kv_norm_g, w_uq, w_ukv, qk_g, rope)
    Qc, Kc, Vc = _mla_heads(cqc, ckvc, krc, q_norm_g, kv_norm_g, w_uq, w_ukv, qk_g, None)
    K = jnp.concatenate([Kc, Kl], axis=2)
    V = jnp.concatenate([Vc, Vl], axis=2)
    yl = _map_query_blocks(lambda qb: _attn(qb, K, V, scale), Ql, axis=2)

    def post(y):
        B, H, T, dv = y.shape
        return y.transpose(0, 2, 1, 3).reshape(B, T, H * dv)

    yc = post(_attn(Qc, Kc, Vc, scale)) if ctx_out else None
    return post(yl), yc


def _merge(h, ys, w_branch, w_gate, b_gate, w_o):
    m = 0.0
    for n in range(N_BRANCH):
        gate = jax.nn.sigmoid(h @ w_gate[n] + b_gate[n])
        m = m + gate * (ys[n] @ w_branch[n])
    return m @ w_o


def _peer(h, w_query, sub_keys, u, v):
    B, T, D = h.shape
    hb_all = h.reshape(-1, PEER_BLOCK, D)
    kk = PEER_TOPK * PEER_TOPK

    def block(hb):
        q = (hb @ w_query).reshape(PEER_BLOCK, PEER_HEADS, 2, PEER_DQ // 2)
        s = jnp.einsum('phsd,hskd->phsk', q, sub_keys).astype(jnp.float32)
        top_s, top_i = lax.top_k(s, PEER_TOPK)
        cand_s = (top_s[:, :, 0, :, None] + top_s[:, :, 1, None, :]).reshape(PEER_BLOCK, PEER_HEADS, kk)
        cand_i = (top_i[:, :, 0, :, None] * PEER_NKEYS + top_i[:, :, 1, None, :]).reshape(PEER_BLOCK, PEER_HEADS, kk)
        sel_s, pos = lax.top_k(cand_s, PEER_TOPK)
        idx = jnp.take_along_axis(cand_i, pos, axis=-1)
        g = jax.nn.softmax(sel_s, axis=-1)
        ue = u[idx]
        ve = v[idx]
        a = jax.nn.gelu(jnp.einsum('pd,phkd->phk', hb, ue).astype(jnp.float32), approximate=False)
        return jnp.einsum('phk,phkd->pd', (g * a).astype(ve.dtype), ve)

    return lax.map(block, hb_all).reshape(B, T, D)


def _layer(x, xc, c, c_ctx, lp, layer_idx, last, rows, rope_a, rope_r):
    ctx_out = not last
    mod = jax.nn.silu(c) @ lp['w_mod'] + lp['b_mod']
    sh1, sc1, g1, sh2, sc2, g2 = [m[:, None, :] for m in jnp.split(mod, 6, axis=-1)]
    modc = jax.nn.silu(c_ctx) @ lp['w_mod'] + lp['b_mod']
    sh1c, sc1c, g1c, sh2c, sc2c, g2c = jnp.split(modc, 6)

    h = _modulate(x, lp['norm1_g'], sh1, sc1)
    hc = _modulate(xc, lp['norm1_g'], sh1c, sc1c)
    pl = _split_in(h @ lp['w_in'])
    pc = _split_in(hc @ lp['w_in'])

    lam_init = 0.8 - 0.6 * math.exp(-0.3 * layer_idx)
    yA = _mix_diff(pl[0], pl[1], pl[2], pc[0], pc[1], pc[2], lp['diff_lambda'], lp['diff_qk_g'],
                   lp['diff_sub_g'], lam_init, rope_a, ctx_out)
    yB = _mix_ret(pl[3], pl[4], pl[5], pl[6], pc[3], pc[4], pc[5], pc[6], lp['ret_decay'],
                  lp['ret_norm_g'], rope_r, ctx_out)
    yC = _mix_na(pl[7], pl[8], pl[9], pc[7], pc[8], pc[9], lp['na_qk_g'], lp['na_rpb'], rows, ctx_out)
    yD = _mix_mla(pl[10], pl[11], pl[12], pc[10], pc[11], pc[12], lp['mla_q_norm_g'],
                  lp['mla_kv_norm_g'], lp['w_uq'], lp['w_ukv'], lp['mla_qk_g'], rope_a, ctx_out)

    x = x + g1 * _merge(h, [yA[0], yB[0], yC[0], yD[0]], lp['w_branch'], lp['w_gate'], lp['b_gate'], lp['w_o'])
    x = x + g2 * _peer(_modulate(x, lp['norm2_g'], sh2, sc2), lp['peer_w_query'], lp['peer_sub_keys'],
                       lp['peer_u'], lp['peer_v'])
    if last:
        return x, None
    xc = xc + g1c * _merge(hc, [yA[1], yB[1], yC[1], yD[1]], lp['w_branch'], lp['w_gate'], lp['b_gate'], lp['w_o'])
    xc = xc + g2c * _peer(_modulate(xc, lp['norm2_g'], sh2c, sc2c), lp['peer_w_query'], lp['peer_sub_keys'],
                          lp['peer_u'], lp['peer_v'])
    return x, xc


def setup_inputs(seed: int = 0) -> dict:
    key = jax.random.key(seed)
    ks = jax.random.split(key, 32)
    f32 = jnp.float32
    D = D_MODEL

    def nrm(k, shape, scale):
        return jax.random.normal(k, shape, f32) * scale

    def gain(k, shape):
        return 1.0 + 0.02 * jax.random.normal(k, shape, f32)

    base = 1.0 - np.exp(np.linspace(np.log(1.0 / 32), np.log(1.0 / 512), RT_HEADS))
    decay_logit = np.log(base / (1.0 - base)).astype(np.float32)

    return {
        'x': nrm(ks[0], (BATCH, SEQ, D), 1.0),
        'c': nrm(ks[1], (BATCH, D), 1.0),
        'ctx': nrm(ks[2], (BATCH, CTX_LEN, D), 1.0),
        'c_ctx': nrm(ks[3], (D,), 1.0),
        'w_mod': nrm(ks[4], (DEPTH, D, 6 * D), 0.5 * D ** -0.5),
        'b_mod': nrm(ks[5], (DEPTH, 6 * D), 0.01),
        'norm1_g': gain(ks[6], (DEPTH, D)),
        'norm2_g': gain(ks[7], (DEPTH, D)),
        'w_in': nrm(ks[8], (DEPTH, D, IN_WIDTH), D ** -0.5),
        'diff_lambda': nrm(ks[9], (DEPTH, 4, DA_HEAD), 0.1),
        'diff_qk_g': gain(ks[10], (DEPTH, 2, DA_HEAD)),
        'diff_sub_g': gain(ks[11], (DEPTH, DA_V)),
        'ret_decay': jnp.asarray(decay_logit)[None, None, :] + nrm(ks[12], (DEPTH, 2, RT_HEADS), 0.1),
        'ret_norm_g': gain(ks[13], (DEPTH, RT_DV)),
        'na_qk_g': gain(ks[14], (DEPTH, 2, NA_HEAD)),
        'na_rpb': nrm(ks[15], (DEPTH, NA_HEADS, 2 * NA_KH - 1, 2 * NA_KW - 1), 0.02),
        'mla_q_norm_g': gain(ks[16], (DEPTH, MLA_Q_LORA)),
        'mla_kv_norm_g': gain(ks[17], (DEPTH, MLA_KV_LORA)),
        'w_uq': nrm(ks[18], (DEPTH, MLA_Q_LORA, MLA_HEADS * (MLA_NOPE + MLA_ROPE)), MLA_Q_LORA ** -0.5),
        'w_ukv': nrm(ks[19], (DEPTH, MLA_KV_LORA, MLA_HEADS * (MLA_NOPE + MLA_V)), MLA_KV_LORA ** -0.5),
        'mla_qk_g': gain(ks[20], (DEPTH, 2, MLA_NOPE + MLA_ROPE)),
        'w_branch': nrm(ks[21], (DEPTH, N_BRANCH, BRANCH_W, D), BRANCH_W ** -0.5),
        'w_gate': nrm(ks[22], (DEPTH, N_BRANCH, D, D), D ** -0.5),
        'b_gate': nrm(ks[23], (DEPTH, N_BRANCH, D), 0.01),
        'w_o': nrm(ks[24], (DEPTH, D, D), D ** -0.5),
        'peer_w_query': nrm(ks[25], (DEPTH, D, PEER_HEADS * PEER_DQ), D ** -0.5),
        'peer_sub_keys': nrm(ks[26], (DEPTH, PEER_HEADS, 2, PEER_NKEYS, PEER_DQ // 2), (PEER_DQ // 2) ** -0.5),
        'peer_u': nrm(ks[27], (DEPTH, PEER_EXPERTS, D), D ** -0.5),
        'peer_v': nrm(ks[28], (DEPTH, PEER_EXPERTS, D), PEER_HEADS ** -0.5),
    }


def reference(x, c, ctx, c_ctx, w_mod, b_mod, norm1_g, norm2_g, w_in, diff_lambda, diff_qk_g,
              diff_sub_g, ret_decay, ret_norm_g, na_qk_g, na_rpb, mla_q_norm_g, mla_kv_norm_g,
              w_uq, w_ukv, mla_qk_g, w_branch, w_gate, b_gate, w_o, peer_w_query, peer_sub_keys,
              peer_u, peer_v):
    n_tok = x.shape[1]
    rows = n_tok // GRID_W
    rope_a = _rope_tables(n_tok, DA_HEAD)
    rope_r = _rope_tables(n_tok, RT_DK)
    xc = ctx
    for l in range(DEPTH):
        lp = dict(w_mod=w_mod[l], b_mod=b_mod[l], norm1_g=norm1_g[l], norm2_g=norm2_g[l],
                  w_in=w_in[l], diff_lambda=diff_lambda[l], diff_qk_g=diff_qk_g[l],
                  diff_sub_g=diff_sub_g[l], ret_decay=ret_decay[l], ret_norm_g=ret_norm_g[l],
                  na_qk_g=na_qk_g[l], na_rpb=na_rpb[l], mla_q_norm_g=mla_q_norm_g[l],
                  mla_kv_norm_g=mla_kv_norm_g[l], w_uq=w_uq[l], w_ukv=w_ukv[l],
                  mla_qk_g=mla_qk_g[l], w_branch=w_branch[l], w_gate=w_gate[l],
                  b_gate=b_gate[l], w_o=w_o[l], peer_w_query=peer_w_query[l],
                  peer_sub_keys=peer_sub_keys[l], peer_u=peer_u[l], peer_v=peer_v[l])
        x, xc = _layer(x, xc, c, c_ctx, lp, l, l == DEPTH - 1, rows, rope_a, rope_r)
    return x
```

```python
import functools
import math

import jax
import jax.numpy as jnp
import numpy as np
from jax import lax
from jax.experimental import pallas as pl
from jax.experimental.pallas import tpu as pltpu

F32 = jnp.float32
BF16 = jnp.bfloat16

D_MODEL = 2048
CTX_LEN = 256
GRID_W = 64
EPS = 1e-6
ROPE_BASE = 10000.0
N_BRANCH = 4
BRANCH_W = 512
DA_HEADS, DA_HEAD = 4, 64
RT_HEADS, RT_DK, RT_CHUNK = 4, 128, 128
NA_HEADS, NA_HEAD, NA_KH, NA_KW = 8, 64, 8, 16
MLA_HEADS, MLA_Q_LORA, MLA_KV_LORA, MLA_NOPE, MLA_ROPE, MLA_V = 4, 512, 256, 128, 64, 128
PEER_HEADS, PEER_NKEYS, PEER_DQ, PEER_TOPK = 8, 128, 256, 16
PEER_EXPERTS = PEER_NKEYS * PEER_NKEYS

LANES = 128
IN_WIDTH = 5952
IN_PAD = 6144
COL_DA_V = 1024 // LANES
COL_RT_V = 2560 // LANES
COL_RT_G = 3072 // LANES
COL_NA_V = 4608 // LANES
OFF_RT_Q, OFF_RT_K = 1536, 2048
OFF_NA_Q, OFF_NA_K = 3584, 4096
OFF_MLA_CQ, OFF_MLA_CKV, OFF_MLA_KR = 5120, 5632, 5888

TB = 512
TQ = 256
PEER_T = 512
PEER_EC = 512
NEG = -1e30
VMEM_LIMIT = 56 * 1024 * 1024


def _cp(*sem):
    return pltpu.CompilerParams(dimension_semantics=sem, vmem_limit_bytes=VMEM_LIMIT)


def _dot(a, b):
    return jnp.dot(a, b, preferred_element_type=F32)


def _dot_nt(a, b):
    return lax.dot_general(a, b, (((1,), (1,)), ((), ())), preferred_element_type=F32)


def _sigmoid(z):
    return 1.0 / (1.0 + jnp.exp(-z))


def _rms_rows(x, g):
    return x * lax.rsqrt(jnp.mean(x * x, axis=-1, keepdims=True) + EPS) * g


def _mod_kernel(c_ref, w_ref, b_ref, o_ref):
    c = c_ref[...]
    a = (c * _sigmoid(c)).astype(BF16)
    o_ref[...] = _dot(a, w_ref[...].astype(BF16)) + b_ref[...]


def _modulation(c_all, w_mod, b_mod):
    rows, d = c_all.shape
    n = w_mod.shape[1]
    cb = 1024
    return pl.pallas_call(
        _mod_kernel,
        grid=(n // cb,),
        in_specs=[pl.BlockSpec((rows, d), lambda j: (0, 0)),
                  pl.BlockSpec((d, cb), lambda j: (0, j)),
                  pl.BlockSpec((1, cb), lambda j: (0, j))],
        out_specs=pl.BlockSpec((rows, cb), lambda j: (0, j)),
        out_shape=jax.ShapeDtypeStruct((rows, n), F32),
        compiler_params=_cp("parallel"),
        name="adaln_mod",
    )(c_all, w_mod, b_mod.reshape(1, n))


def _inproj_kernel(x_ref, mod_ref, g_ref, w_ref, h_ref, p_ref):
    @pl.when(pl.program_id(1) == 0)
    def _():
        y = _rms_rows(x_ref[...], g_ref[...])
        h_ref[...] = (y * (1.0 + mod_ref[0, 1:2, :]) + mod_ref[0, 0:1, :]).astype(BF16)

    p_ref[...] = _dot(h_ref[...], w_ref[...]).astype(BF16)


def _in_proj(x, mod, g, w_bf, n_rows, mod_idx):
    d = x.shape[1]
    cb = 512
    return pl.pallas_call(
        _inproj_kernel,
        grid=(n_rows // TB, IN_PAD // cb),
        in_specs=[pl.BlockSpec((TB, d), lambda i, j: (i, 0)),
                  pl.BlockSpec((1, 6, d), lambda i, j: (mod_idx(i), 0, 0)),
                  pl.BlockSpec((1, d), lambda i, j: (0, 0)),
                  pl.BlockSpec((d, cb), lambda i, j: (0, j))],
        out_specs=[pl.BlockSpec((TB, d), lambda i, j: (i, 0)),
                   pl.BlockSpec((TB, cb), lambda i, j: (i, j))],
        out_shape=[jax.ShapeDtypeStruct((n_rows, d), BF16),
                   jax.ShapeDtypeStruct((n_rows, IN_PAD), BF16)],
        compiler_params=_cp("parallel", "arbitrary"),
        name="norm1_in_proj",
    )(x, mod, g, w_bf)


def _rot_partner(x, q):
    lane = lax.broadcasted_iota(jnp.int32, x.shape, 1)
    first = (lane & (2 * q - 1)) < q
    return jnp.where(first, pltpu.roll(x, LANES - q, 1), pltpu.roll(x, q, 1))


def _rope(x, cos, sin_signed, q):
    return x * cos + _rot_partner(x, q) * sin_signed


def _group_sumsq(x, gsz):
    x2 = x * x
    hi = x2.astype(BF16)
    lo = (x2 - hi.astype(F32)).astype(BF16)
    r = lax.broadcasted_iota(jnp.int32, (LANES, LANES), 0) // gsz
    c = lax.broadcasted_iota(jnp.int32, (LANES, LANES), 1) // gsz
    ones = jnp.where(r == c, 1.0, 0.0).astype(BF16)
    return _dot(hi, ones) + _dot(lo, ones)


def _prep_kernel(p_ref, cosa_ref, sina_ref, cosr_ref, sinr_ref, gda_ref, gna_ref, gq_ref, gkv_ref,
                 gmq_ref, gmk_ref, wuq_ref, wuk_ref, wuv_ref,
                 qa_ref, ka_ref, qr_ref, kr_ref, qn_ref, kn_ref, qm_ref, km_ref, vm_ref):
    cosa, sina = cosa_ref[...], sina_ref[...]
    cosr, sinr = cosr_ref[...], sinr_ref[...]
    for t in range(4):
        sl = slice(t * LANES, (t + 1) * LANES)
        for off, gi, oref, scale in ((0, 0, qa_ref, DA_HEAD ** -0.5), (512, 1, ka_ref, 1.0)):
            x = p_ref[:, off + t * LANES: off + (t + 1) * LANES].astype(F32)
            y = x * lax.rsqrt(_group_sumsq(x, DA_HEAD) * (1.0 / DA_HEAD) + EPS) * gda_ref[gi:gi + 1, :]
            oref[:, sl] = (_rope(y, cosa, sina, DA_HEAD // 4) * scale).astype(BF16)
        x = p_ref[:, OFF_RT_Q + t * LANES: OFF_RT_Q + (t + 1) * LANES].astype(F32)
        qr_ref[:, sl] = _rope(x, cosr, sinr, RT_DK // 4).astype(BF16)
        x = p_ref[:, OFF_RT_K + t * LANES: OFF_RT_K + (t + 1) * LANES].astype(F32) * (RT_DK ** -0.5)
        kr_ref[:, sl] = _rope(x, cosr, sinr, RT_DK // 4).astype(BF16)
        for off, gi, oref, scale in ((OFF_NA_Q, 0, qn_ref, NA_HEAD ** -0.5), (OFF_NA_K, 1, kn_ref, 1.0)):
            x = p_ref[:, off + t * LANES: off + (t + 1) * LANES].astype(F32)
            y = x * lax.rsqrt(_group_sumsq(x, NA_HEAD) * (1.0 / NA_HEAD) + EPS) * gna_ref[gi:gi + 1, :]
            oref[:, sl] = (y * scale).astype(BF16)
    cq = p_ref[:, OFF_MLA_CQ:OFF_MLA_CQ + MLA_Q_LORA].astype(F32)
    q = _dot(_rms_rows(cq, gq_ref[...]).astype(BF16), wuq_ref[...])
    ckv = p_ref[:, OFF_MLA_CKV:OFF_MLA_CKV + MLA_KV_LORA].astype(F32)
    ckv_n = _rms_rows(ckv, gkv_ref[...]).astype(BF16)
    k_nope = _dot(ckv_n, wuk_ref[...])
    vm_ref[...] = _dot(ckv_n, wuv_ref[...]).astype(BF16)
    kr = p_ref[:, OFF_MLA_KR:OFF_MLA_KR + LANES].astype(F32)
    kr_ss = jnp.sum(kr * kr, axis=-1, keepdims=True)
    inv_n = 1.0 / (MLA_NOPE + MLA_ROPE)
    for h in range(MLA_HEADS):
        a, b = h * 2 * LANES, h * 2 * LANES + LANES
        q0, q1 = q[:, a:b], q[:, b:b + LANES]
        r = lax.rsqrt((jnp.sum(q0 * q0, axis=-1, keepdims=True)
                       + jnp.sum(q1 * q1, axis=-1, keepdims=True)) * inv_n + EPS)
        qm_ref[:, a:b] = (q0 * r * gmq_ref[:, 0:LANES]).astype(BF16)
        qm_ref[:, b:b + LANES] = _rope(q1 * r * gmq_ref[:, LANES:2 * LANES], cosa, sina,
                                       MLA_ROPE // 4).astype(BF16)
        k0 = k_nope[:, h * LANES:(h + 1) * LANES]
        r = lax.rsqrt((jnp.sum(k0 * k0, axis=-1, keepdims=True) + kr_ss) * inv_n + EPS)
        km_ref[:, a:b] = (k0 * r * gmk_ref[:, 0:LANES]).astype(BF16)
        km_ref[:, b:b + LANES] = _rope(kr * r * gmk_ref[:, LANES:2 * LANES], cosa, sina,
                                       MLA_ROPE // 4).astype(BF16)


def _prep(p, tabs, gains, weights, n_rows, rope_idx):
    row = lambda w: pl.BlockSpec((TB, w), lambda i: (i, 0))
    tab = pl.BlockSpec((TB, LANES), lambda i: (rope_idx(i), 0))
    full = lambda a: pl.BlockSpec(a.shape, lambda i: (0,) * a.ndim)
    outs = [512] * 6 + [1024, 1024, 512]
    return pl.pallas_call(
        _prep_kernel,
        grid=(n_rows // TB,),
        in_specs=[row(IN_PAD)] + [tab] * 4 + [full(a) for a in gains] + [full(a) for a in weights],
        out_specs=[row(w) for w in outs],
        out_shape=[jax.ShapeDtypeStruct((n_rows, w), BF16) for w in outs],
        compiler_params=_cp("parallel"),
        name="mixer_prep",
    )(p, *tabs, *gains, *weights)


def _softmax_parts(scores):
    m = functools.reduce(jnp.maximum, [jnp.max(s, axis=-1, keepdims=True) for s in scores])
    es = [jnp.exp(s - m) for s in scores]
    z = functools.reduce(jnp.add, [jnp.sum(e, axis=-1, keepdims=True) for e in es])
    return es, z


def _attn_kernel(*refs, mode, nseg, scale, lam_init):
    q_ref = refs[0]
    ks = [refs[1 + 2 * s] for s in range(nseg)]
    vs = [refs[2 + 2 * s] for s in range(nseg)]
    o_ref = refs[-1]
    q = q_ref[...]
    if mode == "plain":
        es, z = _softmax_parts([_dot_nt(q, k[...]) * scale for k in ks])
        y = functools.reduce(jnp.add, [_dot(e.astype(BF16), v[...]) for e, v in zip(es, vs)])
        o_ref[...] = (y / z).astype(o_ref.dtype)
        return
    lane = lax.broadcasted_iota(jnp.int32, q.shape, 1)
    halves = [jnp.where(lane < LANES // 2, q, jnp.zeros_like(q)),
              jnp.where(lane >= LANES // 2, q, jnp.zeros_like(q))]
    parts = [_softmax_parts([_dot_nt(qh, k[...]) for k in ks]) for qh in halves]
    if mode == "pair":
        outs = [functools.reduce(jnp.add, [_dot(e.astype(BF16), v[...]) for e, v in zip(es, vs)]) / z
                for es, z in parts]
        lo = lax.broadcasted_iota(jnp.int32, outs[0].shape, 1) < LANES // 2
        o_ref[...] = jnp.where(lo, outs[0], outs[1]).astype(o_ref.dtype)
        return
    lam_ref, subg_ref = refs[1 + 2 * nseg], refs[2 + 2 * nseg]
    lv = lam_ref[...]
    lam = (jnp.exp(jnp.sum(lv[0:1] * lv[1:2], axis=-1, keepdims=True))
           - jnp.exp(jnp.sum(lv[2:3] * lv[3:4], axis=-1, keepdims=True)) + lam_init)
    (e0, z0), (e1, z1) = parts
    w0, w1 = 1.0 / z0, lam / z1
    y = functools.reduce(jnp.add, [_dot((a * w0 - b * w1).astype(BF16), v[...])
                                   for a, b, v in zip(e0, e1, vs)])
    o_ref[...] = (_rms_rows(y, subg_ref[...]) * (1.0 - lam_init)).astype(o_ref.dtype)


def _attention(q_arr, k_arr, v_arr, extra, *, mode, heads, dq, v_col0, batch, seq, latent, scale=1.0,
               lam_init=0.0, name):
    nl = batch * seq
    cblk = nl // CTX_LEN
    if latent:
        nq = seq // TQ
        grid = (batch, heads, nq)
        q_spec = pl.BlockSpec((TQ, dq), lambda b, h, j: (b * nq + j, h))
        segs = [(CTX_LEN, lambda b, h, j: (cblk + b, h), lambda b, h, j: (cblk + b, v_col0 + h)),
                (seq, lambda b, h, j: (b, h), lambda b, h, j: (b, v_col0 + h))]
        o_spec = pl.BlockSpec((TQ, LANES), lambda b, h, j: (b * nq + j, h))
        rows = nl
    else:
        grid = (batch, heads, 1)
        q_spec = pl.BlockSpec((CTX_LEN, dq), lambda b, h, j: (cblk + b, h))
        segs = [(CTX_LEN, lambda b, h, j: (cblk + b, h), lambda b, h, j: (cblk + b, v_col0 + h))]
        o_spec = pl.BlockSpec((CTX_LEN, LANES), lambda b, h, j: (b, h))
        rows = batch * CTX_LEN
    in_specs, args = [q_spec], [q_arr]
    for n, kmap, vmap in segs:
        in_specs += [pl.BlockSpec((n, dq), kmap), pl.BlockSpec((n, LANES), vmap)]
        args += [k_arr, v_arr]
    for a in extra:
        in_specs.append(pl.BlockSpec(a.shape, lambda b, h, j: (0, 0)))
        args.append(a)
    return pl.pallas_call(
        functools.partial(_attn_kernel, mode=mode, nseg=len(segs), scale=scale, lam_init=lam_init),
        grid=grid, in_specs=in_specs, out_specs=o_spec,
        out_shape=jax.ShapeDtypeStruct((rows, heads * LANES), BF16),
        compiler_params=_cp("parallel", "parallel", "arbitrary"),
        name=name,
    )(*args)


def _ret_kernel(ql_ref, qc_ref, kl_ref, kc_ref, vl_ref, vc_ref, gl_ref, gc_ref, dmat_ref, xi_ref, zeta_ref,
                gch_ref, ng_ref, yl_ref, yc_ref, accl_ref, accc_ref, *, n_lat, n_ctx, ctx_out):
    C = RT_CHUNK

    def chunk(q, k, v, R, d):
        s = _dot_nt(q, k) * dmat_ref[d, 0]
        inner = _dot(s.astype(BF16), v)
        cross = _dot(q, R.astype(BF16)) * xi_ref[d, 0]
        kz = (k.astype(F32) * zeta_ref[d, 0]).T.astype(BF16)
        return inner + cross, gch_ref[d, 0] * R + _dot(kz, v)

    for d in range(2):
        R = jnp.zeros((RT_DK, RT_DK), F32)
        for j in range(n_ctx):
            c = j if d == 0 else n_ctx - 1 - j
            sl = pl.ds(c * C, C)
            o, R = chunk(qc_ref[sl, :], kc_ref[sl, :], vc_ref[sl, :], R, d)
            if ctx_out:
                if d == 0:
                    accc_ref[sl, :] = o
                else:
                    accc_ref[sl, :] += o

        def body(j, R, d=d):
            c = j if d == 0 else n_lat - 1 - j
            sl = pl.ds(pl.multiple_of(c * C, C), C)
            o, R = chunk(ql_ref[sl, :], kl_ref[sl, :], vl_ref[sl, :], R, d)
            if d == 0:
                accl_ref[sl, :] = o
            else:
                accl_ref[sl, :] += o
            return R

        lax.fori_loop(0, n_lat, body, R)

    def post(acc_ref, g_ref, y_ref):
        g = g_ref[...].astype(F32)
        y_ref[...] = (_rms_rows(acc_ref[...], ng_ref[...]) * (g * _sigmoid(g))).astype(y_ref.dtype)

    post(accl_ref, gl_ref, yl_ref)
    if ctx_out:
        post(accc_ref, gc_ref, yc_ref)
    else:
        yc_ref[...] = jnp.zeros_like(yc_ref)


def _retention(qr, kr, p, tabs, norm_g, *, batch, seq, ctx_out):
    nl = batch * seq
    cblk = nl // CTX_LEN
    lat = lambda c0: pl.BlockSpec((seq, LANES), lambda b, h: (b, c0 + h))
    ctx = lambda c0: pl.BlockSpec((CTX_LEN, LANES), lambda b, h: (cblk + b, c0 + h))
    tab = lambda a: pl.BlockSpec((2, 1) + a.shape[2:], lambda b, h: (0, h, 0, 0))
    return pl.pallas_call(
        functools.partial(_ret_kernel, n_lat=seq // RT_CHUNK, n_ctx=CTX_LEN // RT_CHUNK, ctx_out=ctx_out),
        grid=(batch, RT_HEADS),
        in_specs=[lat(0), ctx(0), lat(0), ctx(0), lat(COL_RT_V), ctx(COL_RT_V), lat(COL_RT_G), ctx(COL_RT_G)]
        + [tab(a) for a in tabs] + [pl.BlockSpec((1, LANES), lambda b, h: (0, 0))],
        out_specs=[pl.BlockSpec((seq, LANES), lambda b, h: (b, h)),
                   pl.BlockSpec((CTX_LEN, LANES), lambda b, h: (b, h))],
        out_shape=[jax.ShapeDtypeStruct((nl, RT_HEADS * LANES), BF16),
                   jax.ShapeDtypeStruct((batch * CTX_LEN, RT_HEADS * LANES), BF16)],
        scratch_shapes=[pltpu.VMEM((seq, LANES), F32), pltpu.VMEM((CTX_LEN, LANES), F32)],
        compiler_params=_cp("parallel", "parallel"),
        name="retention",
    )(qr, qr, kr, kr, p, p, p, p, *tabs, norm_g)


def _na_kernel(q_ref, kl_ref, kc_ref, vl_ref, vc_ref, bias_ref, o_ref, *, rows):
    win = NA_KH * GRID_W
    kc, vc = kc_ref[...], vc_ref[...]

    def row(r, carry):
        rs = jnp.clip(r - NA_KH // 2, 0, rows - NA_KH)
        d0 = rs - r + (NA_KH - 1)
        q = q_ref[pl.ds(pl.multiple_of(r * GRID_W, GRID_W), GRID_W), :]
        ksl = pl.ds(pl.multiple_of(rs * GRID_W, GRID_W), win)
        kw, vw = kl_ref[ksl, :], vl_ref[ksl, :]
        lane = lax.broadcasted_iota(jnp.int32, q.shape, 1)
        outs = []
        for hh in range(2):
            keep = (lane < LANES // 2) if hh == 0 else (lane >= LANES // 2)
            qh = jnp.where(keep, q, jnp.zeros_like(q))
            (el, ec), z = _softmax_parts([_dot_nt(qh, kw) + bias_ref[hh, d0], _dot_nt(qh, kc)])
            outs.append((_dot(el.astype(BF16), vw) + _dot(ec.astype(BF16), vc)) / z)
        lo = lax.broadcasted_iota(jnp.int32, outs[0].shape, 1) < LANES // 2
        o_ref[pl.ds(pl.multiple_of(r * GRID_W, GRID_W), GRID_W), :] = (
            jnp.where(lo, outs[0], outs[1]).astype(o_ref.dtype))
        return carry

    lax.fori_loop(0, rows, row, 0)


def _na_latent(qn, kn, p, bias, *, batch, seq):
    nl = batch * seq
    cblk = nl // CTX_LEN
    hp = NA_HEADS // 2
    return pl.pallas_call(
        functools.partial(_na_kernel, rows=seq // GRID_W),
        grid=(batch, hp),
        in_specs=[pl.BlockSpec((seq, LANES), lambda b, h: (b, h)),
                  pl.BlockSpec((seq, LANES), lambda b, h: (b, h)),
                  pl.BlockSpec((CTX_LEN, LANES), lambda b, h: (cblk + b, h)),
                  pl.BlockSpec((seq, LANES), lambda b, h: (b, COL_NA_V + h)),
                  pl.BlockSpec((CTX_LEN, LANES), lambda b, h: (cblk + b, COL_NA_V + h)),
                  pl.BlockSpec((2, NA_KH, GRID_W, NA_KH * GRID_W), lambda b, h: (h, 0, 0, 0))],
        out_specs=pl.BlockSpec((seq, LANES), lambda b, h: (b, h)),
        out_shape=jax.ShapeDtypeStruct((nl, hp * LANES), BF16),
        compiler_params=_cp("parallel", "parallel"),
        name="neighbourhood_attn",
    )(qn, kn, kn, p, p, bias)


def _merge1_kernel(h_ref, ya_ref, yb_ref, yc_ref, yd_ref, wg_ref, wb_ref, bg_ref, m_ref):
    h = h_ref[...]
    acc = None
    for n, y_ref in enumerate((ya_ref, yb_ref, yc_ref, yd_ref)):
        gate = _sigmoid(_dot(h, wg_ref[n]) + bg_ref[n:n + 1, :])
        term = gate * _dot(y_ref[...], wb_ref[n])
        acc = term if acc is None else acc + term
    m_ref[...] = acc.astype(BF16)


def _merge1(h, ys, wg, wb, bg, n_rows):
    d = h.shape[1]
    cb = 512
    return pl.pallas_call(
        _merge1_kernel,
        grid=(d // cb, n_rows // TB),
        in_specs=[pl.BlockSpec((TB, d), lambda j, i: (i, 0))]
        + [pl.BlockSpec((TB, BRANCH_W), lambda j, i: (i, 0))] * N_BRANCH
        + [pl.BlockSpec((N_BRANCH, d, cb), lambda j, i: (0, 0, j)),
           pl.BlockSpec((N_BRANCH, BRANCH_W, cb), lambda j, i: (0, 0, j)),
           pl.BlockSpec((N_BRANCH, cb), lambda j, i: (0, j))],
        out_specs=pl.BlockSpec((TB, cb), lambda j, i: (i, j)),
        out_shape=jax.ShapeDtypeStruct((n_rows, d), BF16),
        compiler_params=_cp("parallel", "parallel"),
        name="gated_branch_sum",
    )(h, *ys, wg, wb, bg)


def _merge2_kernel(m_ref, wo_ref, x_ref, mod_ref, g_ref, xo_ref, h2t_ref):
    x = x_ref[...] + mod_ref[0, 2:3, :] * _dot(m_ref[...], wo_ref[...])
    xo_ref[...] = x
    h2 = _rms_rows(x, g_ref[...]) * (1.0 + mod_ref[0, 4:5, :]) + mod_ref[0, 3:4, :]
    h2t_ref[...] = h2.T.astype(BF16)


def _merge2(m, wo, x, mod, g, n_rows, mod_idx):
    d = x.shape[1]
    return pl.pallas_call(
        _merge2_kernel,
        grid=(n_rows // TB,),
        in_specs=[pl.BlockSpec((TB, d), lambda i: (i, 0)),
                  pl.BlockSpec((d, d), lambda i: (0, 0)),
                  pl.BlockSpec((TB, d), lambda i: (i, 0)),
                  pl.BlockSpec((1, 6, d), lambda i: (mod_idx(i), 0, 0)),
                  pl.BlockSpec((1, d), lambda i: (0, 0))],
        out_specs=[pl.BlockSpec((TB, d), lambda i: (i, 0)),
                   pl.BlockSpec((d, TB), lambda i: (0, i))],
        out_shape=[jax.ShapeDtypeStruct((n_rows, d), F32),
                   jax.ShapeDtypeStruct((d, n_rows), BF16)],
        compiler_params=_cp("parallel"),
        name="out_proj_residual_norm2",
    )(m, wo, x, mod, g)


def _top_values(x, k):
    tops = []
    for _ in range(k):
        m = jnp.max(x, axis=0, keepdims=True)
        tops.append(m)
        x = jnp.where(x == m, -jnp.inf, x)
    return tops


def _route_kernel(h2t_ref, wqt_ref, sk_ref, s1_ref, s2_ref, e1_ref, e2_ref, tau_ref):
    qt = _dot(wqt_ref[...], h2t_ref[...]).astype(BF16)
    nk = PEER_NKEYS
    taus = []
    for h in range(PEER_HEADS):
        s1 = _dot(sk_ref[2 * h], qt[(2 * h) * nk:(2 * h + 1) * nk, :])
        s2 = _dot(sk_ref[2 * h + 1], qt[(2 * h + 1) * nk:(2 * h + 2) * nk, :])
        t1 = _top_values(s1, PEER_TOPK)
        t2 = jnp.concatenate(_top_values(s2, PEER_TOPK), axis=0)
        cand = jnp.concatenate([a + t2 for a in t1], axis=0)
        sel = _top_values(cand, PEER_TOPK)
        m1, m2 = t1[0], t2[0:1]
        z = functools.reduce(jnp.add, [jnp.exp(c - sel[0]) for c in sel])
        s1_ref[h] = s1
        s2_ref[h] = s2
        e1_ref[h] = jnp.exp(s1 - m1) / z
        e2_ref[h] = jnp.exp(s2 - m2)
        taus.append(sel[-1])
    tau_ref[...] = jnp.concatenate(taus, axis=0)


def _route(h2t, wqt, sk, n_rows):
    d = h2t.shape[0]
    t = 256
    big = lambda: pl.BlockSpec((PEER_HEADS, PEER_NKEYS, t), lambda i: (0, 0, i))
    return pl.pallas_call(
        _route_kernel,
        grid=(n_rows // t,),
        in_specs=[pl.BlockSpec((d, t), lambda i: (0, i)),
                  pl.BlockSpec(wqt.shape, lambda i: (0, 0)),
                  pl.BlockSpec(sk.shape, lambda i: (0, 0, 0))],
        out_specs=[big(), big(), big(), big(), pl.BlockSpec((PEER_HEADS, t), lambda i: (0, i))],
        out_shape=[jax.ShapeDtypeStruct((PEER_HEADS, PEER_NKEYS, n_rows), F32)] * 4
        + [jax.ShapeDtypeStruct((PEER_HEADS, n_rows), F32)],
        compiler_params=_cp("parallel"),
        name="peer_route",
    )(h2t, wqt, sk)


def _experts_kernel(h2t_ref, u_ref, vt_ref, s1_ref, e1_ref, s2_ref, e2_ref, tau_ref, out_ref, a_ref, w_ref):
    c = pl.program_id(1)
    nk = PEER_NKEYS
    a_ref[...] = _dot(u_ref[...], h2t_ref[...])
    for ii in range(PEER_EC // nk):
        for lt in range(PEER_T // LANES):
            ls = slice(lt * LANES, (lt + 1) * LANES)
            g = None
            for h in range(PEER_HEADS):
                tsum = s1_ref[ii, h:h + 1, ls] + s2_ref[h, :, ls]
                term = jnp.where(tsum >= tau_ref[h:h + 1, ls], e2_ref[h, :, ls] * e1_ref[ii, h:h + 1, ls], 0.0)
                g = term if g is None else g + term
            a = a_ref[ii * nk:(ii + 1) * nk, ls]
            act = 0.5 * a * (1.0 + lax.erf(a * (2.0 ** -0.5)))
            w_ref[ii * nk:(ii + 1) * nk, ls] = (g * act).astype(BF16)
    contrib = _dot(vt_ref[...], w_ref[...])

    @pl.when(c == 0)
    def _():
        out_ref[...] = contrib

    @pl.when(c != 0)
    def _():
        out_ref[...] += contrib


def _experts(h2t, u, vt, s1, e1, s2, e2, tau, n_rows):
    d = h2t.shape[0]
    t, ec = PEER_T, PEER_EC
    ni = ec // PEER_NKEYS
    return pl.pallas_call(
        _experts_kernel,
        grid=(n_rows // t, PEER_EXPERTS // ec),
        in_specs=[pl.BlockSpec((d, t), lambda i, c: (0, i)),
                  pl.BlockSpec((ec, d), lambda i, c: (c, 0)),
                  pl.BlockSpec((d, ec), lambda i, c: (0, c)),
                  pl.BlockSpec((ni, PEER_HEADS, t), lambda i, c: (c, 0, i)),
                  pl.BlockSpec((ni, PEER_HEADS, t), lambda i, c: (c, 0, i)),
                  pl.BlockSpec((PEER_HEADS, PEER_NKEYS, t), lambda i, c: (0, 0, i)),
                  pl.BlockSpec((PEER_HEADS, PEER_NKEYS, t), lambda i, c: (0, 0, i)),
                  pl.BlockSpec((PEER_HEADS, t), lambda i, c: (0, i))],
        out_specs=pl.BlockSpec((d, t), lambda i, c: (0, i)),
        out_shape=jax.ShapeDtypeStruct((d, n_rows), F32),
        scratch_shapes=[pltpu.VMEM((ec, t), F32), pltpu.VMEM((ec, t), BF16)],
        compiler_params=_cp("parallel", "arbitrary"),
        name="peer_experts",
    )(h2t, u, vt, s1, e1, s2, e2, tau)


def _resid_kernel(x_ref, pt_ref, mod_ref, o_ref):
    o_ref[...] = x_ref[...] + mod_ref[0, 5:6, :] * pt_ref[...].T


def _peer_residual(x, pt, mod, n_rows, mod_idx):
    d = x.shape[1]
    return pl.pallas_call(
        _resid_kernel,
        grid=(n_rows // TB,),
        in_specs=[pl.BlockSpec((TB, d), lambda i: (i, 0)),
                  pl.BlockSpec((d, TB), lambda i: (0, i)),
                  pl.BlockSpec((1, 6, d), lambda i: (mod_idx(i), 0, 0))],
        out_specs=pl.BlockSpec((TB, d), lambda i: (i, 0)),
        out_shape=jax.ShapeDtypeStruct((n_rows, d), F32),
        compiler_params=_cp("parallel"),
        name="peer_residual",
    )(x, pt, mod)


def _rope_table(n_tok, dim):
    m = dim // 2
    inv = ROPE_BASE ** (-jnp.arange(0, m, 2, dtype=F32) / m)
    t = jnp.arange(n_tok)
    row = (t // GRID_W).astype(F32)
    col = (t % GRID_W).astype(F32)
    ar, ac = row[:, None] * inv, col[:, None] * inv
    ang = jnp.concatenate([ar, ar, ac, ac], axis=-1)
    sign = np.where((np.arange(dim) % (dim // 2)) < dim // 4, -1.0, 1.0).astype(np.float32)
    reps = LANES // dim
    cos = jnp.tile(jnp.cos(ang), (1, reps))
    sin = jnp.tile(jnp.sin(ang) * sign, (1, reps))
    return (jnp.concatenate([cos, jnp.ones((TB, LANES), F32)], axis=0),
            jnp.concatenate([sin, jnp.zeros((TB, LANES), F32)], axis=0))


def _na_bias_table(rpb):
    cols = np.arange(GRID_W)
    start = np.clip(cols - NA_KW // 2, 0, GRID_W - NA_KW)
    valid = (cols[None, :] >= start[:, None]) & (cols[None, :] < start[:, None] + NA_KW)
    dc = np.clip(cols[None, :] - cols[:, None] + (NA_KW - 1), 0, 2 * NA_KW - 2)
    g = jnp.where(valid[None, None], rpb.astype(F32)[:, :, dc], NEG)
    dr = np.arange(NA_KH)[:, None] + np.arange(NA_KH)[None, :]
    t = g[:, dr]
    return t.transpose(0, 1, 3, 2, 4).reshape(NA_HEADS, NA_KH, GRID_W, NA_KH * GRID_W)


def _retention_tables(decay_logit):
    C = RT_CHUNK
    log_g = jax.nn.log_sigmoid(decay_logit.astype(F32))
    n = jnp.arange(C, dtype=F32)
    diff = n[:, None] - n[None, :]
    dm = jnp.where(diff >= 0, jnp.exp(log_g[:, :, None, None] * jnp.maximum(diff, 0.0)), 0.0)
    dmat = jnp.stack([dm[0], jnp.swapaxes(dm[1], -1, -2)])
    xi = jnp.exp(log_g[:, :, None] * (n + 1.0))
    zeta = jnp.exp(log_g[:, :, None] * (C - 1.0 - n))
    xi = jnp.stack([xi[0], xi[1, :, ::-1]])
    zeta = jnp.stack([zeta[0], zeta[1, :, ::-1]])
    bc = lambda a: jnp.broadcast_to(a[..., None], a.shape + (LANES,))
    gch = jnp.broadcast_to(jnp.exp(log_g * C)[:, :, None, None], (2, RT_HEADS, 1, LANES))
    return dmat, bc(xi), bc(zeta), gch


def kernel(x, c, ctx, c_ctx, w_mod, b_mod, norm1_g, norm2_g, w_in, diff_lambda, diff_qk_g, diff_sub_g,
           ret_decay, ret_norm_g, na_qk_g, na_rpb, mla_q_norm_g, mla_kv_norm_g, w_uq, w_ukv, mla_qk_g,
           w_branch, w_gate, b_gate, w_o, peer_w_query, peer_sub_keys, peer_u, peer_v):
    B, S, D = x.shape
    depth = w_mod.shape[0]
    assert D == D_MODEL and ctx.shape[1] == CTX_LEN and S % TB == 0 and (B * CTX_LEN) % TB == 0
    assert S // GRID_W >= NA_KH and S % GRID_W == 0
    NL, NC = B * S, B * CTX_LEN
    NT = NL + NC

    mod_idx = lambda i: jnp.minimum(i * TB // S, B)
    nlb, spb = NL // TB, S // TB
    rope_idx = lambda i: jnp.where(i < nlb, i % spb, spb)

    X = jnp.concatenate([x.reshape(NL, D), ctx.reshape(NC, D)], axis=0)
    c_all = jnp.zeros((16, D), F32).at[:B].set(c).at[B].set(c_ctx)
    cos_a, sin_a = _rope_table(S, DA_HEAD)
    cos_r, sin_r = _rope_table(S, RT_DK)
    tile2 = lambda g: jnp.tile(g.astype(F32), (1, 2))
    pad_qk = lambda g: jnp.pad(g.astype(F32), (0, 2 * LANES - g.shape[0])).reshape(1, 2 * LANES)

    for l in range(depth):
        last = l == depth - 1
        lam_init = 0.8 - 0.6 * math.exp(-0.3 * l)
        n_tok = NL if last else NT

        w_in_bf = jnp.pad(w_in[l], ((0, 0), (0, IN_PAD - IN_WIDTH))).astype(BF16)
        wuq = w_uq[l].reshape(MLA_Q_LORA, MLA_HEADS, MLA_NOPE + MLA_ROPE)
        wuq = jnp.pad(wuq, ((0, 0), (0, 0), (0, 2 * LANES - MLA_NOPE - MLA_ROPE)))
        wuq = wuq.reshape(MLA_Q_LORA, MLA_HEADS * 2 * LANES).astype(BF16)
        wukv = w_ukv[l].reshape(MLA_KV_LORA, MLA_HEADS, MLA_NOPE + MLA_V)
        wuk = wukv[:, :, :MLA_NOPE].reshape(MLA_KV_LORA, MLA_HEADS * MLA_NOPE).astype(BF16)
        wuv = wukv[:, :, MLA_NOPE:].reshape(MLA_KV_LORA, MLA_HEADS * MLA_V).astype(BF16)
        gains = [tile2(diff_qk_g[l]), tile2(na_qk_g[l]), mla_q_norm_g[l].reshape(1, -1),
                 mla_kv_norm_g[l].reshape(1, -1), pad_qk(mla_qk_g[l, 0]), pad_qk(mla_qk_g[l, 1])]
        wg_bf, wb_bf, wo_bf = w_gate[l].astype(BF16), w_branch[l].astype(BF16), w_o[l].astype(BF16)
        wqt = peer_w_query[l].T.astype(BF16)
        sk = peer_sub_keys[l].reshape(PEER_HEADS * 2, PEER_NKEYS, PEER_DQ // 2).astype(BF16)
        u_bf = peer_u[l].astype(BF16)
        vt_bf = peer_v[l].T.astype(BF16)

        mod = _modulation(c_all, w_mod[l], b_mod[l]).reshape(16, 6, D)
        h1, p = _in_proj(X, mod, norm1_g[l].reshape(1, D), w_in_bf, NT, mod_idx)
        qa, ka, qr, kr, qn, kn, qm, km, vm = _prep(
            p, (cos_a, sin_a, cos_r, sin_r), gains, (wuq, wuk, wuv), NT, rope_idx)

        sub_g = diff_sub_g[l].reshape(1, -1)
        mla_scale = (MLA_NOPE + MLA_ROPE) ** -0.5
        ret_tabs = _retention_tables(ret_decay[l])
        common = dict(batch=B, seq=S)
        ya = _attention(qa, ka, p, (diff_lambda[l], sub_g), mode="diff", heads=DA_HEADS, dq=LANES,
                        v_col0=COL_DA_V, latent=True, lam_init=lam_init, name="diff_attn", **common)
        yb, yb_c = _retention(qr, kr, p, ret_tabs, ret_norm_g[l].reshape(1, -1), ctx_out=not last, **common)
        yc = _na_latent(qn, kn, p, _na_bias_table(na_rpb[l]), **common)
        yd = _attention(qm, km, vm, (), mode="plain", heads=MLA_HEADS, dq=2 * LANES, v_col0=0,
                        latent=True, scale=mla_scale, name="latent_attn", **common)
        ys = [ya, yb, yc, yd]
        if not last:
            ya_c = _attention(qa, ka, p, (diff_lambda[l], sub_g), mode="diff", heads=DA_HEADS, dq=LANES,
                              v_col0=COL_DA_V, latent=False, lam_init=lam_init, name="diff_attn_ctx", **common)
            yc_c = _attention(qn, kn, p, (), mode="pair", heads=NA_HEADS // 2, dq=LANES, v_col0=COL_NA_V,
                              latent=False, name="ctx_attn", **common)
            yd_c = _attention(qm, km, vm, (), mode="plain", heads=MLA_HEADS, dq=2 * LANES, v_col0=0,
                              latent=False, scale=mla_scale, name="latent_attn_ctx", **common)
            ys = [jnp.concatenate([a, b], axis=0) for a, b in zip(ys, (ya_c, yb_c, yc_c, yd_c))]

        m = _merge1(h1, ys, wg_bf, wb_bf, b_gate[l], n_tok)
        x_mid, h2t = _merge2(m, wo_bf, X, mod, norm2_g[l].reshape(1, D), n_tok, mod_idx)
        s1, s2, e1, e2, tau = _route(h2t, wqt, sk, n_tok)
        pt = _experts(h2t, u_bf, vt_bf, jnp.transpose(s1, (1, 0, 2)), jnp.transpose(e1, (1, 0, 2)),
                      s2, e2, tau, n_tok)
        X = _peer_residual(x_mid, pt, mod, n_tok, mod_idx)

    return X[:NL].reshape(B, S, D)
```

```python
import functools
import math

import jax
import jax.numpy as jnp
import numpy as np
from jax import lax
from jax.experimental import pallas as pl
from jax.experimental.pallas import tpu as pltpu

F32 = jnp.float32
BF16 = jnp.bfloat16

D_MODEL = 2048
CTX_LEN = 256
GRID_W = 64
EPS = 1e-6
ROPE_BASE = 10000.0
N_BRANCH = 4
BRANCH_W = 512
DA_HEADS, DA_HEAD = 4, 64
RT_HEADS, RT_DK, RT_CHUNK = 4, 128, 128
NA_HEADS, NA_HEAD, NA_KH, NA_KW = 8, 64, 8, 16
MLA_HEADS, MLA_Q_LORA, MLA_KV_LORA, MLA_NOPE, MLA_ROPE, MLA_V = 4, 512, 256, 128, 64, 128
PEER_HEADS, PEER_NKEYS, PEER_DQ, PEER_TOPK = 8, 128, 256, 16
PEER_EXPERTS = PEER_NKEYS * PEER_NKEYS

LANES = 128
IN_WIDTH = 5952
IN_PAD = 6144
COL_DA_V = 1024 // LANES
COL_RT_V = 2560 // LANES
COL_RT_G = 3072 // LANES
COL_NA_V = 4608 // LANES
OFF_RT_Q, OFF_RT_K = 1536, 2048
OFF_NA_Q, OFF_NA_K = 3584, 4096
OFF_MLA_CQ, OFF_MLA_CKV, OFF_MLA_KR = 5120, 5632, 5888

TB = 512
TQ = 256
PEER_T = 512
PEER_EC = 512
NA_GROUP = 4
NA_UNION = NA_GROUP + NA_KH
NEG = -1e30
VMEM_LIMIT = 56 * 1024 * 1024


def _cp(*sem):
    return pltpu.CompilerParams(dimension_semantics=sem, vmem_limit_bytes=VMEM_LIMIT)


def _dot(a, b):
    return jnp.dot(a, b, preferred_element_type=F32)


def _dot_nt(a, b):
    return lax.dot_general(a, b, (((1,), (1,)), ((), ())), preferred_element_type=F32)


def _sigmoid(z):
    return 1.0 / (1.0 + jnp.exp(-z))


def _rms_rows(x, g):
    return x * lax.rsqrt(jnp.mean(x * x, axis=-1, keepdims=True) + EPS) * g


def _mod_kernel(c_ref, w_ref, b_ref, o_ref):
    c = c_ref[...]
    a = (c * _sigmoid(c)).astype(BF16)
    o_ref[...] = _dot(a, w_ref[...].astype(BF16)) + b_ref[...]


def _modulation(c_all, w_mod, b_mod):
    rows, d = c_all.shape
    n = w_mod.shape[1]
    cb = 1024
    return pl.pallas_call(
        _mod_kernel,
        grid=(n // cb,),
        in_specs=[pl.BlockSpec((rows, d), lambda j: (0, 0)),
                  pl.BlockSpec((d, cb), lambda j: (0, j)),
                  pl.BlockSpec((1, cb), lambda j: (0, j))],
        out_specs=pl.BlockSpec((rows, cb), lambda j: (0, j)),
        out_shape=jax.ShapeDtypeStruct((rows, n), F32),
        compiler_params=_cp("parallel"),
        name="adaln_mod",
    )(c_all, w_mod, b_mod.reshape(1, n))


def _inproj_kernel(x_ref, mod_ref, g_ref, w_ref, h_ref, p_ref):
    @pl.when(pl.program_id(1) == 0)
    def _():
        y = _rms_rows(x_ref[...], g_ref[...])
        h_ref[...] = (y * (1.0 + mod_ref[0, 1:2, :]) + mod_ref[0, 0:1, :]).astype(BF16)

    p_ref[...] = _dot(h_ref[...], w_ref[...]).astype(BF16)


def _in_proj(x, mod, g, w_bf, n_rows, mod_idx):
    d = x.shape[1]
    cb = 512
    return pl.pallas_call(
        _inproj_kernel,
        grid=(n_rows // TB, IN_PAD // cb),
        in_specs=[pl.BlockSpec((TB, d), lambda i, j: (i, 0)),
                  pl.BlockSpec((1, 6, d), lambda i, j: (mod_idx(i), 0, 0)),
                  pl.BlockSpec((1, d), lambda i, j: (0, 0)),
                  pl.BlockSpec((d, cb), lambda i, j: (0, j))],
        out_specs=[pl.BlockSpec((TB, d), lambda i, j: (i, 0)),
                   pl.BlockSpec((TB, cb), lambda i, j: (i, j))],
        out_shape=[jax.ShapeDtypeStruct((n_rows, d), BF16),
                   jax.ShapeDtypeStruct((n_rows, IN_PAD), BF16)],
        compiler_params=_cp("parallel", "arbitrary"),
        name="norm1_in_proj",
    )(x, mod, g, w_bf)


def _rot_partner(x, q):
    lane = lax.broadcasted_iota(jnp.int32, x.shape, 1)
    first = (lane & (2 * q - 1)) < q
    return jnp.where(first, pltpu.roll(x, LANES - q, 1), pltpu.roll(x, q, 1))


def _rope(x, cos, sin_signed, q):
    return x * cos + _rot_partner(x, q) * sin_signed


def _group_sumsq(x, gsz):
    x2 = x * x
    hi = x2.astype(BF16)
    lo = (x2 - hi.astype(F32)).astype(BF16)
    r = lax.broadcasted_iota(jnp.int32, (LANES, LANES), 0) // gsz
    c = lax.broadcasted_iota(jnp.int32, (LANES, LANES), 1) // gsz
    ones = jnp.where(r == c, 1.0, 0.0).astype(BF16)
    return _dot(hi, ones) + _dot(lo, ones)


def _prep_kernel(p_ref, cosa_ref, sina_ref, cosr_ref, sinr_ref, gda_ref, gna_ref, gq_ref, gkv_ref,
                 gmq_ref, gmk_ref, wuq_ref, wuk_ref, wuv_ref,
                 qa_ref, ka_ref, qr_ref, kr_ref, qn_ref, kn_ref, qm_ref, km_ref, vm_ref):
    cosa, sina = cosa_ref[...], sina_ref[...]
    cosr, sinr = cosr_ref[...], sinr_ref[...]
    for t in range(4):
        sl = slice(t * LANES, (t + 1) * LANES)
        for off, gi, oref, scale in ((0, 0, qa_ref, DA_HEAD ** -0.5), (512, 1, ka_ref, 1.0)):
            x = p_ref[:, off + t * LANES: off + (t + 1) * LANES].astype(F32)
            y = x * lax.rsqrt(_group_sumsq(x, DA_HEAD) * (1.0 / DA_HEAD) + EPS) * gda_ref[gi:gi + 1, :]
            oref[:, sl] = (_rope(y, cosa, sina, DA_HEAD // 4) * scale).astype(BF16)
        x = p_ref[:, OFF_RT_Q + t * LANES: OFF_RT_Q + (t + 1) * LANES].astype(F32)
        qr_ref[:, sl] = _rope(x, cosr, sinr, RT_DK // 4).astype(BF16)
        x = p_ref[:, OFF_RT_K + t * LANES: OFF_RT_K + (t + 1) * LANES].astype(F32) * (RT_DK ** -0.5)
        kr_ref[:, sl] = _rope(x, cosr, sinr, RT_DK // 4).astype(BF16)
        for off, gi, oref, scale in ((OFF_NA_Q, 0, qn_ref, NA_HEAD ** -0.5), (OFF_NA_K, 1, kn_ref, 1.0)):
            x = p_ref[:, off + t * LANES: off + (t + 1) * LANES].astype(F32)
            y = x * lax.rsqrt(_group_sumsq(x, NA_HEAD) * (1.0 / NA_HEAD) + EPS) * gna_ref[gi:gi + 1, :]
            oref[:, sl] = (y * scale).astype(BF16)
    cq = p_ref[:, OFF_MLA_CQ:OFF_MLA_CQ + MLA_Q_LORA].astype(F32)
    q = _dot(_rms_rows(cq, gq_ref[...]).astype(BF16), wuq_ref[...])
    ckv = p_ref[:, OFF_MLA_CKV:OFF_MLA_CKV + MLA_KV_LORA].astype(F32)
    ckv_n = _rms_rows(ckv, gkv_ref[...]).astype(BF16)
    k_nope = _dot(ckv_n, wuk_ref[...])
    vm_ref[...] = _dot(ckv_n, wuv_ref[...]).astype(BF16)
    kr = p_ref[:, OFF_MLA_KR:OFF_MLA_KR + LANES].astype(F32)
    kr_ss = jnp.sum(kr * kr, axis=-1, keepdims=True)
    inv_n = 1.0 / (MLA_NOPE + MLA_ROPE)
    for h in range(MLA_HEADS):
        a, b = h * 2 * LANES, h * 2 * LANES + LANES
        q0, q1 = q[:, a:b], q[:, b:b + LANES]
        r = lax.rsqrt((jnp.sum(q0 * q0, axis=-1, keepdims=True)
                       + jnp.sum(q1 * q1, axis=-1, keepdims=True)) * inv_n + EPS)
        qm_ref[:, a:b] = (q0 * r * gmq_ref[:, 0:LANES]).astype(BF16)
        qm_ref[:, b:b + LANES] = _rope(q1 * r * gmq_ref[:, LANES:2 * LANES], cosa, sina,
                                       MLA_ROPE // 4).astype(BF16)
        k0 = k_nope[:, h * LANES:(h + 1) * LANES]
        r = lax.rsqrt((jnp.sum(k0 * k0, axis=-1, keepdims=True) + kr_ss) * inv_n + EPS)
        km_ref[:, a:b] = (k0 * r * gmk_ref[:, 0:LANES]).astype(BF16)
        km_ref[:, b:b + LANES] = _rope(kr * r * gmk_ref[:, LANES:2 * LANES], cosa, sina,
                                       MLA_ROPE // 4).astype(BF16)


def _prep(p, tabs, gains, weights, n_rows, rope_idx):
    row = lambda w: pl.BlockSpec((TB, w), lambda i: (i, 0))
    tab = pl.BlockSpec((TB, LANES), lambda i: (rope_idx(i), 0))
    full = lambda a: pl.BlockSpec(a.shape, lambda i: (0,) * a.ndim)
    outs = [512] * 6 + [1024, 1024, 512]
    return pl.pallas_call(
        _prep_kernel,
        grid=(n_rows // TB,),
        in_specs=[row(IN_PAD)] + [tab] * 4 + [full(a) for a in gains] + [full(a) for a in weights],
        out_specs=[row(w) for w in outs],
        out_shape=[jax.ShapeDtypeStruct((n_rows, w), BF16) for w in outs],
        compiler_params=_cp("parallel"),
        name="mixer_prep",
    )(p, *tabs, *gains, *weights)


def _softmax_parts(scores):
    m = functools.reduce(jnp.maximum, [jnp.max(s, axis=-1, keepdims=True) for s in scores])
    es = [jnp.exp(s - m) for s in scores]
    z = functools.reduce(jnp.add, [jnp.sum(e, axis=-1, keepdims=True) for e in es])
    return es, z


def _attn_kernel(*refs, mode, nseg, scale, lam_init):
    q_ref = refs[0]
    ks = [refs[1 + 2 * s] for s in range(nseg)]
    vs = [refs[2 + 2 * s] for s in range(nseg)]
    o_ref = refs[-1]
    q = q_ref[...]
    if mode == "plain":
        es, z = _softmax_parts([_dot_nt(q, k[...]) * scale for k in ks])
        y = functools.reduce(jnp.add, [_dot(e.astype(BF16), v[...]) for e, v in zip(es, vs)])
        o_ref[...] = (y / z).astype(o_ref.dtype)
        return
    lane = lax.broadcasted_iota(jnp.int32, q.shape, 1)
    halves = [jnp.where(lane < LANES // 2, q, jnp.zeros_like(q)),
              jnp.where(lane >= LANES // 2, q, jnp.zeros_like(q))]
    parts = [_softmax_parts([_dot_nt(qh, k[...]) for k in ks]) for qh in halves]
    if mode == "pair":
        outs = [functools.reduce(jnp.add, [_dot(e.astype(BF16), v[...]) for e, v in zip(es, vs)]) / z
                for es, z in parts]
        lo = lax.broadcasted_iota(jnp.int32, outs[0].shape, 1) < LANES // 2
        o_ref[...] = jnp.where(lo, outs[0], outs[1]).astype(o_ref.dtype)
        return
    lam_ref, subg_ref = refs[1 + 2 * nseg], refs[2 + 2 * nseg]
    lv = lam_ref[...]
    lam = (jnp.exp(jnp.sum(lv[0:1] * lv[1:2], axis=-1, keepdims=True))
           - jnp.exp(jnp.sum(lv[2:3] * lv[3:4], axis=-1, keepdims=True)) + lam_init)
    (e0, z0), (e1, z1) = parts
    w0, w1 = 1.0 / z0, lam / z1
    y = functools.reduce(jnp.add, [_dot((a * w0 - b * w1).astype(BF16), v[...])
                                   for a, b, v in zip(e0, e1, vs)])
    o_ref[...] = (_rms_rows(y, subg_ref[...]) * (1.0 - lam_init)).astype(o_ref.dtype)


def _attention(q_arr, k_arr, v_arr, extra, *, mode, heads, dq, v_col0, batch, seq, latent, scale=1.0,
               lam_init=0.0, name):
    nl = batch * seq
    cblk = nl // CTX_LEN
    if latent:
        nq = seq // TQ
        grid = (batch, heads, nq)
        q_spec = pl.BlockSpec((TQ, dq), lambda b, h, j: (b * nq + j, h))
        segs = [(CTX_LEN, lambda b, h, j: (cblk + b, h), lambda b, h, j: (cblk + b, v_col0 + h)),
                (seq, lambda b, h, j: (b, h), lambda b, h, j: (b, v_col0 + h))]
        o_spec = pl.BlockSpec((TQ, LANES), lambda b, h, j: (b * nq + j, h))
        rows = nl
    else:
        grid = (batch, heads, 1)
        q_spec = pl.BlockSpec((CTX_LEN, dq), lambda b, h, j: (cblk + b, h))
        segs = [(CTX_LEN, lambda b, h, j: (cblk + b, h), lambda b, h, j: (cblk + b, v_col0 + h))]
        o_spec = pl.BlockSpec((CTX_LEN, LANES), lambda b, h, j: (b, h))
        rows = batch * CTX_LEN
    in_specs, args = [q_spec], [q_arr]
    for n, kmap, vmap in segs:
        in_specs += [pl.BlockSpec((n, dq), kmap), pl.BlockSpec((n, LANES), vmap)]
        args += [k_arr, v_arr]
    for a in extra:
        in_specs.append(pl.BlockSpec(a.shape, lambda b, h, j: (0, 0)))
        args.append(a)
    return pl.pallas_call(
        functools.partial(_attn_kernel, mode=mode, nseg=len(segs), scale=scale, lam_init=lam_init),
        grid=grid, in_specs=in_specs, out_specs=o_spec,
        out_shape=jax.ShapeDtypeStruct((rows, heads * LANES), BF16),
        compiler_params=_cp("parallel", "parallel", "arbitrary"),
        name=name,
    )(*args)


def _ret_kernel(ql_ref, qc_ref, kl_ref, kc_ref, vl_ref, vc_ref, gl_ref, gc_ref, dmat_ref, xi_ref, zeta_ref,
                gch_ref, ng_ref, yl_ref, yc_ref, accl_ref, accc_ref, *, n_lat, n_ctx, ctx_out):
    C = RT_CHUNK

    def chunk(q, k, v, R, d):
        s = _dot_nt(q, k) * dmat_ref[d, 0]
        inner = _dot(s.astype(BF16), v)
        cross = _dot(q, R.astype(BF16)) * xi_ref[d, 0]
        kz = (k.astype(F32) * zeta_ref[d, 0]).T.astype(BF16)
        return inner + cross, gch_ref[d, 0] * R + _dot(kz, v)

    for d in range(2):
        R = jnp.zeros((RT_DK, RT_DK), F32)
        for j in range(n_ctx):
            c = j if d == 0 else n_ctx - 1 - j
            sl = pl.ds(c * C, C)
            o, R = chunk(qc_ref[sl, :], kc_ref[sl, :], vc_ref[sl, :], R, d)
            if ctx_out:
                if d == 0:
                    accc_ref[sl, :] = o
                else:
                    accc_ref[sl, :] += o

        def body(j, R, d=d):
            c = j if d == 0 else n_lat - 1 - j
            sl = pl.ds(pl.multiple_of(c * C, C), C)
            o, R = chunk(ql_ref[sl, :], kl_ref[sl, :], vl_ref[sl, :], R, d)
            if d == 0:
                accl_ref[sl, :] = o
            else:
                accl_ref[sl, :] += o
            return R

        lax.fori_loop(0, n_lat, body, R)

    def post(acc_ref, g_ref, y_ref):
        g = g_ref[...].astype(F32)
        y_ref[...] = (_rms_rows(acc_ref[...], ng_ref[...]) * (g * _sigmoid(g))).astype(y_ref.dtype)

    post(accl_ref, gl_ref, yl_ref)
    if ctx_out:
        post(accc_ref, gc_ref, yc_ref)
    else:
        yc_ref[...] = jnp.zeros_like(yc_ref)


def _retention(qr, kr, p, tabs, norm_g, *, batch, seq, ctx_out):
    nl = batch * seq
    cblk = nl // CTX_LEN
    lat = lambda c0: pl.BlockSpec((seq, LANES), lambda b, h: (b, c0 + h))
    ctx = lambda c0: pl.BlockSpec((CTX_LEN, LANES), lambda b, h: (cblk + b, c0 + h))
    tab = lambda a: pl.BlockSpec((2, 1) + a.shape[2:], lambda b, h: (0, h, 0, 0))
    return pl.pallas_call(
        functools.partial(_ret_kernel, n_lat=seq // RT_CHUNK, n_ctx=CTX_LEN // RT_CHUNK, ctx_out=ctx_out),
        grid=(batch, RT_HEADS),
        in_specs=[lat(0), ctx(0), lat(0), ctx(0), lat(COL_RT_V), ctx(COL_RT_V), lat(COL_RT_G), ctx(COL_RT_G)]
        + [tab(a) for a in tabs] + [pl.BlockSpec((1, LANES), lambda b, h: (0, 0))],
        out_specs=[pl.BlockSpec((seq, LANES), lambda b, h: (b, h)),
                   pl.BlockSpec((CTX_LEN, LANES), lambda b, h: (b, h))],
        out_shape=[jax.ShapeDtypeStruct((nl, RT_HEADS * LANES), BF16),
                   jax.ShapeDtypeStruct((batch * CTX_LEN, RT_HEADS * LANES), BF16)],
        scratch_shapes=[pltpu.VMEM((seq, LANES), F32), pltpu.VMEM((CTX_LEN, LANES), F32)],
        compiler_params=_cp("parallel", "parallel"),
        name="retention",
    )(qr, qr, kr, kr, p, p, p, p, *tabs, norm_g)


def _na_groups(rows):
    plans, variants = [], []
    for gi in range(rows // NA_GROUP):
        r = gi * NA_GROUP + np.arange(NA_GROUP)
        u0 = int(np.clip(gi * NA_GROUP - NA_KH // 2, 0, rows - NA_UNION))
        rs = np.clip(r - NA_KH // 2, 0, rows - NA_KH)
        krow = u0 + np.arange(NA_UNION)
        valid = (krow[None, :] >= rs[:, None]) & (krow[None, :] < rs[:, None] + NA_KH)
        dr = np.where(valid, krow[None, :] - r[:, None] + (NA_KH - 1), 0)
        key = (valid.tobytes(), dr.tobytes())
        for v, (k, _, _) in enumerate(variants):
            if k == key:
                break
        else:
            v = len(variants)
            variants.append((key, valid, dr))
        plans.append((u0, v))
    return tuple(plans), [(valid, dr) for _, valid, dr in variants]


def _na_kernel(q_ref, kl_ref, kc_ref, vl_ref, vc_ref, bias_ref, o_ref, *, plans):
    gq, gk = NA_GROUP * GRID_W, NA_UNION * GRID_W
    kc, vc = kc_ref[...], vc_ref[...]
    for gi, (u0, var) in enumerate(plans):
        q = q_ref[gi * gq:(gi + 1) * gq, :]
        kw = kl_ref[u0 * GRID_W:u0 * GRID_W + gk, :]
        vw = vl_ref[u0 * GRID_W:u0 * GRID_W + gk, :]
        lane = lax.broadcasted_iota(jnp.int32, q.shape, 1)
        outs = []
        for hh in range(2):
            keep = (lane < LANES // 2) if hh == 0 else (lane >= LANES // 2)
            qh = jnp.where(keep, q, jnp.zeros_like(q))
            (el, ec), z = _softmax_parts([_dot_nt(qh, kw) + bias_ref[hh, var], _dot_nt(qh, kc)])
            outs.append((_dot(el.astype(BF16), vw) + _dot(ec.astype(BF16), vc)) / z)
        lo = lax.broadcasted_iota(jnp.int32, outs[0].shape, 1) < LANES // 2
        o_ref[gi * gq:(gi + 1) * gq, :] = jnp.where(lo, outs[0], outs[1]).astype(o_ref.dtype)


def _na_latent(qn, kn, p, bias, plans, *, batch, seq):
    nl = batch * seq
    cblk = nl // CTX_LEN
    hp = NA_HEADS // 2
    return pl.pallas_call(
        functools.partial(_na_kernel, plans=plans),
        grid=(batch, hp),
        in_specs=[pl.BlockSpec((seq, LANES), lambda b, h: (b, h)),
                  pl.BlockSpec((seq, LANES), lambda b, h: (b, h)),
                  pl.BlockSpec((CTX_LEN, LANES), lambda b, h: (cblk + b, h)),
                  pl.BlockSpec((seq, LANES), lambda b, h: (b, COL_NA_V + h)),
                  pl.BlockSpec((CTX_LEN, LANES), lambda b, h: (cblk + b, COL_NA_V + h)),
                  pl.BlockSpec((2,) + bias.shape[1:], lambda b, h: (h, 0, 0, 0))],
        out_specs=pl.BlockSpec((seq, LANES), lambda b, h: (b, h)),
        out_shape=jax.ShapeDtypeStruct((nl, hp * LANES), BF16),
        compiler_params=_cp("parallel", "parallel"),
        name="neighbourhood_attn",
    )(qn, kn, kn, p, p, bias)


def _merge1_kernel(h_ref, ya_ref, yb_ref, yc_ref, yd_ref, wg_ref, wb_ref, bg_ref, m_ref):
    h = h_ref[...]
    acc = None
    for n, y_ref in enumerate((ya_ref, yb_ref, yc_ref, yd_ref)):
        gate = _sigmoid(_dot(h, wg_ref[n]) + bg_ref[n:n + 1, :])
        term = gate * _dot(y_ref[...], wb_ref[n])
        acc = term if acc is None else acc + term
    m_ref[...] = acc.astype(BF16)


def _merge1(h, ys, wg, wb, bg, n_rows):
    d = h.shape[1]
    cb = 512
    return pl.pallas_call(
        _merge1_kernel,
        grid=(d // cb, n_rows // TB),
        in_specs=[pl.BlockSpec((TB, d), lambda j, i: (i, 0))]
        + [pl.BlockSpec((TB, BRANCH_W), lambda j, i: (i, 0))] * N_BRANCH
        + [pl.BlockSpec((N_BRANCH, d, cb), lambda j, i: (0, 0, j)),
           pl.BlockSpec((N_BRANCH, BRANCH_W, cb), lambda j, i: (0, 0, j)),
           pl.BlockSpec((N_BRANCH, cb), lambda j, i: (0, j))],
        out_specs=pl.BlockSpec((TB, cb), lambda j, i: (i, j)),
        out_shape=jax.ShapeDtypeStruct((n_rows, d), BF16),
        compiler_params=_cp("parallel", "parallel"),
        name="gated_branch_sum",
    )(h, *ys, wg, wb, bg)


def _merge2_kernel(m_ref, wo_ref, x_ref, mod_ref, g_ref, xo_ref, h2t_ref):
    x = x_ref[...] + mod_ref[0, 2:3, :] * _dot(m_ref[...], wo_ref[...])
    xo_ref[...] = x
    h2 = _rms_rows(x, g_ref[...]) * (1.0 + mod_ref[0, 4:5, :]) + mod_ref[0, 3:4, :]
    h2t_ref[...] = h2.T.astype(BF16)


def _merge2(m, wo, x, mod, g, n_rows, mod_idx):
    d = x.shape[1]
    return pl.pallas_call(
        _merge2_kernel,
        grid=(n_rows // TB,),
        in_specs=[pl.BlockSpec((TB, d), lambda i: (i, 0)),
                  pl.BlockSpec((d, d), lambda i: (0, 0)),
                  pl.BlockSpec((TB, d), lambda i: (i, 0)),
                  pl.BlockSpec((1, 6, d), lambda i: (mod_idx(i), 0, 0)),
                  pl.BlockSpec((1, d), lambda i: (0, 0))],
        out_specs=[pl.BlockSpec((TB, d), lambda i: (i, 0)),
                   pl.BlockSpec((d, TB), lambda i: (0, i))],
        out_shape=[jax.ShapeDtypeStruct((n_rows, d), F32),
                   jax.ShapeDtypeStruct((d, n_rows), BF16)],
        compiler_params=_cp("parallel"),
        name="out_proj_residual_norm2",
    )(m, wo, x, mod, g)


def _top_values(x, k):
    tops = []
    for _ in range(k):
        m = jnp.max(x, axis=0, keepdims=True)
        tops.append(m)
        x = jnp.where(x == m, -jnp.inf, x)
    return tops


def _route_kernel(h2t_ref, wqt_ref, sk_ref, s1_ref, th_ref, e1_ref, e2_ref):
    qt = _dot(wqt_ref[...], h2t_ref[...]).astype(BF16)
    nk, k, half = PEER_NKEYS, PEER_TOPK, PEER_TOPK // 2
    for h in range(PEER_HEADS):
        s1 = _dot(sk_ref[2 * h], qt[(2 * h) * nk:(2 * h + 1) * nk, :])
        s2 = _dot(sk_ref[2 * h + 1], qt[(2 * h + 1) * nk:(2 * h + 2) * nk, :])
        t1, t2 = _top_values(s1, k), _top_values(s2, k)
        t1_hi = jnp.concatenate(t1[half:], axis=0)
        t2_lo, t2_hi = jnp.concatenate(t2[:half], axis=0), jnp.concatenate(t2[half:], axis=0)
        cand = jnp.concatenate([t1[a] + t2_lo for a in range(half)] + [t1[0] + t2_hi, t1_hi + t2[0]], axis=0)
        sel = _top_values(cand, k)
        tau = sel[-1]
        z = functools.reduce(jnp.add, [jnp.exp(c - sel[0]) for c in sel])
        th = jnp.full(s2.shape, jnp.inf, F32)
        for a in range(half):
            th = jnp.where(t1[a] + s2 >= tau, t1[a], th)
        th0 = jnp.full(tau.shape, jnp.inf, F32)
        for a in range(half, k):
            th0 = jnp.where(t1[a] + t2[0] >= tau, t1[a], th0)
        th = jnp.where(s2 == t2[0], jnp.minimum(th, th0), th)
        s1_ref[h] = s1
        th_ref[h] = th
        e1_ref[h] = jnp.exp(s1 - t1[0]) / z
        e2_ref[h] = jnp.exp(s2 - t2[0])


def _route(h2t, wqt, sk, n_rows):
    d = h2t.shape[0]
    t = 256
    big = pl.BlockSpec((PEER_HEADS, PEER_NKEYS, t), lambda i: (0, 0, i))
    return pl.pallas_call(
        _route_kernel,
        grid=(n_rows // t,),
        in_specs=[pl.BlockSpec((d, t), lambda i: (0, i)),
                  pl.BlockSpec(wqt.shape, lambda i: (0, 0)),
                  pl.BlockSpec(sk.shape, lambda i: (0, 0, 0))],
        out_specs=[big] * 4,
        out_shape=[jax.ShapeDtypeStruct((PEER_HEADS, PEER_NKEYS, n_rows), F32)] * 4,
        compiler_params=_cp("parallel"),
        name="peer_route",
    )(h2t, wqt, sk)


def _experts_kernel(h2t_ref, u_ref, vt_ref, s1_ref, e1_ref, th_ref, e2_ref, out_ref,
                    a_even, a_odd, w_even, w_odd, *, n_items):
    s = pl.program_id(0)
    nk = PEER_NKEYS
    ni = PEER_EC // nk
    nchunk = PEER_EXPERTS // PEER_EC

    @pl.when(s == 0)
    def _():
        a_odd[...] = jnp.zeros_like(a_odd)
        w_odd[...] = jnp.zeros_like(w_odd)

    @pl.when((s == 0) | ((s >= 2) & ((s - 2) % nchunk == 0)))
    def _():
        out_ref[...] = jnp.zeros_like(out_ref)

    half = jnp.where((s >= 1) & (s <= n_items), 0.5, 0.0)

    def gate_block(a_ref, w_ref, lt, jg):
        ls = slice(lt * LANES, (lt + 1) * LANES)
        js = slice(jg * 16, (jg + 1) * 16)
        gs = [None] * ni
        for h in range(PEER_HEADS):
            th, e2 = th_ref[h, js, ls], e2_ref[h, js, ls]
            for ii in range(ni):
                term = jnp.where(s1_ref[ii, h:h + 1, ls] >= th, e2 * e1_ref[ii, h:h + 1, ls], 0.0)
                gs[ii] = term if gs[ii] is None else gs[ii] + term
        for ii in range(ni):
            rows = slice(ii * nk + jg * 16, ii * nk + (jg + 1) * 16)
            a = a_ref[rows, ls]
            act = half * a * (1.0 + lax.erf(a * (2.0 ** -0.5)))
            w_ref[rows, ls] = (gs[ii] * act).astype(BF16)

    def step(a_new, a_old, w_new, w_old):
        d, t = out_ref.shape
        ec = PEER_EC

        def out_unit(m, n_m=8):
            rs = slice(m * d // n_m, (m + 1) * d // n_m)
            out_ref[rs, :] += _dot(vt_ref[rs, :], w_old[...])

        def pre_unit(q):
            rs = slice((q // 2) * ec // 2, (q // 2 + 1) * ec // 2)
            cs = slice((q % 2) * t // 2, (q % 2 + 1) * t // 2)
            a_new[rs, cs] = _dot(u_ref[rs, :], h2t_ref[:, cs])

        blocks = [(lt, jg) for lt in range(t // LANES) for jg in range(nk // 16)]
        units = [functools.partial(out_unit, m) for m in range(8)] + [functools.partial(pre_unit, q) for q in range(4)]
        per_unit = [2] * 8 + [4] * 4
        k = 0
        for unit, nb in zip(units, per_unit):
            unit()
            for lt, jg in blocks[k:k + nb]:
                gate_block(a_old, w_new, lt, jg)
            k += nb

    @pl.when(s % 2 == 0)
    def _():
        step(a_even, a_odd, w_even, w_odd)

    @pl.when(s % 2 == 1)
    def _():
        step(a_odd, a_even, w_odd, w_even)


def _experts(h2t, u, vt, s1, e1, th, e2, n_rows):
    d = h2t.shape[0]
    t, ec = PEER_T, PEER_EC
    nchunk = PEER_EXPERTS // ec
    n_items = (n_rows // t) * nchunk
    item = lambda s, lag: jnp.clip(s - lag, 0, n_items - 1)
    res = lambda a: pl.BlockSpec(a.shape[:2] + (t,), lambda s: (0, 0, item(s, 1) // nchunk))
    row = lambda a: pl.BlockSpec((ec // PEER_NKEYS, PEER_HEADS, t),
                                 lambda s: (item(s, 1) % nchunk, 0, item(s, 1) // nchunk))
    return pl.pallas_call(
        functools.partial(_experts_kernel, n_items=n_items),
        grid=(n_items + 2,),
        in_specs=[pl.BlockSpec((d, t), lambda s: (0, item(s, 0) // nchunk)),
                  pl.BlockSpec((ec, d), lambda s: (item(s, 0) % nchunk, 0)),
                  pl.BlockSpec((d, ec), lambda s: (0, item(s, 2) % nchunk)),
                  row(s1), row(e1), res(th), res(e2)],
        out_specs=pl.BlockSpec((d, t), lambda s: (0, item(s, 2) // nchunk)),
        out_shape=jax.ShapeDtypeStruct((d, n_rows), F32),
        scratch_shapes=[pltpu.VMEM((ec, t), F32), pltpu.VMEM((ec, t), F32),
                        pltpu.VMEM((ec, t), BF16), pltpu.VMEM((ec, t), BF16)],
        compiler_params=_cp("arbitrary"),
        name="peer_experts",
    )(h2t, u, vt, s1, e1, th, e2)


def _resid_kernel(x_ref, pt_ref, mod_ref, o_ref):
    o_ref[...] = x_ref[...] + mod_ref[0, 5:6, :] * pt_ref[...].T


def _peer_residual(x, pt, mod, n_rows, mod_idx):
    d = x.shape[1]
    return pl.pallas_call(
        _resid_kernel,
        grid=(n_rows // TB,),
        in_specs=[pl.BlockSpec((TB, d), lambda i: (i, 0)),
                  pl.BlockSpec((d, TB), lambda i: (0, i)),
                  pl.BlockSpec((1, 6, d), lambda i: (mod_idx(i), 0, 0))],
        out_specs=pl.BlockSpec((TB, d), lambda i: (i, 0)),
        out_shape=jax.ShapeDtypeStruct((n_rows, d), F32),
        compiler_params=_cp("parallel"),
        name="peer_residual",
    )(x, pt, mod)


def _rope_table(n_tok, dim):
    m = dim // 2
    inv = ROPE_BASE ** (-jnp.arange(0, m, 2, dtype=F32) / m)
    t = jnp.arange(n_tok)
    row = (t // GRID_W).astype(F32)
    col = (t % GRID_W).astype(F32)
    ar, ac = row[:, None] * inv, col[:, None] * inv
    ang = jnp.concatenate([ar, ar, ac, ac], axis=-1)
    sign = np.where((np.arange(dim) % (dim // 2)) < dim // 4, -1.0, 1.0).astype(np.float32)
    reps = LANES // dim
    cos = jnp.tile(jnp.cos(ang), (1, reps))
    sin = jnp.tile(jnp.sin(ang) * sign, (1, reps))
    return (jnp.concatenate([cos, jnp.ones((TB, LANES), F32)], axis=0),
            jnp.concatenate([sin, jnp.zeros((TB, LANES), F32)], axis=0))


def _na_bias_table(rpb, variants):
    cols = np.arange(GRID_W)
    start = np.clip(cols - NA_KW // 2, 0, GRID_W - NA_KW)
    valid_c = (cols[None, :] >= start[:, None]) & (cols[None, :] < start[:, None] + NA_KW)
    dc = np.clip(cols[None, :] - cols[:, None] + (NA_KW - 1), 0, 2 * NA_KW - 2)
    g = rpb.astype(F32)[:, :, dc]
    tabs = []
    for valid_r, dr in variants:
        ok = valid_r[None, :, :, None, None] & valid_c[None, None, None]
        t = jnp.where(ok, g[:, dr], NEG).transpose(0, 1, 3, 2, 4)
        tabs.append(t.reshape(NA_HEADS, NA_GROUP * GRID_W, NA_UNION * GRID_W))
    return jnp.stack(tabs, axis=1)


def _retention_tables(decay_logit):
    C = RT_CHUNK
    log_g = jax.nn.log_sigmoid(decay_logit.astype(F32))
    n = jnp.arange(C, dtype=F32)
    diff = n[:, None] - n[None, :]
    dm = jnp.where(diff >= 0, jnp.exp(log_g[:, :, None, None] * jnp.maximum(diff, 0.0)), 0.0)
    dmat = jnp.stack([dm[0], jnp.swapaxes(dm[1], -1, -2)])
    xi = jnp.exp(log_g[:, :, None] * (n + 1.0))
    zeta = jnp.exp(log_g[:, :, None] * (C - 1.0 - n))
    xi = jnp.stack([xi[0], xi[1, :, ::-1]])
    zeta = jnp.stack([zeta[0], zeta[1, :, ::-1]])
    bc = lambda a: jnp.broadcast_to(a[..., None], a.shape + (LANES,))
    gch = jnp.broadcast_to(jnp.exp(log_g * C)[:, :, None, None], (2, RT_HEADS, 1, LANES))
    return dmat, bc(xi), bc(zeta), gch


def kernel(x, c, ctx, c_ctx, w_mod, b_mod, norm1_g, norm2_g, w_in, diff_lambda, diff_qk_g, diff_sub_g,
           ret_decay, ret_norm_g, na_qk_g, na_rpb, mla_q_norm_g, mla_kv_norm_g, w_uq, w_ukv, mla_qk_g,
           w_branch, w_gate, b_gate, w_o, peer_w_query, peer_sub_keys, peer_u, peer_v):
    B, S, D = x.shape
    depth = w_mod.shape[0]
    assert D == D_MODEL and ctx.shape[1] == CTX_LEN and S % TB == 0 and (B * CTX_LEN) % TB == 0
    assert S % (GRID_W * NA_GROUP) == 0 and S // GRID_W >= NA_UNION
    NL, NC = B * S, B * CTX_LEN
    NT = NL + NC

    mod_idx = lambda i: jnp.minimum(i * TB // S, B)
    nlb, spb = NL // TB, S // TB
    rope_idx = lambda i: jnp.where(i < nlb, i % spb, spb)

    X = jnp.concatenate([x.reshape(NL, D), ctx.reshape(NC, D)], axis=0)
    c_all = jnp.zeros((16, D), F32).at[:B].set(c).at[B].set(c_ctx)
    cos_a, sin_a = _rope_table(S, DA_HEAD)
    cos_r, sin_r = _rope_table(S, RT_DK)
    na_plans, na_variants = _na_groups(S // GRID_W)
    tile2 = lambda g: jnp.tile(g.astype(F32), (1, 2))
    pad_qk = lambda g: jnp.pad(g.astype(F32), (0, 2 * LANES - g.shape[0])).reshape(1, 2 * LANES)

    for l in range(depth):
        last = l == depth - 1
        lam_init = 0.8 - 0.6 * math.exp(-0.3 * l)
        n_tok = NL if last else NT

        w_in_bf = jnp.pad(w_in[l], ((0, 0), (0, IN_PAD - IN_WIDTH))).astype(BF16)
        wuq = w_uq[l].reshape(MLA_Q_LORA, MLA_HEADS, MLA_NOPE + MLA_ROPE)
        wuq = jnp.pad(wuq, ((0, 0), (0, 0), (0, 2 * LANES - MLA_NOPE - MLA_ROPE)))
        wuq = wuq.reshape(MLA_Q_LORA, MLA_HEADS * 2 * LANES).astype(BF16)
        wukv = w_ukv[l].reshape(MLA_KV_LORA, MLA_HEADS, MLA_NOPE + MLA_V)
        wuk = wukv[:, :, :MLA_NOPE].reshape(MLA_KV_LORA, MLA_HEADS * MLA_NOPE).astype(BF16)
        wuv = wukv[:, :, MLA_NOPE:].reshape(MLA_KV_LORA, MLA_HEADS * MLA_V).astype(BF16)
        gains = [tile2(diff_qk_g[l]), tile2(na_qk_g[l]), mla_q_norm_g[l].reshape(1, -1),
                 mla_kv_norm_g[l].reshape(1, -1), pad_qk(mla_qk_g[l, 0]), pad_qk(mla_qk_g[l, 1])]
        wg_bf, wb_bf, wo_bf = w_gate[l].astype(BF16), w_branch[l].astype(BF16), w_o[l].astype(BF16)
        wqt = peer_w_query[l].T.astype(BF16)
        sk = peer_sub_keys[l].reshape(PEER_HEADS * 2, PEER_NKEYS, PEER_DQ // 2).astype(BF16)
        u_bf = peer_u[l].astype(BF16)
        vt_bf = peer_v[l].T.astype(BF16)

        mod = _modulation(c_all, w_mod[l], b_mod[l]).reshape(16, 6, D)
        h1, p = _in_proj(X, mod, norm1_g[l].reshape(1, D), w_in_bf, NT, mod_idx)
        qa, ka, qr, kr, qn, kn, qm, km, vm = _prep(
            p, (cos_a, sin_a, cos_r, sin_r), gains, (wuq, wuk, wuv), NT, rope_idx)

        sub_g = diff_sub_g[l].reshape(1, -1)
        mla_scale = (MLA_NOPE + MLA_ROPE) ** -0.5
        ret_tabs = _retention_tables(ret_decay[l])
        common = dict(batch=B, seq=S)
        ya = _attention(qa, ka, p, (diff_lambda[l], sub_g), mode="diff", heads=DA_HEADS, dq=LANES,
                        v_col0=COL_DA_V, latent=True, lam_init=lam_init, name="diff_attn", **common)
        yb, yb_c = _retention(qr, kr, p, ret_tabs, ret_norm_g[l].reshape(1, -1), ctx_out=not last, **common)
        yc = _na_latent(qn, kn, p, _na_bias_table(na_rpb[l], na_variants), na_plans, **common)
        yd = _attention(qm, km, vm, (), mode="plain", heads=MLA_HEADS, dq=2 * LANES, v_col0=0,
                        latent=True, scale=mla_scale, name="latent_attn", **common)
        ys = [ya, yb, yc, yd]
        if not last:
            ya_c = _attention(qa, ka, p, (diff_lambda[l], sub_g), mode="diff", heads=DA_HEADS, dq=LANES,
                              v_col0=COL_DA_V, latent=False, lam_init=lam_init, name="diff_attn_ctx", **common)
            yc_c = _attention(qn, kn, p, (), mode="pair", heads=NA_HEADS // 2, dq=LANES, v_col0=COL_NA_V,
                              latent=False, name="ctx_attn", **common)
            yd_c = _attention(qm, km, vm, (), mode="plain", heads=MLA_HEADS, dq=2 * LANES, v_col0=0,
                              latent=False, scale=mla_scale, name="latent_attn_ctx", **common)
            ys = [jnp.concatenate([a, b], axis=0) for a, b in zip(ys, (ya_c, yb_c, yc_c, yd_c))]

        m = _merge1(h1, ys, wg_bf, wb_bf, b_gate[l], n_tok)
        x_mid, h2t = _merge2(m, wo_bf, X, mod, norm2_g[l].reshape(1, D), n_tok, mod_idx)
        s1, th, e1, e2 = _route(h2t, wqt, sk, n_tok)
        pt = _experts(h2t, u_bf, vt_bf, jnp.transpose(s1, (1, 0, 2)), jnp.transpose(e1, (1, 0, 2)),
                      th, e2, n_tok)
        X = _peer_residual(x_mid, pt, mod, n_tok, mod_idx)

    return X[:NL].reshape(B, S, D)
```

```python
import functools
import math

import jax
import jax.numpy as jnp
import numpy as np
from jax import lax
from jax.experimental import pallas as pl
from jax.experimental.pallas import tpu as pltpu

F32 = jnp.float32
BF16 = jnp.bfloat16

D_MODEL = 2048
CTX_LEN = 256
GRID_W = 64
EPS = 1e-6
ROPE_BASE = 10000.0
N_BRANCH = 4
BRANCH_W = 512
DA_HEADS, DA_HEAD = 4, 64
RT_HEADS, RT_DK, RT_CHUNK = 4, 128, 128
NA_HEADS, NA_HEAD, NA_KH, NA_KW = 8, 64, 8, 16
MLA_HEADS, MLA_Q_LORA, MLA_KV_LORA, MLA_NOPE, MLA_ROPE, MLA_V = 4, 512, 256, 128, 64, 128
PEER_HEADS, PEER_NKEYS, PEER_DQ, PEER_TOPK = 8, 128, 256, 16
PEER_EXPERTS = PEER_NKEYS * PEER_NKEYS

LANES = 128
IN_WIDTH = 5952
IN_PAD = 6144
COL_DA_V = 1024 // LANES
COL_RT_V = 2560 // LANES
COL_RT_G = 3072 // LANES
COL_NA_V = 4608 // LANES
OFF_RT_Q, OFF_RT_K = 1536, 2048
OFF_NA_Q, OFF_NA_K = 3584, 4096
OFF_MLA_CQ, OFF_MLA_CKV, OFF_MLA_KR = 5120, 5632, 5888

TB = 512
TB_IN = 1024
TQ = 256
PEER_T = 512
PEER_EC = 1024
NA_GROUP = 4
NA_UNION = NA_GROUP + NA_KH
NEG = -1e30
VMEM_LIMIT = 56 * 1024 * 1024


def _cp(*sem):
    return pltpu.CompilerParams(dimension_semantics=sem, vmem_limit_bytes=VMEM_LIMIT)


def _dot(a, b):
    return jnp.dot(a, b, preferred_element_type=F32)


def _dot_nt(a, b):
    return lax.dot_general(a, b, (((1,), (1,)), ((), ())), preferred_element_type=F32)


def _sigmoid(z):
    return 1.0 / (1.0 + jnp.exp(-z))


def _rms_rows(x, g):
    return x * lax.rsqrt(jnp.mean(x * x, axis=-1, keepdims=True) + EPS) * g


def _mod_kernel(c_ref, w_ref, b_ref, o_ref):
    c = c_ref[...]
    a = (c * _sigmoid(c)).astype(BF16)
    o_ref[...] = _dot(a, w_ref[...].astype(BF16)) + b_ref[...]


def _modulation(c_all, w_mod, b_mod):
    rows, d = c_all.shape
    n = w_mod.shape[1]
    cb = 1024
    return pl.pallas_call(
        _mod_kernel,
        grid=(n // cb,),
        in_specs=[pl.BlockSpec((rows, d), lambda j: (0, 0)),
                  pl.BlockSpec((d, cb), lambda j: (0, j)),
                  pl.BlockSpec((1, cb), lambda j: (0, j))],
        out_specs=pl.BlockSpec((rows, cb), lambda j: (0, j)),
        out_shape=jax.ShapeDtypeStruct((rows, n), F32),
        compiler_params=_cp("parallel"),
        name="adaln_mod",
    )(c_all, w_mod, b_mod.reshape(1, n))


def _split_rows(n_a, b_off):
    return (lambda i, *_: (jnp.minimum(i, n_a - 1), 0),
            lambda i, *_: (jnp.maximum(i - n_a, 0) + b_off, 0))


def _inproj_kernel(xa_ref, xb_ref, mod_ref, g_ref, w_ref, h_ref, p_ref, *, n_a):
    def prologue(x_ref):
        y = _rms_rows(x_ref[...], g_ref[...])
        h_ref[...] = (y * (1.0 + mod_ref[0, 1:2, :]) + mod_ref[0, 0:1, :]).astype(BF16)

    first = pl.program_id(1) == 0
    pl.when(first & (pl.program_id(0) < n_a))(functools.partial(prologue, xa_ref))
    pl.when(first & (pl.program_id(0) >= n_a))(functools.partial(prologue, xb_ref))
    p_ref[...] = _dot(h_ref[...], w_ref[...]).astype(BF16)


def _in_proj(xa, xb, xb_row0, n_lat, mod, g, w_bf, n_rows, mod_idx):
    d = xa.shape[1]
    cb = 512
    amap, bmap = _split_rows(n_lat // TB_IN, xb_row0 // TB_IN)
    return pl.pallas_call(
        functools.partial(_inproj_kernel, n_a=n_lat // TB_IN),
        grid=(n_rows // TB_IN, IN_PAD // cb),
        in_specs=[pl.BlockSpec((TB_IN, d), amap),
                  pl.BlockSpec((TB_IN, d), bmap),
                  pl.BlockSpec((1, 6, d), lambda i, j: (mod_idx(i), 0, 0)),
                  pl.BlockSpec((1, d), lambda i, j: (0, 0)),
                  pl.BlockSpec((d, cb), lambda i, j: (0, j))],
        out_specs=[pl.BlockSpec((TB_IN, d), lambda i, j: (i, 0)),
                   pl.BlockSpec((TB_IN, cb), lambda i, j: (i, j))],
        out_shape=[jax.ShapeDtypeStruct((n_rows, d), BF16),
                   jax.ShapeDtypeStruct((n_rows, IN_PAD), BF16)],
        compiler_params=_cp("parallel", "arbitrary"),
        name="norm1_in_proj",
    )(xa, xb, mod, g, w_bf)


def _rot_partner(x, q):
    lane = lax.broadcasted_iota(jnp.int32, x.shape, 1)
    first = (lane & (2 * q - 1)) < q
    return jnp.where(first, pltpu.roll(x, LANES - q, 1), pltpu.roll(x, q, 1))


def _rope(x, cos, sin_signed, q):
    return x * cos + _rot_partner(x, q) * sin_signed


def _group_sumsq(x, gsz):
    x2 = x * x
    hi = x2.astype(BF16)
    lo = (x2 - hi.astype(F32)).astype(BF16)
    r = lax.broadcasted_iota(jnp.int32, (LANES, LANES), 0) // gsz
    c = lax.broadcasted_iota(jnp.int32, (LANES, LANES), 1) // gsz
    ones = jnp.where(r == c, 1.0, 0.0).astype(BF16)
    return _dot(hi, ones) + _dot(lo, ones)


def _prep_kernel(p_ref, cosa_ref, sina_ref, cosr_ref, sinr_ref, gda_ref, gna_ref, gq_ref, gkv_ref,
                 gmq_ref, gmk_ref, wuq_ref, wuk_ref, wuv_ref,
                 qa_ref, ka_ref, qr_ref, kr_ref, qn_ref, kn_ref, qm_ref, km_ref, vm_ref):
    cosa, sina = cosa_ref[...], sina_ref[...]
    cosr, sinr = cosr_ref[...], sinr_ref[...]
    for t in range(4):
        sl = slice(t * LANES, (t + 1) * LANES)
        for off, gi, oref, scale in ((0, 0, qa_ref, DA_HEAD ** -0.5), (512, 1, ka_ref, 1.0)):
            x = p_ref[:, off + t * LANES: off + (t + 1) * LANES].astype(F32)
            y = x * lax.rsqrt(_group_sumsq(x, DA_HEAD) * (1.0 / DA_HEAD) + EPS) * gda_ref[gi:gi + 1, :]
            oref[:, sl] = (_rope(y, cosa, sina, DA_HEAD // 4) * scale).astype(BF16)
        x = p_ref[:, OFF_RT_Q + t * LANES: OFF_RT_Q + (t + 1) * LANES].astype(F32)
        qr_ref[:, sl] = _rope(x, cosr, sinr, RT_DK // 4).astype(BF16)
        x = p_ref[:, OFF_RT_K + t * LANES: OFF_RT_K + (t + 1) * LANES].astype(F32) * (RT_DK ** -0.5)
        kr_ref[:, sl] = _rope(x, cosr, sinr, RT_DK // 4).astype(BF16)
        for off, gi, oref, scale in ((OFF_NA_Q, 0, qn_ref, NA_HEAD ** -0.5), (OFF_NA_K, 1, kn_ref, 1.0)):
            x = p_ref[:, off + t * LANES: off + (t + 1) * LANES].astype(F32)
            y = x * lax.rsqrt(_group_sumsq(x, NA_HEAD) * (1.0 / NA_HEAD) + EPS) * gna_ref[gi:gi + 1, :]
            oref[:, sl] = (y * scale).astype(BF16)
    cq = p_ref[:, OFF_MLA_CQ:OFF_MLA_CQ + MLA_Q_LORA].astype(F32)
    q = _dot(_rms_rows(cq, gq_ref[...]).astype(BF16), wuq_ref[...])
    ckv = p_ref[:, OFF_MLA_CKV:OFF_MLA_CKV + MLA_KV_LORA].astype(F32)
    ckv_n = _rms_rows(ckv, gkv_ref[...]).astype(BF16)
    k_nope = _dot(ckv_n, wuk_ref[...])
    vm_ref[...] = _dot(ckv_n, wuv_ref[...]).astype(BF16)
    kr = p_ref[:, OFF_MLA_KR:OFF_MLA_KR + LANES].astype(F32)
    kr_ss = jnp.sum(kr * kr, axis=-1, keepdims=True)
    inv_n = 1.0 / (MLA_NOPE + MLA_ROPE)
    for h in range(MLA_HEADS):
        a, b = h * 2 * LANES, h * 2 * LANES + LANES
        q0, q1 = q[:, a:b], q[:, b:b + LANES]
        r = lax.rsqrt((jnp.sum(q0 * q0, axis=-1, keepdims=True)
                       + jnp.sum(q1 * q1, axis=-1, keepdims=True)) * inv_n + EPS)
        qm_ref[:, a:b] = (q0 * r * gmq_ref[:, 0:LANES]).astype(BF16)
        qm_ref[:, b:b + LANES] = _rope(q1 * r * gmq_ref[:, LANES:2 * LANES], cosa, sina,
                                       MLA_ROPE // 4).astype(BF16)
        k0 = k_nope[:, h * LANES:(h + 1) * LANES]
        r = lax.rsqrt((jnp.sum(k0 * k0, axis=-1, keepdims=True) + kr_ss) * inv_n + EPS)
        km_ref[:, a:b] = (k0 * r * gmk_ref[:, 0:LANES]).astype(BF16)
        km_ref[:, b:b + LANES] = _rope(kr * r * gmk_ref[:, LANES:2 * LANES], cosa, sina,
                                       MLA_ROPE // 4).astype(BF16)


def _prep(p, tabs, gains, weights, n_rows, rope_idx):
    row = lambda w: pl.BlockSpec((TB, w), lambda i: (i, 0))
    tab = pl.BlockSpec((TB, LANES), lambda i: (rope_idx(i), 0))
    full = lambda a: pl.BlockSpec(a.shape, lambda i: (0,) * a.ndim)
    outs = [512] * 6 + [1024, 1024, 512]
    return pl.pallas_call(
        _prep_kernel,
        grid=(n_rows // TB,),
        in_specs=[row(IN_PAD)] + [tab] * 4 + [full(a) for a in gains] + [full(a) for a in weights],
        out_specs=[row(w) for w in outs],
        out_shape=[jax.ShapeDtypeStruct((n_rows, w), BF16) for w in outs],
        compiler_params=_cp("parallel"),
        name="mixer_prep",
    )(p, *tabs, *gains, *weights)


def _softmax_parts(scores):
    m = functools.reduce(jnp.maximum, [jnp.max(s, axis=-1, keepdims=True) for s in scores])
    es = [jnp.exp(s - m) for s in scores]
    z = functools.reduce(jnp.add, [jnp.sum(e, axis=-1, keepdims=True) for e in es])
    return es, z


def _attn_kernel(*refs, mode, nseg, scale, lam_init):
    q_ref = refs[0]
    ks = [refs[1 + 2 * s] for s in range(nseg)]
    vs = [refs[2 + 2 * s] for s in range(nseg)]
    o_ref = refs[-1]
    q = q_ref[...]
    if mode == "plain":
        es, z = _softmax_parts([_dot_nt(q, k[...]) * scale for k in ks])
        y = functools.reduce(jnp.add, [_dot(e.astype(BF16), v[...]) for e, v in zip(es, vs)])
        o_ref[...] = (y / z).astype(o_ref.dtype)
        return
    lane = lax.broadcasted_iota(jnp.int32, q.shape, 1)
    halves = [jnp.where(lane < LANES // 2, q, jnp.zeros_like(q)),
              jnp.where(lane >= LANES // 2, q, jnp.zeros_like(q))]
    parts = [_softmax_parts([_dot_nt(qh, k[...]) for k in ks]) for qh in halves]
    if mode == "pair":
        outs = [functools.reduce(jnp.add, [_dot(e.astype(BF16), v[...]) for e, v in zip(es, vs)]) / z
                for es, z in parts]
        lo = lax.broadcasted_iota(jnp.int32, outs[0].shape, 1) < LANES // 2
        o_ref[...] = jnp.where(lo, outs[0], outs[1]).astype(o_ref.dtype)
        return
    lam_ref, subg_ref = refs[1 + 2 * nseg], refs[2 + 2 * nseg]
    lv = lam_ref[...]
    lam = (jnp.exp(jnp.sum(lv[0:1] * lv[1:2], axis=-1, keepdims=True))
           - jnp.exp(jnp.sum(lv[2:3] * lv[3:4], axis=-1, keepdims=True)) + lam_init)
    (e0, z0), (e1, z1) = parts
    w0, w1 = 1.0 / z0, lam / z1
    y = functools.reduce(jnp.add, [_dot((a * w0 - b * w1).astype(BF16), v[...])
                                   for a, b, v in zip(e0, e1, vs)])
    o_ref[...] = (_rms_rows(y, subg_ref[...]) * (1.0 - lam_init)).astype(o_ref.dtype)


def _attention(q_arr, k_arr, v_arr, extra, *, mode, heads, dq, v_col0, batch, seq, latent, scale=1.0,
               lam_init=0.0, name):
    nl = batch * seq
    cblk = nl // CTX_LEN
    if latent:
        nq = seq // TQ
        grid = (batch, heads, nq)
        q_spec = pl.BlockSpec((TQ, dq), lambda b, h, j: (b * nq + j, h))
        segs = [(CTX_LEN, lambda b, h, j: (cblk + b, h), lambda b, h, j: (cblk + b, v_col0 + h)),
                (seq, lambda b, h, j: (b, h), lambda b, h, j: (b, v_col0 + h))]
        o_spec = pl.BlockSpec((TQ, LANES), lambda b, h, j: (b * nq + j, h))
        rows = nl
    else:
        grid = (batch, heads, 1)
        q_spec = pl.BlockSpec((CTX_LEN, dq), lambda b, h, j: (cblk + b, h))
        segs = [(CTX_LEN, lambda b, h, j: (cblk + b, h), lambda b, h, j: (cblk + b, v_col0 + h))]
        o_spec = pl.BlockSpec((CTX_LEN, LANES), lambda b, h, j: (b, h))
        rows = batch * CTX_LEN
    in_specs, args = [q_spec], [q_arr]
    for n, kmap, vmap in segs:
        in_specs += [pl.BlockSpec((n, dq), kmap), pl.BlockSpec((n, LANES), vmap)]
        args += [k_arr, v_arr]
    for a in extra:
        in_specs.append(pl.BlockSpec(a.shape, lambda b, h, j: (0, 0)))
        args.append(a)
    return pl.pallas_call(
        functools.partial(_attn_kernel, mode=mode, nseg=len(segs), scale=scale, lam_init=lam_init),
        grid=grid, in_specs=in_specs, out_specs=o_spec,
        out_shape=jax.ShapeDtypeStruct((rows, heads * LANES), BF16),
        compiler_params=_cp("parallel", "parallel", "arbitrary"),
        name=name,
    )(*args)


def _ret_kernel(ql_ref, qc_ref, kl_ref, kc_ref, vl_ref, vc_ref, gl_ref, gc_ref, dmat_ref, xi_ref, zeta_ref,
                gch_ref, ng_ref, yl_ref, yc_ref, accl_ref, accc_ref, *, n_lat, n_ctx, ctx_out):
    C = RT_CHUNK

    def chunk(q, k, v, R, d):
        s = _dot_nt(q, k) * dmat_ref[d, 0]
        inner = _dot(s.astype(BF16), v)
        cross = _dot(q, R.astype(BF16)) * xi_ref[d, 0]
        kz = (k.astype(F32) * zeta_ref[d, 0]).T.astype(BF16)
        return inner + cross, gch_ref[d, 0] * R + _dot(kz, v)

    for d in range(2):
        R = jnp.zeros((RT_DK, RT_DK), F32)
        for j in range(n_ctx):
            c = j if d == 0 else n_ctx - 1 - j
            sl = pl.ds(c * C, C)
            o, R = chunk(qc_ref[sl, :], kc_ref[sl, :], vc_ref[sl, :], R, d)
            if ctx_out:
                if d == 0:
                    accc_ref[sl, :] = o
                else:
                    accc_ref[sl, :] += o

        def body(j, R, d=d):
            c = j if d == 0 else n_lat - 1 - j
            sl = pl.ds(pl.multiple_of(c * C, C), C)
            o, R = chunk(ql_ref[sl, :], kl_ref[sl, :], vl_ref[sl, :], R, d)
            if d == 0:
                accl_ref[sl, :] = o
            else:
                accl_ref[sl, :] += o
            return R

        lax.fori_loop(0, n_lat, body, R)

    def post(acc_ref, g_ref, y_ref):
        g = g_ref[...].astype(F32)
        y_ref[...] = (_rms_rows(acc_ref[...], ng_ref[...]) * (g * _sigmoid(g))).astype(y_ref.dtype)

    post(accl_ref, gl_ref, yl_ref)
    if ctx_out:
        post(accc_ref, gc_ref, yc_ref)
    else:
        yc_ref[...] = jnp.zeros_like(yc_ref)


def _retention(qr, kr, p, tabs, norm_g, *, batch, seq, ctx_out):
    nl = batch * seq
    cblk = nl // CTX_LEN
    lat = lambda c0: pl.BlockSpec((seq, LANES), lambda b, h: (b, c0 + h))
    ctx = lambda c0: pl.BlockSpec((CTX_LEN, LANES), lambda b, h: (cblk + b, c0 + h))
    tab = lambda a: pl.BlockSpec((2, 1) + a.shape[2:], lambda b, h: (0, h, 0, 0))
    return pl.pallas_call(
        functools.partial(_ret_kernel, n_lat=seq // RT_CHUNK, n_ctx=CTX_LEN // RT_CHUNK, ctx_out=ctx_out),
        grid=(batch, RT_HEADS),
        in_specs=[lat(0), ctx(0), lat(0), ctx(0), lat(COL_RT_V), ctx(COL_RT_V), lat(COL_RT_G), ctx(COL_RT_G)]
        + [tab(a) for a in tabs] + [pl.BlockSpec((1, LANES), lambda b, h: (0, 0))],
        out_specs=[pl.BlockSpec((seq, LANES), lambda b, h: (b, h)),
                   pl.BlockSpec((CTX_LEN, LANES), lambda b, h: (b, h))],
        out_shape=[jax.ShapeDtypeStruct((nl, RT_HEADS * LANES), BF16),
                   jax.ShapeDtypeStruct((batch * CTX_LEN, RT_HEADS * LANES), BF16)],
        scratch_shapes=[pltpu.VMEM((seq, LANES), F32), pltpu.VMEM((CTX_LEN, LANES), F32)],
        compiler_params=_cp("parallel", "parallel"),
        name="retention",
    )(qr, qr, kr, kr, p, p, p, p, *tabs, norm_g)


def _na_groups(rows):
    plans, variants = [], []
    for gi in range(rows // NA_GROUP):
        r = gi * NA_GROUP + np.arange(NA_GROUP)
        u0 = int(np.clip(gi * NA_GROUP - NA_KH // 2, 0, rows - NA_UNION))
        rs = np.clip(r - NA_KH // 2, 0, rows - NA_KH)
        krow = u0 + np.arange(NA_UNION)
        valid = (krow[None, :] >= rs[:, None]) & (krow[None, :] < rs[:, None] + NA_KH)
        dr = np.where(valid, krow[None, :] - r[:, None] + (NA_KH - 1), 0)
        key = (valid.tobytes(), dr.tobytes())
        for v, (k, _, _) in enumerate(variants):
            if k == key:
                break
        else:
            v = len(variants)
            variants.append((key, valid, dr))
        plans.append((u0, v))
    return tuple(plans), [(valid, dr) for _, valid, dr in variants]


def _na_kernel(q_ref, kl_ref, kc_ref, vl_ref, vc_ref, bias_ref, o_ref, *, plans):
    gq, gk = NA_GROUP * GRID_W, NA_UNION * GRID_W
    kc, vc = kc_ref[...], vc_ref[...]
    for gi, (u0, var) in enumerate(plans):
        q = q_ref[gi * gq:(gi + 1) * gq, :]
        kw = kl_ref[u0 * GRID_W:u0 * GRID_W + gk, :]
        vw = vl_ref[u0 * GRID_W:u0 * GRID_W + gk, :]
        lane = lax.broadcasted_iota(jnp.int32, q.shape, 1)
        outs = []
        for hh in range(2):
            keep = (lane < LANES // 2) if hh == 0 else (lane >= LANES // 2)
            qh = jnp.where(keep, q, jnp.zeros_like(q))
            (el, ec), z = _softmax_parts([_dot_nt(qh, kw) + bias_ref[hh, var], _dot_nt(qh, kc)])
            outs.append((_dot(el.astype(BF16), vw) + _dot(ec.astype(BF16), vc)) / z)
        lo = lax.broadcasted_iota(jnp.int32, outs[0].shape, 1) < LANES // 2
        o_ref[gi * gq:(gi + 1) * gq, :] = jnp.where(lo, outs[0], outs[1]).astype(o_ref.dtype)


def _na_latent(qn, kn, p, bias, plans, *, batch, seq):
    nl = batch * seq
    cblk = nl // CTX_LEN
    hp = NA_HEADS // 2
    return pl.pallas_call(
        functools.partial(_na_kernel, plans=plans),
        grid=(batch, hp),
        in_specs=[pl.BlockSpec((seq, LANES), lambda b, h: (b, h)),
                  pl.BlockSpec((seq, LANES), lambda b, h: (b, h)),
                  pl.BlockSpec((CTX_LEN, LANES), lambda b, h: (cblk + b, h)),
                  pl.BlockSpec((seq, LANES), lambda b, h: (b, COL_NA_V + h)),
                  pl.BlockSpec((CTX_LEN, LANES), lambda b, h: (cblk + b, COL_NA_V + h)),
                  pl.BlockSpec((2,) + bias.shape[1:], lambda b, h: (h, 0, 0, 0))],
        out_specs=pl.BlockSpec((seq, LANES), lambda b, h: (b, h)),
        out_shape=jax.ShapeDtypeStruct((nl, hp * LANES), BF16),
        compiler_params=_cp("parallel", "parallel"),
        name="neighbourhood_attn",
    )(qn, kn, kn, p, p, bias)


def _merge1_kernel(h_ref, ya_ref, yb_ref, yc_ref, yd_ref, wg_ref, wb_ref, bg_ref, m_ref):
    h = h_ref[...]
    acc = None
    for n, y_ref in enumerate((ya_ref, yb_ref, yc_ref, yd_ref)):
        gate = _sigmoid(_dot(h, wg_ref[n]) + bg_ref[n:n + 1, :])
        term = gate * _dot(y_ref[...], wb_ref[n])
        acc = term if acc is None else acc + term
    m_ref[...] = acc.astype(BF16)


def _merge1(h, ys, wg, wb, bg, n_rows):
    d = h.shape[1]
    cb = 512
    return pl.pallas_call(
        _merge1_kernel,
        grid=(d // cb, n_rows // TB),
        in_specs=[pl.BlockSpec((TB, d), lambda j, i: (i, 0))]
        + [pl.BlockSpec((TB, BRANCH_W), lambda j, i: (i, 0))] * N_BRANCH
        + [pl.BlockSpec((N_BRANCH, d, cb), lambda j, i: (0, 0, j)),
           pl.BlockSpec((N_BRANCH, BRANCH_W, cb), lambda j, i: (0, 0, j)),
           pl.BlockSpec((N_BRANCH, cb), lambda j, i: (0, j))],
        out_specs=pl.BlockSpec((TB, cb), lambda j, i: (i, j)),
        out_shape=jax.ShapeDtypeStruct((n_rows, d), BF16),
        compiler_params=_cp("parallel", "parallel"),
        name="gated_branch_sum",
    )(h, *ys, wg, wb, bg)


def _merge2_kernel(m_ref, wo_ref, xa_ref, xb_ref, mod_ref, g_ref, xo_ref, h2t_ref, *, n_a):
    def body(x_ref):
        x = x_ref[...] + mod_ref[0, 2:3, :] * _dot(m_ref[...], wo_ref[...])
        xo_ref[...] = x
        h2 = _rms_rows(x, g_ref[...]) * (1.0 + mod_ref[0, 4:5, :]) + mod_ref[0, 3:4, :]
        h2t_ref[...] = h2.T.astype(BF16)

    pl.when(pl.program_id(0) < n_a)(functools.partial(body, xa_ref))
    pl.when(pl.program_id(0) >= n_a)(functools.partial(body, xb_ref))


def _merge2(m, wo, xa, xb, xb_row0, n_lat, mod, g, n_rows, mod_idx):
    d = xa.shape[1]
    amap, bmap = _split_rows(n_lat // TB, xb_row0 // TB)
    return pl.pallas_call(
        functools.partial(_merge2_kernel, n_a=n_lat // TB),
        grid=(n_rows // TB,),
        in_specs=[pl.BlockSpec((TB, d), lambda i: (i, 0)),
                  pl.BlockSpec((d, d), lambda i: (0, 0)),
                  pl.BlockSpec((TB, d), amap),
                  pl.BlockSpec((TB, d), bmap),
                  pl.BlockSpec((1, 6, d), lambda i: (mod_idx(i), 0, 0)),
                  pl.BlockSpec((1, d), lambda i: (0, 0))],
        out_specs=[pl.BlockSpec((TB, d), lambda i: (i, 0)),
                   pl.BlockSpec((d, TB), lambda i: (0, i))],
        out_shape=[jax.ShapeDtypeStruct((n_rows, d), F32),
                   jax.ShapeDtypeStruct((d, n_rows), BF16)],
        compiler_params=_cp("parallel"),
        name="out_proj_residual_norm2",
    )(m, wo, xa, xb, mod, g)


def _merge_exchange(n):
    comps, p = [], 1
    while p < n:
        k = p
        while k >= 1:
            for j in range(k % p, n - k, 2 * k):
                for i in range(min(k, n - j - k)):
                    if (i + j) // (2 * p) == (i + j + k) // (2 * p):
                        comps.append((i + j, i + j + k))
            k //= 2
        p *= 2
    return comps


def _top_values(x, k):
    sub = 8
    cols = [x[v * sub:(v + 1) * sub, :] for v in range(x.shape[0] // sub)]
    for i, j in _merge_exchange(len(cols)):
        cols[i], cols[j] = jnp.maximum(cols[i], cols[j]), jnp.minimum(cols[i], cols[j])
    tops = []
    for it in range(k):
        m = jnp.max(cols[0], axis=0, keepdims=True)
        tops.append(m)
        if it == k - 1:
            break
        hit = cols[0] == m
        depth = min(len(cols), k - it)
        for d in range(depth - 1):
            cols[d] = jnp.where(hit, cols[d + 1], cols[d])
        if depth == len(cols):
            cols[depth - 1] = jnp.where(hit, -jnp.inf, cols[depth - 1])
    return tops


def _route_kernel(h2t_ref, wqt_ref, sk_ref, s1_ref, th_ref, e1_ref, e2_ref):
    qt = _dot(wqt_ref[...], h2t_ref[...]).astype(BF16)
    nk, k, half = PEER_NKEYS, PEER_TOPK, PEER_TOPK // 2
    for h in range(PEER_HEADS):
        s1 = _dot(sk_ref[2 * h], qt[(2 * h) * nk:(2 * h + 1) * nk, :])
        s2 = _dot(sk_ref[2 * h + 1], qt[(2 * h + 1) * nk:(2 * h + 2) * nk, :])
        t1, t2 = _top_values(s1, k), _top_values(s2, k)
        t1_hi = jnp.concatenate(t1[half:], axis=0)
        t2_lo, t2_hi = jnp.concatenate(t2[:half], axis=0), jnp.concatenate(t2[half:], axis=0)
        cand = jnp.concatenate([t1[a] + t2_lo for a in range(half)] + [t1[0] + t2_hi, t1_hi + t2[0]], axis=0)
        sel = _top_values(cand, k)
        tau = sel[-1]
        z = functools.reduce(jnp.add, [jnp.exp(c - sel[0]) for c in sel])
        th = jnp.full(s2.shape, jnp.inf, F32)
        for a in range(half):
            th = jnp.where(t1[a] + s2 >= tau, t1[a], th)
        th0 = jnp.full(tau.shape, jnp.inf, F32)
        for a in range(half, k):
            th0 = jnp.where(t1[a] + t2[0] >= tau, t1[a], th0)
        th = jnp.where(s2 == t2[0], jnp.minimum(th, th0), th)
        s1_ref[h] = s1
        th_ref[h] = th
        e1_ref[h] = jnp.exp(s1 - t1[0]) / z
        e2_ref[h] = jnp.exp(s2 - t2[0])


def _route(h2t, wqt, sk, n_rows):
    d = h2t.shape[0]
    t = 256
    big = pl.BlockSpec((PEER_HEADS, PEER_NKEYS, t), lambda i: (0, 0, i))
    return pl.pallas_call(
        _route_kernel,
        grid=(n_rows // t,),
        in_specs=[pl.BlockSpec((d, t), lambda i: (0, i)),
                  pl.BlockSpec(wqt.shape, lambda i: (0, 0)),
                  pl.BlockSpec(sk.shape, lambda i: (0, 0, 0))],
        out_specs=[big] * 4,
        out_shape=[jax.ShapeDtypeStruct((PEER_HEADS, PEER_NKEYS, n_rows), F32)] * 4,
        compiler_params=_cp("parallel"),
        name="peer_route",
    )(h2t, wqt, sk)


def _experts_kernel(h2t_ref, u_ref, vt_ref, s1_ref, e1_ref, th_ref, e2_ref, out_ref,
                    a_even, a_odd, w_even, w_odd, *, n_items):
    s = pl.program_id(0)
    nk = PEER_NKEYS
    ni = PEER_EC // nk
    nchunk = PEER_EXPERTS // PEER_EC

    @pl.when(s == 0)
    def _():
        a_odd[...] = jnp.zeros_like(a_odd)
        w_odd[...] = jnp.zeros_like(w_odd)

    @pl.when((s == 0) | ((s >= 2) & ((s - 2) % nchunk == 0)))
    def _():
        out_ref[...] = jnp.zeros_like(out_ref)

    half = jnp.where((s >= 1) & (s <= n_items), 0.5, 0.0)
    gi = 4

    def gate_block(a_ref, w_ref, lt, jg, ig):
        ls = slice(lt * LANES, (lt + 1) * LANES)
        js = slice(jg * 16, (jg + 1) * 16)
        gs = [None] * gi
        for h in range(PEER_HEADS):
            th, e2 = th_ref[h, js, ls], e2_ref[h, js, ls]
            for ii in range(gi):
                r = ig * gi + ii
                term = jnp.where(s1_ref[r, h:h + 1, ls] >= th, e2 * e1_ref[r, h:h + 1, ls], 0.0)
                gs[ii] = term if gs[ii] is None else gs[ii] + term
        for ii in range(gi):
            r = ig * gi + ii
            rows = slice(r * nk + jg * 16, r * nk + (jg + 1) * 16)
            a = a_ref[rows, ls]
            act = half * a * (1.0 + lax.erf(a * (2.0 ** -0.5)))
            w_ref[rows, ls] = (gs[ii] * act).astype(BF16)

    def step(a_new, a_old, w_new, w_old):
        d, t = out_ref.shape
        n_out, n_pre = 8, 2 * (PEER_EC // 256)

        def out_unit(m):
            rs = slice(m * d // n_out, (m + 1) * d // n_out)
            out_ref[rs, :] += _dot(vt_ref[rs, :], w_old[...])

        def pre_unit(q):
            rs = slice((q // 2) * 256, (q // 2 + 1) * 256)
            cs = slice((q % 2) * t // 2, (q % 2 + 1) * t // 2)
            a_new[rs, cs] = _dot(u_ref[rs, :], h2t_ref[:, cs])

        blocks = [(lt, jg, ig) for lt in range(t // LANES) for jg in range(nk // 16) for ig in range(ni // gi)]
        units = ([functools.partial(out_unit, m) for m in range(n_out)]
                 + [functools.partial(pre_unit, q) for q in range(n_pre)])
        for k, unit in enumerate(units):
            unit()
            for blk in blocks[k * len(blocks) // len(units):(k + 1) * len(blocks) // len(units)]:
                gate_block(a_old, w_new, *blk)

    @pl.when(s % 2 == 0)
    def _():
        step(a_even, a_odd, w_even, w_odd)

    @pl.when(s % 2 == 1)
    def _():
        step(a_odd, a_even, w_odd, w_even)


def _experts(h2t, u, vt, s1, e1, th, e2, n_rows):
    d = h2t.shape[0]
    t, ec = PEER_T, PEER_EC
    nchunk = PEER_EXPERTS // ec
    n_items = (n_rows // t) * nchunk
    item = lambda s, lag: jnp.clip(s - lag, 0, n_items - 1)
    res = lambda a: pl.BlockSpec(a.shape[:2] + (t,), lambda s: (0, 0, item(s, 1) // nchunk))
    row = lambda a: pl.BlockSpec((ec // PEER_NKEYS, PEER_HEADS, t),
                                 lambda s: (item(s, 1) % nchunk, 0, item(s, 1) // nchunk))
    return pl.pallas_call(
        functools.partial(_experts_kernel, n_items=n_items),
        grid=(n_items + 2,),
        in_specs=[pl.BlockSpec((d, t), lambda s: (0, item(s, 0) // nchunk)),
                  pl.BlockSpec((ec, d), lambda s: (item(s, 0) % nchunk, 0)),
                  pl.BlockSpec((d, ec), lambda s: (0, item(s, 2) % nchunk)),
                  row(s1), row(e1), res(th), res(e2)],
        out_specs=pl.BlockSpec((d, t), lambda s: (0, item(s, 2) // nchunk)),
        out_shape=jax.ShapeDtypeStruct((d, n_rows), F32),
        scratch_shapes=[pltpu.VMEM((ec, t), F32), pltpu.VMEM((ec, t), F32),
                        pltpu.VMEM((ec, t), BF16), pltpu.VMEM((ec, t), BF16)],
        compiler_params=_cp("arbitrary"),
        name="peer_experts",
    )(h2t, u, vt, s1, e1, th, e2)


def _resid_kernel(x_ref, pt_ref, mod_ref, o_ref):
    o_ref[...] = x_ref[...] + mod_ref[0, 5:6, :] * pt_ref[...].T


def _peer_residual(x, pt, mod, n_rows, mod_idx):
    d = x.shape[1]
    return pl.pallas_call(
        _resid_kernel,
        grid=(n_rows // TB,),
        in_specs=[pl.BlockSpec((TB, d), lambda i: (i, 0)),
                  pl.BlockSpec((d, TB), lambda i: (0, i)),
                  pl.BlockSpec((1, 6, d), lambda i: (mod_idx(i), 0, 0))],
        out_specs=pl.BlockSpec((TB, d), lambda i: (i, 0)),
        out_shape=jax.ShapeDtypeStruct((n_rows, d), F32),
        compiler_params=_cp("parallel"),
        name="peer_residual",
    )(x, pt, mod)


def _rope_table(n_tok, dim):
    m = dim // 2
    inv = ROPE_BASE ** (-jnp.arange(0, m, 2, dtype=F32) / m)
    t = jnp.arange(n_tok)
    row = (t // GRID_W).astype(F32)
    col = (t % GRID_W).astype(F32)
    ar, ac = row[:, None] * inv, col[:, None] * inv
    ang = jnp.concatenate([ar, ar, ac, ac], axis=-1)
    sign = np.where((np.arange(dim) % (dim // 2)) < dim // 4, -1.0, 1.0).astype(np.float32)
    reps = LANES // dim
    cos = jnp.tile(jnp.cos(ang), (1, reps))
    sin = jnp.tile(jnp.sin(ang) * sign, (1, reps))
    return (jnp.concatenate([cos, jnp.ones((TB, LANES), F32)], axis=0),
            jnp.concatenate([sin, jnp.zeros((TB, LANES), F32)], axis=0))


def _na_bias_table(rpb, variants):
    cols = np.arange(GRID_W)
    start = np.clip(cols - NA_KW // 2, 0, GRID_W - NA_KW)
    valid_c = (cols[None, :] >= start[:, None]) & (cols[None, :] < start[:, None] + NA_KW)
    dc = np.clip(cols[None, :] - cols[:, None] + (NA_KW - 1), 0, 2 * NA_KW - 2)
    g = rpb.astype(F32)[:, :, dc]
    tabs = []
    for valid_r, dr in variants:
        ok = valid_r[None, :, :, None, None] & valid_c[None, None, None]
        t = jnp.where(ok, g[:, dr], NEG).transpose(0, 1, 3, 2, 4)
        tabs.append(t.reshape(NA_HEADS, NA_GROUP * GRID_W, NA_UNION * GRID_W))
    return jnp.stack(tabs, axis=1)


def _retention_tables(decay_logit):
    C = RT_CHUNK
    log_g = jax.nn.log_sigmoid(decay_logit.astype(F32))
    n = jnp.arange(C, dtype=F32)
    diff = n[:, None] - n[None, :]
    dm = jnp.where(diff >= 0, jnp.exp(log_g[:, :, None, None] * jnp.maximum(diff, 0.0)), 0.0)
    dmat = jnp.stack([dm[0], jnp.swapaxes(dm[1], -1, -2)])
    xi = jnp.exp(log_g[:, :, None] * (n + 1.0))
    zeta = jnp.exp(log_g[:, :, None] * (C - 1.0 - n))
    xi = jnp.stack([xi[0], xi[1, :, ::-1]])
    zeta = jnp.stack([zeta[0], zeta[1, :, ::-1]])
    bc = lambda a: jnp.broadcast_to(a[..., None], a.shape + (LANES,))
    gch = jnp.broadcast_to(jnp.exp(log_g * C)[:, :, None, None], (2, RT_HEADS, 1, LANES))
    return dmat, bc(xi), bc(zeta), gch


def kernel(x, c, ctx, c_ctx, w_mod, b_mod, norm1_g, norm2_g, w_in, diff_lambda, diff_qk_g, diff_sub_g,
           ret_decay, ret_norm_g, na_qk_g, na_rpb, mla_q_norm_g, mla_kv_norm_g, w_uq, w_ukv, mla_qk_g,
           w_branch, w_gate, b_gate, w_o, peer_w_query, peer_sub_keys, peer_u, peer_v):
    B, S, D = x.shape
    depth = w_mod.shape[0]
    assert D == D_MODEL and ctx.shape[1] == CTX_LEN and S % TB_IN == 0 and (B * CTX_LEN) % TB_IN == 0
    assert S % (GRID_W * NA_GROUP) == 0 and S // GRID_W >= NA_UNION
    NL, NC = B * S, B * CTX_LEN
    NT = NL + NC

    mod_idx = lambda i: jnp.minimum(i * TB // S, B)
    mod_idx_in = lambda i: jnp.minimum(i * TB_IN // S, B)
    nlb, spb = NL // TB, S // TB
    rope_idx = lambda i: jnp.where(i < nlb, i % spb, spb)

    xa, xb, xb_row0 = x.reshape(NL, D), ctx.reshape(NC, D), 0
    c_all = jnp.zeros((16, D), F32).at[:B].set(c).at[B].set(c_ctx)
    cos_a, sin_a = _rope_table(S, DA_HEAD)
    cos_r, sin_r = _rope_table(S, RT_DK)
    na_plans, na_variants = _na_groups(S // GRID_W)
    tile2 = lambda g: jnp.tile(g.astype(F32), (1, 2))
    pad_qk = lambda g: jnp.pad(g.astype(F32), (0, 2 * LANES - g.shape[0])).reshape(1, 2 * LANES)

    for l in range(depth):
        last = l == depth - 1
        lam_init = 0.8 - 0.6 * math.exp(-0.3 * l)
        n_tok = NL if last else NT

        w_in_bf = jnp.pad(w_in[l], ((0, 0), (0, IN_PAD - IN_WIDTH))).astype(BF16)
        wuq = w_uq[l].reshape(MLA_Q_LORA, MLA_HEADS, MLA_NOPE + MLA_ROPE)
        wuq = jnp.pad(wuq, ((0, 0), (0, 0), (0, 2 * LANES - MLA_NOPE - MLA_ROPE)))
        wuq = wuq.reshape(MLA_Q_LORA, MLA_HEADS * 2 * LANES).astype(BF16)
        wukv = w_ukv[l].reshape(MLA_KV_LORA, MLA_HEADS, MLA_NOPE + MLA_V)
        wuk = wukv[:, :, :MLA_NOPE].reshape(MLA_KV_LORA, MLA_HEADS * MLA_NOPE).astype(BF16)
        wuv = wukv[:, :, MLA_NOPE:].reshape(MLA_KV_LORA, MLA_HEADS * MLA_V).astype(BF16)
        gains = [tile2(diff_qk_g[l]), tile2(na_qk_g[l]), mla_q_norm_g[l].reshape(1, -1),
                 mla_kv_norm_g[l].reshape(1, -1), pad_qk(mla_qk_g[l, 0]), pad_qk(mla_qk_g[l, 1])]
        wg_bf, wb_bf, wo_bf = w_gate[l].astype(BF16), w_branch[l].astype(BF16), w_o[l].astype(BF16)
        wqt = peer_w_query[l].T.astype(BF16)
        sk = peer_sub_keys[l].reshape(PEER_HEADS * 2, PEER_NKEYS, PEER_DQ // 2).astype(BF16)
        u_bf = peer_u[l].astype(BF16)
        vt_bf = peer_v[l].T.astype(BF16)

        mod = _modulation(c_all, w_mod[l], b_mod[l]).reshape(16, 6, D)
        h1, p = _in_proj(xa, xb, xb_row0, NL, mod, norm1_g[l].reshape(1, D), w_in_bf, NT, mod_idx_in)
        qa, ka, qr, kr, qn, kn, qm, km, vm = _prep(
            p, (cos_a, sin_a, cos_r, sin_r), gains, (wuq, wuk, wuv), NT, rope_idx)

        sub_g = diff_sub_g[l].reshape(1, -1)
        mla_scale = (MLA_NOPE + MLA_ROPE) ** -0.5
        ret_tabs = _retention_tables(ret_decay[l])
        common = dict(batch=B, seq=S)
        ya = _attention(qa, ka, p, (diff_lambda[l], sub_g), mode="diff", heads=DA_HEADS, dq=LANES,
                        v_col0=COL_DA_V, latent=True, lam_init=lam_init, name="diff_attn", **common)
        yb, yb_c = _retention(qr, kr, p, ret_tabs, ret_norm_g[l].reshape(1, -1), ctx_out=not last, **common)
        yc = _na_latent(qn, kn, p, _na_bias_table(na_rpb[l], na_variants), na_plans, **common)
        yd = _attention(qm, km, vm, (), mode="plain", heads=MLA_HEADS, dq=2 * LANES, v_col0=0,
                        latent=True, scale=mla_scale, name="latent_attn", **common)
        ys = [ya, yb, yc, yd]
        if not last:
            ya_c = _attention(qa, ka, p, (diff_lambda[l], sub_g), mode="diff", heads=DA_HEADS, dq=LANES,
                              v_col0=COL_DA_V, latent=False, lam_init=lam_init, name="diff_attn_ctx", **common)
            yc_c = _attention(qn, kn, p, (), mode="pair", heads=NA_HEADS // 2, dq=LANES, v_col0=COL_NA_V,
                              latent=False, name="ctx_attn", **common)
            yd_c = _attention(qm, km, vm, (), mode="plain", heads=MLA_HEADS, dq=2 * LANES, v_col0=0,
                              latent=False, scale=mla_scale, name="latent_attn_ctx", **common)
            ys = [jnp.concatenate([a, b], axis=0) for a, b in zip(ys, (ya_c, yb_c, yc_c, yd_c))]

        m = _merge1(h1, ys, wg_bf, wb_bf, b_gate[l], n_tok)
        x_mid, h2t = _merge2(m, wo_bf, xa, xb, xb_row0, NL, mod, norm2_g[l].reshape(1, D), n_tok, mod_idx)
        s1, th, e1, e2 = _route(h2t, wqt, sk, n_tok)
        pt = _experts(h2t, u_bf, vt_bf, jnp.transpose(s1, (1, 0, 2)), jnp.transpose(e1, (1, 0, 2)),
                      th, e2, n_tok)
        xa = xb = _peer_residual(x_mid, pt, mod, n_tok, mod_idx)
        xb_row0 = NL

    return xa.reshape(B, S, D)
```

```python
import functools
import math

import jax
import jax.numpy as jnp
import numpy as np
from jax import lax
from jax.experimental import pallas as pl
from jax.experimental.pallas import tpu as pltpu

F32 = jnp.float32
BF16 = jnp.bfloat16

D_MODEL = 2048
CTX_LEN = 256
GRID_W = 64
EPS = 1e-6
ROPE_BASE = 10000.0
N_BRANCH = 4
BRANCH_W = 512
DA_HEADS, DA_HEAD = 4, 64
RT_HEADS, RT_DK, RT_CHUNK = 4, 128, 128
NA_HEADS, NA_HEAD, NA_KH, NA_KW = 8, 64, 8, 16
MLA_HEADS, MLA_Q_LORA, MLA_KV_LORA, MLA_NOPE, MLA_ROPE, MLA_V = 4, 512, 256, 128, 64, 128
PEER_HEADS, PEER_NKEYS, PEER_DQ, PEER_TOPK = 8, 128, 256, 16
PEER_EXPERTS = PEER_NKEYS * PEER_NKEYS

LANES = 128
IN_WIDTH = 5952
IN_PAD = 6144
COL_DA_V = 1024 // LANES
COL_RT_V = 2560 // LANES
COL_RT_G = 3072 // LANES
COL_NA_V = 4608 // LANES
OFF_RT_Q, OFF_RT_K = 1536, 2048
OFF_NA_Q, OFF_NA_K = 3584, 4096
OFF_MLA_CQ, OFF_MLA_CKV, OFF_MLA_KR = 5120, 5632, 5888

TB = 512
TB_IN = 1024
TQ = 1024
TQ_SPLIT = 4
PEER_T = 512
PEER_EC = 512
NA_GROUP = 4
NA_UNION = NA_GROUP + NA_KH
NEG = -1e30
VMEM_LIMIT = 56 * 1024 * 1024


def _cp(*sem):
    return pltpu.CompilerParams(dimension_semantics=sem, vmem_limit_bytes=VMEM_LIMIT)


def _dot(a, b):
    return jnp.dot(a, b, preferred_element_type=F32)


def _dot_nt(a, b):
    return lax.dot_general(a, b, (((1,), (1,)), ((), ())), preferred_element_type=F32)


def _sigmoid(z):
    return 1.0 / (1.0 + jnp.exp(-z))


def _rms_rows(x, g):
    return x * lax.rsqrt(jnp.mean(x * x, axis=-1, keepdims=True) + EPS) * g


def _mod_kernel(c_ref, w_ref, b_ref, o_ref):
    c = c_ref[...]
    a = (c * _sigmoid(c)).astype(BF16)
    o_ref[...] = _dot(a, w_ref[...].astype(BF16)) + b_ref[...]


def _modulation(c_all, w_mod, b_mod):
    rows, d = c_all.shape
    n = w_mod.shape[1]
    cb = 1024
    return pl.pallas_call(
        _mod_kernel,
        grid=(n // cb,),
        in_specs=[pl.BlockSpec((rows, d), lambda j: (0, 0)),
                  pl.BlockSpec((d, cb), lambda j: (0, j)),
                  pl.BlockSpec((1, cb), lambda j: (0, j))],
        out_specs=pl.BlockSpec((rows, cb), lambda j: (0, j)),
        out_shape=jax.ShapeDtypeStruct((rows, n), F32),
        compiler_params=_cp("parallel"),
        name="adaln_mod",
    )(c_all, w_mod, b_mod.reshape(1, n))


def _split_rows(n_a, b_off):
    return (lambda i, *_: (jnp.minimum(i, n_a - 1), 0),
            lambda i, *_: (jnp.maximum(i - n_a, 0) + b_off, 0))


def _inproj_kernel(xa_ref, xb_ref, mod_ref, g_ref, w_ref, h_ref, p_ref, *, n_a):
    def prologue(x_ref):
        y = _rms_rows(x_ref[...], g_ref[...])
        h_ref[...] = (y * (1.0 + mod_ref[0, 1:2, :]) + mod_ref[0, 0:1, :]).astype(BF16)

    first = pl.program_id(1) == 0
    pl.when(first & (pl.program_id(0) < n_a))(functools.partial(prologue, xa_ref))
    pl.when(first & (pl.program_id(0) >= n_a))(functools.partial(prologue, xb_ref))
    p_ref[...] = _dot(h_ref[...], w_ref[...]).astype(BF16)


def _in_proj(xa, xb, xb_row0, n_lat, mod, g, w_bf, n_rows, mod_idx):
    d = xa.shape[1]
    cb = 512
    amap, bmap = _split_rows(n_lat // TB_IN, xb_row0 // TB_IN)
    return pl.pallas_call(
        functools.partial(_inproj_kernel, n_a=n_lat // TB_IN),
        grid=(n_rows // TB_IN, IN_PAD // cb),
        in_specs=[pl.BlockSpec((TB_IN, d), amap),
                  pl.BlockSpec((TB_IN, d), bmap),
                  pl.BlockSpec((1, 6, d), lambda i, j: (mod_idx(i), 0, 0)),
                  pl.BlockSpec((1, d), lambda i, j: (0, 0)),
                  pl.BlockSpec((d, cb), lambda i, j: (0, j))],
        out_specs=[pl.BlockSpec((TB_IN, d), lambda i, j: (i, 0)),
                   pl.BlockSpec((TB_IN, cb), lambda i, j: (i, j))],
        out_shape=[jax.ShapeDtypeStruct((n_rows, d), BF16),
                   jax.ShapeDtypeStruct((n_rows, IN_PAD), BF16)],
        compiler_params=_cp("parallel", "arbitrary"),
        name="norm1_in_proj",
    )(xa, xb, mod, g, w_bf)


def _rot_partner(x, q):
    lane = lax.broadcasted_iota(jnp.int32, x.shape, 1)
    first = (lane & (2 * q - 1)) < q
    return jnp.where(first, pltpu.roll(x, LANES - q, 1), pltpu.roll(x, q, 1))


def _rope(x, cos, sin_signed, q):
    return x * cos + _rot_partner(x, q) * sin_signed


def _group_sumsq(x, gsz):
    x2 = x * x
    hi = x2.astype(BF16)
    lo = (x2 - hi.astype(F32)).astype(BF16)
    r = lax.broadcasted_iota(jnp.int32, (LANES, LANES), 0) // gsz
    c = lax.broadcasted_iota(jnp.int32, (LANES, LANES), 1) // gsz
    ones = jnp.where(r == c, 1.0, 0.0).astype(BF16)
    return _dot(hi, ones) + _dot(lo, ones)


def _prep_kernel(p_ref, cosa_ref, sina_ref, cosr_ref, sinr_ref, gda_ref, gna_ref, gq_ref, gkv_ref,
                 gmq_ref, gmk_ref, wuq_ref, wuk_ref, wuv_ref,
                 qa_ref, ka_ref, qr_ref, kr_ref, qn_ref, kn_ref, qm_ref, km_ref, vm_ref):
    cosa, sina = cosa_ref[...], sina_ref[...]
    cosr, sinr = cosr_ref[...], sinr_ref[...]
    for t in range(4):
        sl = slice(t * LANES, (t + 1) * LANES)
        for off, gi, oref, scale in ((0, 0, qa_ref, DA_HEAD ** -0.5), (512, 1, ka_ref, 1.0)):
            x = p_ref[:, off + t * LANES: off + (t + 1) * LANES].astype(F32)
            y = x * lax.rsqrt(_group_sumsq(x, DA_HEAD) * (1.0 / DA_HEAD) + EPS) * gda_ref[gi:gi + 1, :]
            oref[:, sl] = (_rope(y, cosa, sina, DA_HEAD // 4) * scale).astype(BF16)
        x = p_ref[:, OFF_RT_Q + t * LANES: OFF_RT_Q + (t + 1) * LANES].astype(F32)
        qr_ref[:, sl] = _rope(x, cosr, sinr, RT_DK // 4).astype(BF16)
        x = p_ref[:, OFF_RT_K + t * LANES: OFF_RT_K + (t + 1) * LANES].astype(F32) * (RT_DK ** -0.5)
        kr_ref[:, sl] = _rope(x, cosr, sinr, RT_DK // 4).astype(BF16)
        for off, gi, oref, scale in ((OFF_NA_Q, 0, qn_ref, NA_HEAD ** -0.5), (OFF_NA_K, 1, kn_ref, 1.0)):
            x = p_ref[:, off + t * LANES: off + (t + 1) * LANES].astype(F32)
            y = x * lax.rsqrt(_group_sumsq(x, NA_HEAD) * (1.0 / NA_HEAD) + EPS) * gna_ref[gi:gi + 1, :]
            oref[:, sl] = (y * scale).astype(BF16)
    cq = p_ref[:, OFF_MLA_CQ:OFF_MLA_CQ + MLA_Q_LORA].astype(F32)
    q = _dot(_rms_rows(cq, gq_ref[...]).astype(BF16), wuq_ref[...])
    ckv = p_ref[:, OFF_MLA_CKV:OFF_MLA_CKV + MLA_KV_LORA].astype(F32)
    ckv_n = _rms_rows(ckv, gkv_ref[...]).astype(BF16)
    k_nope = _dot(ckv_n, wuk_ref[...])
    vm_ref[...] = _dot(ckv_n, wuv_ref[...]).astype(BF16)
    kr = p_ref[:, OFF_MLA_KR:OFF_MLA_KR + LANES].astype(F32)
    kr_ss = jnp.sum(kr * kr, axis=-1, keepdims=True)
    inv_n = 1.0 / (MLA_NOPE + MLA_ROPE)
    for h in range(MLA_HEADS):
        a, b = h * 2 * LANES, h * 2 * LANES + LANES
        q0, q1 = q[:, a:b], q[:, b:b + LANES]
        r = lax.rsqrt((jnp.sum(q0 * q0, axis=-1, keepdims=True)
                       + jnp.sum(q1 * q1, axis=-1, keepdims=True)) * inv_n + EPS)
        qm_ref[:, a:b] = (q0 * r * gmq_ref[:, 0:LANES]).astype(BF16)
        qm_ref[:, b:b + LANES] = _rope(q1 * r * gmq_ref[:, LANES:2 * LANES], cosa, sina,
                                       MLA_ROPE // 4).astype(BF16)
        k0 = k_nope[:, h * LANES:(h + 1) * LANES]
        r = lax.rsqrt((jnp.sum(k0 * k0, axis=-1, keepdims=True) + kr_ss) * inv_n + EPS)
        km_ref[:, a:b] = (k0 * r * gmk_ref[:, 0:LANES]).astype(BF16)
        km_ref[:, b:b + LANES] = _rope(kr * r * gmk_ref[:, LANES:2 * LANES], cosa, sina,
                                       MLA_ROPE // 4).astype(BF16)


def _prep(p, tabs, gains, weights, n_rows, rope_idx):
    row = lambda w: pl.BlockSpec((TB, w), lambda i: (i, 0))
    tab = pl.BlockSpec((TB, LANES), lambda i: (rope_idx(i), 0))
    full = lambda a: pl.BlockSpec(a.shape, lambda i: (0,) * a.ndim)
    outs = [512] * 6 + [1024, 1024, 512]
    return pl.pallas_call(
        _prep_kernel,
        grid=(n_rows // TB,),
        in_specs=[row(IN_PAD)] + [tab] * 4 + [full(a) for a in gains] + [full(a) for a in weights],
        out_specs=[row(w) for w in outs],
        out_shape=[jax.ShapeDtypeStruct((n_rows, w), BF16) for w in outs],
        compiler_params=_cp("parallel"),
        name="mixer_prep",
    )(p, *tabs, *gains, *weights)


def _softmax_parts(scores):
    m = functools.reduce(jnp.maximum, [jnp.max(s, axis=-1, keepdims=True) for s in scores])
    es = [jnp.exp(s - m) for s in scores]
    z = functools.reduce(jnp.add, [jnp.sum(e, axis=-1, keepdims=True) for e in es])
    return es, z


def _attn_kernel(*refs, mode, nseg, scale, lam_init, nsplit):
    q_ref = refs[0]
    ks = [refs[1 + 2 * s] for s in range(nseg)]
    vs = [refs[2 + 2 * s] for s in range(nseg)]
    o_ref = refs[-1]
    rows = q_ref.shape[0] // nsplit

    def scores(g):
        q = q_ref[g * rows:(g + 1) * rows, :]
        if mode == "plain":
            return [[_dot_nt(q, k[...]) * scale for k in ks]]
        lane = lax.broadcasted_iota(jnp.int32, q.shape, 1)
        halves = [jnp.where(lane < LANES // 2, q, jnp.zeros_like(q)),
                  jnp.where(lane >= LANES // 2, q, jnp.zeros_like(q))]
        return [[_dot_nt(qh, k[...]) for k in ks] for qh in halves]

    def weighted(es):
        return functools.reduce(jnp.add, [_dot(e.astype(BF16), v[...]) for e, v in zip(es, vs)])

    if mode == "diff":
        lam_ref, subg_ref = refs[1 + 2 * nseg], refs[2 + 2 * nseg]
        lv = lam_ref[...]
        lam = (jnp.exp(jnp.sum(lv[0:1] * lv[1:2], axis=-1, keepdims=True))
               - jnp.exp(jnp.sum(lv[2:3] * lv[3:4], axis=-1, keepdims=True)) + lam_init)

    def finish(g, sc):
        parts = [_softmax_parts(s) for s in sc]
        if mode == "plain":
            (es, z), = parts
            y = weighted(es) / z
        elif mode == "pair":
            outs = [weighted(es) / z for es, z in parts]
            lo = lax.broadcasted_iota(jnp.int32, outs[0].shape, 1) < LANES // 2
            y = jnp.where(lo, outs[0], outs[1])
        else:
            (e0, z0), (e1, z1) = parts
            w0, w1 = 1.0 / z0, lam / z1
            y = weighted([a * w0 - b * w1 for a, b in zip(e0, e1)])
            y = _rms_rows(y, subg_ref[...]) * (1.0 - lam_init)
        o_ref[g * rows:(g + 1) * rows, :] = y.astype(o_ref.dtype)

    sc = [scores(g) for g in range(nsplit)]
    for g in range(nsplit):
        finish(g, sc[g])


def _attention(q_arr, k_arr, v_arr, extra, *, mode, heads, dq, v_col0, batch, seq, latent, scale=1.0,
               lam_init=0.0, name):
    nl = batch * seq
    cblk = nl // CTX_LEN
    if latent:
        nq = seq // TQ
        grid = (batch, heads, nq)
        q_spec = pl.BlockSpec((TQ, dq), lambda b, h, j: (b * nq + j, h))
        segs = [(CTX_LEN, lambda b, h, j: (cblk + b, h), lambda b, h, j: (cblk + b, v_col0 + h)),
                (seq, lambda b, h, j: (b, h), lambda b, h, j: (b, v_col0 + h))]
        o_spec = pl.BlockSpec((TQ, LANES), lambda b, h, j: (b * nq + j, h))
        rows = nl
    else:
        grid = (batch, heads, 1)
        q_spec = pl.BlockSpec((CTX_LEN, dq), lambda b, h, j: (cblk + b, h))
        segs = [(CTX_LEN, lambda b, h, j: (cblk + b, h), lambda b, h, j: (cblk + b, v_col0 + h))]
        o_spec = pl.BlockSpec((CTX_LEN, LANES), lambda b, h, j: (b, h))
        rows = batch * CTX_LEN
    in_specs, args = [q_spec], [q_arr]
    for n, kmap, vmap in segs:
        in_specs += [pl.BlockSpec((n, dq), kmap), pl.BlockSpec((n, LANES), vmap)]
        args += [k_arr, v_arr]
    for a in extra:
        in_specs.append(pl.BlockSpec(a.shape, lambda b, h, j: (0, 0)))
        args.append(a)
    return pl.pallas_call(
        functools.partial(_attn_kernel, mode=mode, nseg=len(segs), scale=scale, lam_init=lam_init,
                          nsplit=TQ_SPLIT if latent else 1),
        grid=grid, in_specs=in_specs, out_specs=o_spec,
        out_shape=jax.ShapeDtypeStruct((rows, heads * LANES), BF16),
        compiler_params=_cp("parallel", "parallel", "arbitrary"),
        name=name,
    )(*args)


def _ret_kernel(ql_ref, qc_ref, kl_ref, kc_ref, vl_ref, vc_ref, gl_ref, gc_ref, dmat_ref, xi_ref, zeta_ref,
                gch_ref, ng_ref, yl_ref, yc_ref, accl_ref, accc_ref, *, n_lat, n_ctx, ctx_out):
    C = RT_CHUNK

    def chunk(q, k, v, R, d):
        s = _dot_nt(q, k) * dmat_ref[d, 0]
        inner = _dot(s.astype(BF16), v)
        cross = _dot(q, R.astype(BF16)) * xi_ref[d, 0]
        kz = (k.astype(F32) * zeta_ref[d, 0]).T.astype(BF16)
        return inner + cross, gch_ref[d, 0] * R + _dot(kz, v)

    for d in range(2):
        R = jnp.zeros((RT_DK, RT_DK), F32)
        for j in range(n_ctx):
            c = j if d == 0 else n_ctx - 1 - j
            sl = pl.ds(c * C, C)
            o, R = chunk(qc_ref[sl, :], kc_ref[sl, :], vc_ref[sl, :], R, d)
            if ctx_out:
                if d == 0:
                    accc_ref[sl, :] = o
                else:
                    accc_ref[sl, :] += o

        for j in range(n_lat):
            c = j if d == 0 else n_lat - 1 - j
            sl = pl.ds(c * C, C)
            o, R = chunk(ql_ref[sl, :], kl_ref[sl, :], vl_ref[sl, :], R, d)
            if d == 0:
                accl_ref[sl, :] = o
            else:
                accl_ref[sl, :] += o

    def post(acc_ref, g_ref, y_ref):
        g = g_ref[...].astype(F32)
        y_ref[...] = (_rms_rows(acc_ref[...], ng_ref[...]) * (g * _sigmoid(g))).astype(y_ref.dtype)

    post(accl_ref, gl_ref, yl_ref)
    if ctx_out:
        post(accc_ref, gc_ref, yc_ref)
    else:
        yc_ref[...] = jnp.zeros_like(yc_ref)


def _retention(qr, kr, p, tabs, norm_g, *, batch, seq, ctx_out):
    nl = batch * seq
    cblk = nl // CTX_LEN
    lat = lambda c0: pl.BlockSpec((seq, LANES), lambda b, h: (b, c0 + h))
    ctx = lambda c0: pl.BlockSpec((CTX_LEN, LANES), lambda b, h: (cblk + b, c0 + h))
    tab = lambda a: pl.BlockSpec((2, 1) + a.shape[2:], lambda b, h: (0, h, 0, 0))
    return pl.pallas_call(
        functools.partial(_ret_kernel, n_lat=seq // RT_CHUNK, n_ctx=CTX_LEN // RT_CHUNK, ctx_out=ctx_out),
        grid=(batch, RT_HEADS),
        in_specs=[lat(0), ctx(0), lat(0), ctx(0), lat(COL_RT_V), ctx(COL_RT_V), lat(COL_RT_G), ctx(COL_RT_G)]
        + [tab(a) for a in tabs] + [pl.BlockSpec((1, LANES), lambda b, h: (0, 0))],
        out_specs=[pl.BlockSpec((seq, LANES), lambda b, h: (b, h)),
                   pl.BlockSpec((CTX_LEN, LANES), lambda b, h: (b, h))],
        out_shape=[jax.ShapeDtypeStruct((nl, RT_HEADS * LANES), BF16),
                   jax.ShapeDtypeStruct((batch * CTX_LEN, RT_HEADS * LANES), BF16)],
        scratch_shapes=[pltpu.VMEM((seq, LANES), F32), pltpu.VMEM((CTX_LEN, LANES), F32)],
        compiler_params=_cp("parallel", "parallel"),
        name="retention",
    )(qr, qr, kr, kr, p, p, p, p, *tabs, norm_g)


def _na_groups(rows):
    plans, variants = [], []
    for gi in range(rows // NA_GROUP):
        r = gi * NA_GROUP + np.arange(NA_GROUP)
        u0 = int(np.clip(gi * NA_GROUP - NA_KH // 2, 0, rows - NA_UNION))
        rs = np.clip(r - NA_KH // 2, 0, rows - NA_KH)
        krow = u0 + np.arange(NA_UNION)
        valid = (krow[None, :] >= rs[:, None]) & (krow[None, :] < rs[:, None] + NA_KH)
        dr = np.where(valid, krow[None, :] - r[:, None] + (NA_KH - 1), 0)
        key = (valid.tobytes(), dr.tobytes())
        for v, (k, _, _) in enumerate(variants):
            if k == key:
                break
        else:
            v = len(variants)
            variants.append((key, valid, dr))
        plans.append((u0, v))
    return tuple(plans), [(valid, dr) for _, valid, dr in variants]


def _na_kernel(q_ref, kl_ref, kc_ref, vl_ref, vc_ref, bias_ref, o_ref, *, plans):
    gq, gk = NA_GROUP * GRID_W, NA_UNION * GRID_W
    kc, vc = kc_ref[...], vc_ref[...]
    for gi, (u0, var) in enumerate(plans):
        q = q_ref[gi * gq:(gi + 1) * gq, :]
        kw = kl_ref[u0 * GRID_W:u0 * GRID_W + gk, :]
        vw = vl_ref[u0 * GRID_W:u0 * GRID_W + gk, :]
        lane = lax.broadcasted_iota(jnp.int32, q.shape, 1)
        outs = []
        for hh in range(2):
            keep = (lane < LANES // 2) if hh == 0 else (lane >= LANES // 2)
            qh = jnp.where(keep, q, jnp.zeros_like(q))
            (el, ec), z = _softmax_parts([_dot_nt(qh, kw) + bias_ref[hh, var], _dot_nt(qh, kc)])
            outs.append((_dot(el.astype(BF16), vw) + _dot(ec.astype(BF16), vc)) / z)
        lo = lax.broadcasted_iota(jnp.int32, outs[0].shape, 1) < LANES // 2
        o_ref[gi * gq:(gi + 1) * gq, :] = jnp.where(lo, outs[0], outs[1]).astype(o_ref.dtype)


def _na_latent(qn, kn, p, bias, plans, *, batch, seq):
    nl = batch * seq
    cblk = nl // CTX_LEN
    hp = NA_HEADS // 2
    return pl.pallas_call(
        functools.partial(_na_kernel, plans=plans),
        grid=(batch, hp),
        in_specs=[pl.BlockSpec((seq, LANES), lambda b, h: (b, h)),
                  pl.BlockSpec((seq, LANES), lambda b, h: (b, h)),
                  pl.BlockSpec((CTX_LEN, LANES), lambda b, h: (cblk + b, h)),
                  pl.BlockSpec((seq, LANES), lambda b, h: (b, COL_NA_V + h)),
                  pl.BlockSpec((CTX_LEN, LANES), lambda b, h: (cblk + b, COL_NA_V + h)),
                  pl.BlockSpec((2,) + bias.shape[1:], lambda b, h: (h, 0, 0, 0))],
        out_specs=pl.BlockSpec((seq, LANES), lambda b, h: (b, h)),
        out_shape=jax.ShapeDtypeStruct((nl, hp * LANES), BF16),
        compiler_params=_cp("parallel", "parallel"),
        name="neighbourhood_attn",
    )(qn, kn, kn, p, p, bias)


def _merge1_kernel(h_ref, ya_ref, yb_ref, yc_ref, yd_ref, wg_ref, wb_ref, bg_ref, m_ref):
    h = h_ref[...]
    acc = None
    for n, y_ref in enumerate((ya_ref, yb_ref, yc_ref, yd_ref)):
        gate = _sigmoid(_dot(h, wg_ref[n]) + bg_ref[n:n + 1, :])
        term = gate * _dot(y_ref[...], wb_ref[n])
        acc = term if acc is None else acc + term
    m_ref[...] = acc.astype(BF16)


def _merge1(h, ys, wg, wb, bg, n_rows):
    d = h.shape[1]
    cb = 512
    return pl.pallas_call(
        _merge1_kernel,
        grid=(d // cb, n_rows // TB),
        in_specs=[pl.BlockSpec((TB, d), lambda j, i: (i, 0))]
        + [pl.BlockSpec((TB, BRANCH_W), lambda j, i: (i, 0))] * N_BRANCH
        + [pl.BlockSpec((N_BRANCH, d, cb), lambda j, i: (0, 0, j)),
           pl.BlockSpec((N_BRANCH, BRANCH_W, cb), lambda j, i: (0, 0, j)),
           pl.BlockSpec((N_BRANCH, cb), lambda j, i: (0, j))],
        out_specs=pl.BlockSpec((TB, cb), lambda j, i: (i, j)),
        out_shape=jax.ShapeDtypeStruct((n_rows, d), BF16),
        compiler_params=_cp("parallel", "parallel"),
        name="gated_branch_sum",
    )(h, *ys, wg, wb, bg)


def _merge2_kernel(m_ref, wo_ref, xa_ref, xb_ref, mod_ref, g_ref, xo_ref, h2t_ref, *, n_a):
    def body(x_ref):
        x = x_ref[...] + mod_ref[0, 2:3, :] * _dot(m_ref[...], wo_ref[...])
        xo_ref[...] = x
        h2 = _rms_rows(x, g_ref[...]) * (1.0 + mod_ref[0, 4:5, :]) + mod_ref[0, 3:4, :]
        h2t_ref[...] = h2.T.astype(BF16)

    pl.when(pl.program_id(0) < n_a)(functools.partial(body, xa_ref))
    pl.when(pl.program_id(0) >= n_a)(functools.partial(body, xb_ref))


def _merge2(m, wo, xa, xb, xb_row0, n_lat, mod, g, n_rows, mod_idx):
    d = xa.shape[1]
    amap, bmap = _split_rows(n_lat // TB, xb_row0 // TB)
    return pl.pallas_call(
        functools.partial(_merge2_kernel, n_a=n_lat // TB),
        grid=(n_rows // TB,),
        in_specs=[pl.BlockSpec((TB, d), lambda i: (i, 0)),
                  pl.BlockSpec((d, d), lambda i: (0, 0)),
                  pl.BlockSpec((TB, d), amap),
                  pl.BlockSpec((TB, d), bmap),
                  pl.BlockSpec((1, 6, d), lambda i: (mod_idx(i), 0, 0)),
                  pl.BlockSpec((1, d), lambda i: (0, 0))],
        out_specs=[pl.BlockSpec((TB, d), lambda i: (i, 0)),
                   pl.BlockSpec((d, TB), lambda i: (0, i))],
        out_shape=[jax.ShapeDtypeStruct((n_rows, d), F32),
                   jax.ShapeDtypeStruct((d, n_rows), BF16)],
        compiler_params=_cp("parallel"),
        name="out_proj_residual_norm2",
    )(m, wo, xa, xb, mod, g)


def _merge_exchange(n):
    comps, p = [], 1
    while p < n:
        k = p
        while k >= 1:
            for j in range(k % p, n - k, 2 * k):
                for i in range(min(k, n - j - k)):
                    if (i + j) // (2 * p) == (i + j + k) // (2 * p):
                        comps.append((i + j, i + j + k))
            k //= 2
        p *= 2
    return comps


def _top_values(x, k):
    sub = 8
    cols = [x[v * sub:(v + 1) * sub, :] for v in range(x.shape[0] // sub)]
    for i, j in _merge_exchange(len(cols)):
        cols[i], cols[j] = jnp.maximum(cols[i], cols[j]), jnp.minimum(cols[i], cols[j])
    tops = []
    for it in range(k):
        m = jnp.max(cols[0], axis=0, keepdims=True)
        tops.append(m)
        if it == k - 1:
            break
        hit = cols[0] == m
        depth = min(len(cols), k - it)
        for d in range(depth - 1):
            cols[d] = jnp.where(hit, cols[d + 1], cols[d])
        if depth == len(cols):
            cols[depth - 1] = jnp.where(hit, -jnp.inf, cols[depth - 1])
    return tops


def _route_kernel(h2t_ref, wqt_ref, sk_ref, s1_ref, th_ref, e1_ref, e2_ref):
    qt = _dot(wqt_ref[...], h2t_ref[...]).astype(BF16)
    nk, k, half = PEER_NKEYS, PEER_TOPK, PEER_TOPK // 2
    for h in range(PEER_HEADS):
        s1 = _dot(sk_ref[2 * h], qt[(2 * h) * nk:(2 * h + 1) * nk, :])
        s2 = _dot(sk_ref[2 * h + 1], qt[(2 * h + 1) * nk:(2 * h + 2) * nk, :])
        t1, t2 = _top_values(s1, k), _top_values(s2, k)
        t1_hi = jnp.concatenate(t1[half:], axis=0)
        t2_lo, t2_hi = jnp.concatenate(t2[:half], axis=0), jnp.concatenate(t2[half:], axis=0)
        cand = jnp.concatenate([t1[a] + t2_lo for a in range(half)] + [t1[0] + t2_hi, t1_hi + t2[0]], axis=0)
        sel = _top_values(cand, k)
        tau = sel[-1]
        z = functools.reduce(jnp.add, [jnp.exp(c - sel[0]) for c in sel])
        th = jnp.full(s2.shape, jnp.inf, F32)
        for a in range(half):
            th = jnp.where(t1[a] + s2 >= tau, t1[a], th)
        th0 = jnp.full(tau.shape, jnp.inf, F32)
        for a in range(half, k):
            th0 = jnp.where(t1[a] + t2[0] >= tau, t1[a], th0)
        th = jnp.where(s2 == t2[0], jnp.minimum(th, th0), th)
        s1_ref[h] = s1
        th_ref[h] = th
        e1_ref[h] = jnp.exp(s1 - t1[0]) / z
        e2_ref[h] = jnp.exp(s2 - t2[0])


def _route(h2t, wqt, sk, n_rows):
    d = h2t.shape[0]
    t = 256
    big = pl.BlockSpec((PEER_HEADS, PEER_NKEYS, t), lambda i: (0, 0, i))
    return pl.pallas_call(
        _route_kernel,
        grid=(n_rows // t,),
        in_specs=[pl.BlockSpec((d, t), lambda i: (0, i)),
                  pl.BlockSpec(wqt.shape, lambda i: (0, 0)),
                  pl.BlockSpec(sk.shape, lambda i: (0, 0, 0))],
        out_specs=[big] * 4,
        out_shape=[jax.ShapeDtypeStruct((PEER_HEADS, PEER_NKEYS, n_rows), F32)] * 4,
        compiler_params=_cp("parallel"),
        name="peer_route",
    )(h2t, wqt, sk)


def _experts_kernel(h2t_ref, u_ref, vt_ref, s1_ref, e1_ref, th_ref, e2_ref, out_ref,
                    a_even, a_odd, w_even, w_odd, *, n_items):
    s = pl.program_id(0)
    nk = PEER_NKEYS
    ni = PEER_EC // nk
    nchunk = PEER_EXPERTS // PEER_EC

    @pl.when(s == 0)
    def _():
        a_odd[...] = jnp.zeros_like(a_odd)
        w_odd[...] = jnp.zeros_like(w_odd)

    @pl.when((s == 0) | ((s >= 2) & ((s - 2) % nchunk == 0)))
    def _():
        out_ref[...] = jnp.zeros_like(out_ref)

    half = jnp.where((s >= 1) & (s <= n_items), 0.5, 0.0)
    gi = 4
    assert nchunk % 2 == 0 and (2 * ni) % 8 == 0

    def gate_block(a_ref, w_ref, r0, lt, jg, ig):
        ls = slice(lt * LANES, (lt + 1) * LANES)
        js = slice(jg * 16, (jg + 1) * 16)
        gs = [None] * gi
        for h in range(PEER_HEADS):
            th, e2 = th_ref[h, js, ls], e2_ref[h, js, ls]
            for ii in range(gi):
                r = slice(r0 + ig * gi + ii, r0 + ig * gi + ii + 1)
                term = jnp.where(s1_ref[h, r, ls] >= th, e2 * e1_ref[h, r, ls], 0.0)
                gs[ii] = term if gs[ii] is None else gs[ii] + term
        for ii in range(gi):
            r = ig * gi + ii
            rows = slice(r * nk + jg * 16, r * nk + (jg + 1) * 16)
            a = a_ref[rows, ls]
            act = half * a * (1.0 + lax.erf(a * (2.0 ** -0.5)))
            w_ref[rows, ls] = (gs[ii] * act).astype(BF16)

    def step(a_new, a_old, w_new, w_old, step_parity):
        r0 = (((step_parity - 1) % 2) * ni) % 8
        d, t = out_ref.shape
        n_out, n_pre = 8, 2 * (PEER_EC // 256)

        def out_unit(m):
            rs = slice(m * d // n_out, (m + 1) * d // n_out)
            out_ref[rs, :] += _dot(vt_ref[rs, :], w_old[...])

        def pre_unit(q):
            rs = slice((q // 2) * 256, (q // 2 + 1) * 256)
            cs = slice((q % 2) * t // 2, (q % 2 + 1) * t // 2)
            a_new[rs, cs] = _dot(u_ref[rs, :], h2t_ref[:, cs])

        blocks = [(lt, jg, ig) for lt in range(t // LANES) for jg in range(nk // 16) for ig in range(ni // gi)]
        units = ([functools.partial(out_unit, m) for m in range(n_out)]
                 + [functools.partial(pre_unit, q) for q in range(n_pre)])
        for k, unit in enumerate(units):
            unit()
            for blk in blocks[k * len(blocks) // len(units):(k + 1) * len(blocks) // len(units)]:
                gate_block(a_old, w_new, r0, *blk)

    @pl.when(s % 2 == 0)
    def _():
        step(a_even, a_odd, w_even, w_odd, 0)

    @pl.when(s % 2 == 1)
    def _():
        step(a_odd, a_even, w_odd, w_even, 1)


def _experts(h2t, u, vt, s1, e1, th, e2, n_rows):
    d = h2t.shape[0]
    t, ec = PEER_T, PEER_EC
    nchunk = PEER_EXPERTS // ec
    n_items = (n_rows // t) * nchunk
    item = lambda s, lag: jnp.clip(s - lag, 0, n_items - 1)
    res = lambda a: pl.BlockSpec(a.shape[:2] + (t,), lambda s: (0, 0, item(s, 1) // nchunk))
    ni = ec // PEER_NKEYS
    row = lambda a: pl.BlockSpec((PEER_HEADS, 8, t),
                                 lambda s: (0, (item(s, 1) % nchunk) * ni // 8, item(s, 1) // nchunk))
    return pl.pallas_call(
        functools.partial(_experts_kernel, n_items=n_items),
        grid=(n_items + 2,),
        in_specs=[pl.BlockSpec((d, t), lambda s: (0, item(s, 0) // nchunk)),
                  pl.BlockSpec((ec, d), lambda s: (item(s, 0) % nchunk, 0)),
                  pl.BlockSpec((d, ec), lambda s: (0, item(s, 2) % nchunk)),
                  row(s1), row(e1), res(th), res(e2)],
        out_specs=pl.BlockSpec((d, t), lambda s: (0, item(s, 2) // nchunk)),
        out_shape=jax.ShapeDtypeStruct((d, n_rows), F32),
        scratch_shapes=[pltpu.VMEM((ec, t), F32), pltpu.VMEM((ec, t), F32),
                        pltpu.VMEM((ec, t), BF16), pltpu.VMEM((ec, t), BF16)],
        compiler_params=_cp("arbitrary"),
        name="peer_experts",
    )(h2t, u, vt, s1, e1, th, e2)


def _resid_kernel(x_ref, pt_ref, mod_ref, o_ref):
    o_ref[...] = x_ref[...] + mod_ref[0, 5:6, :] * pt_ref[...].T


def _peer_residual(x, pt, mod, n_rows, mod_idx):
    d = x.shape[1]
    return pl.pallas_call(
        _resid_kernel,
        grid=(n_rows // TB,),
        in_specs=[pl.BlockSpec((TB, d), lambda i: (i, 0)),
                  pl.BlockSpec((d, TB), lambda i: (0, i)),
                  pl.BlockSpec((1, 6, d), lambda i: (mod_idx(i), 0, 0))],
        out_specs=pl.BlockSpec((TB, d), lambda i: (i, 0)),
        out_shape=jax.ShapeDtypeStruct((n_rows, d), F32),
        compiler_params=_cp("parallel"),
        name="peer_residual",
    )(x, pt, mod)


def _rope_table(n_tok, dim):
    m = dim // 2
    inv = ROPE_BASE ** (-jnp.arange(0, m, 2, dtype=F32) / m)
    t = jnp.arange(n_tok)
    row = (t // GRID_W).astype(F32)
    col = (t % GRID_W).astype(F32)
    ar, ac = row[:, None] * inv, col[:, None] * inv
    ang = jnp.concatenate([ar, ar, ac, ac], axis=-1)
    sign = np.where((np.arange(dim) % (dim // 2)) < dim // 4, -1.0, 1.0).astype(np.float32)
    reps = LANES // dim
    cos = jnp.tile(jnp.cos(ang), (1, reps))
    sin = jnp.tile(jnp.sin(ang) * sign, (1, reps))
    return (jnp.concatenate([cos, jnp.ones((TB, LANES), F32)], axis=0),
            jnp.concatenate([sin, jnp.zeros((TB, LANES), F32)], axis=0))


def _na_bias_table(rpb, variants):
    cols = np.arange(GRID_W)
    start = np.clip(cols - NA_KW // 2, 0, GRID_W - NA_KW)
    valid_c = (cols[None, :] >= start[:, None]) & (cols[None, :] < start[:, None] + NA_KW)
    dc = np.clip(cols[None, :] - cols[:, None] + (NA_KW - 1), 0, 2 * NA_KW - 2)
    g = rpb.astype(F32)[:, :, dc]
    tabs = []
    for valid_r, dr in variants:
        ok = valid_r[None, :, :, None, None] & valid_c[None, None, None]
        t = jnp.where(ok, g[:, dr], NEG).transpose(0, 1, 3, 2, 4)
        tabs.append(t.reshape(NA_HEADS, NA_GROUP * GRID_W, NA_UNION * GRID_W))
    return jnp.stack(tabs, axis=1)


def _retention_tables(decay_logit):
    C = RT_CHUNK
    log_g = jax.nn.log_sigmoid(decay_logit.astype(F32))
    n = jnp.arange(C, dtype=F32)
    diff = n[:, None] - n[None, :]
    dm = jnp.where(diff >= 0, jnp.exp(log_g[:, :, None, None] * jnp.maximum(diff, 0.0)), 0.0)
    dmat = jnp.stack([dm[0], jnp.swapaxes(dm[1], -1, -2)])
    xi = jnp.exp(log_g[:, :, None] * (n + 1.0))
    zeta = jnp.exp(log_g[:, :, None] * (C - 1.0 - n))
    xi = jnp.stack([xi[0], xi[1, :, ::-1]])
    zeta = jnp.stack([zeta[0], zeta[1, :, ::-1]])
    bc = lambda a: jnp.broadcast_to(a[..., None], a.shape + (LANES,))
    gch = jnp.broadcast_to(jnp.exp(log_g * C)[:, :, None, None], (2, RT_HEADS, 1, LANES))
    return dmat, bc(xi), bc(zeta), gch


def kernel(x, c, ctx, c_ctx, w_mod, b_mod, norm1_g, norm2_g, w_in, diff_lambda, diff_qk_g, diff_sub_g,
           ret_decay, ret_norm_g, na_qk_g, na_rpb, mla_q_norm_g, mla_kv_norm_g, w_uq, w_ukv, mla_qk_g,
           w_branch, w_gate, b_gate, w_o, peer_w_query, peer_sub_keys, peer_u, peer_v):
    B, S, D = x.shape
    depth = w_mod.shape[0]
    assert D == D_MODEL and ctx.shape[1] == CTX_LEN and S % TB_IN == 0 and (B * CTX_LEN) % TB_IN == 0
    assert S % (GRID_W * NA_GROUP) == 0 and S // GRID_W >= NA_UNION
    NL, NC = B * S, B * CTX_LEN
    NT = NL + NC

    mod_idx = lambda i: jnp.minimum(i * TB // S, B)
    mod_idx_in = lambda i: jnp.minimum(i * TB_IN // S, B)
    nlb, spb = NL // TB, S // TB
    rope_idx = lambda i: jnp.where(i < nlb, i % spb, spb)

    xa, xb, xb_row0 = x.reshape(NL, D), ctx.reshape(NC, D), 0
    c_all = jnp.zeros((16, D), F32).at[:B].set(c).at[B].set(c_ctx)
    cos_a, sin_a = _rope_table(S, DA_HEAD)
    cos_r, sin_r = _rope_table(S, RT_DK)
    na_plans, na_variants = _na_groups(S // GRID_W)
    tile2 = lambda g: jnp.tile(g.astype(F32), (1, 2))
    pad_qk = lambda g: jnp.pad(g.astype(F32), (0, 2 * LANES - g.shape[0])).reshape(1, 2 * LANES)

    for l in range(depth):
        last = l == depth - 1
        lam_init = 0.8 - 0.6 * math.exp(-0.3 * l)
        n_tok = NL if last else NT

        w_in_bf = jnp.pad(w_in[l], ((0, 0), (0, IN_PAD - IN_WIDTH))).astype(BF16)
        wuq = w_uq[l].reshape(MLA_Q_LORA, MLA_HEADS, MLA_NOPE + MLA_ROPE)
        wuq = jnp.pad(wuq, ((0, 0), (0, 0), (0, 2 * LANES - MLA_NOPE - MLA_ROPE)))
        wuq = wuq.reshape(MLA_Q_LORA, MLA_HEADS * 2 * LANES).astype(BF16)
        wukv = w_ukv[l].reshape(MLA_KV_LORA, MLA_HEADS, MLA_NOPE + MLA_V)
        wuk = wukv[:, :, :MLA_NOPE].reshape(MLA_KV_LORA, MLA_HEADS * MLA_NOPE).astype(BF16)
        wuv = wukv[:, :, MLA_NOPE:].reshape(MLA_KV_LORA, MLA_HEADS * MLA_V).astype(BF16)
        gains = [tile2(diff_qk_g[l]), tile2(na_qk_g[l]), mla_q_norm_g[l].reshape(1, -1),
                 mla_kv_norm_g[l].reshape(1, -1), pad_qk(mla_qk_g[l, 0]), pad_qk(mla_qk_g[l, 1])]
        wg_bf, wb_bf, wo_bf = w_gate[l].astype(BF16), w_branch[l].astype(BF16), w_o[l].astype(BF16)
        wqt = peer_w_query[l].T.astype(BF16)
        sk = peer_sub_keys[l].reshape(PEER_HEADS * 2, PEER_NKEYS, PEER_DQ // 2).astype(BF16)
        u_bf = peer_u[l].astype(BF16)
        vt_bf = peer_v[l].T.astype(BF16)

        mod = _modulation(c_all, w_mod[l], b_mod[l]).reshape(16, 6, D)
        h1, p = _in_proj(xa, xb, xb_row0, NL, mod, norm1_g[l].reshape(1, D), w_in_bf, NT, mod_idx_in)
        qa, ka, qr, kr, qn, kn, qm, km, vm = _prep(
            p, (cos_a, sin_a, cos_r, sin_r), gains, (wuq, wuk, wuv), NT, rope_idx)

        sub_g = diff_sub_g[l].reshape(1, -1)
        mla_scale = (MLA_NOPE + MLA_ROPE) ** -0.5
        ret_tabs = _retention_tables(ret_decay[l])
        common = dict(batch=B, seq=S)
        ya = _attention(qa, ka, p, (diff_lambda[l], sub_g), mode="diff", heads=DA_HEADS, dq=LANES,
                        v_col0=COL_DA_V, latent=True, lam_init=lam_init, name="diff_attn", **common)
        yb, yb_c = _retention(qr, kr, p, ret_tabs, ret_norm_g[l].reshape(1, -1), ctx_out=not last, **common)
        yc = _na_latent(qn, kn, p, _na_bias_table(na_rpb[l], na_variants), na_plans, **common)
        yd = _attention(qm, km, vm, (), mode="plain", heads=MLA_HEADS, dq=2 * LANES, v_col0=0,
                        latent=True, scale=mla_scale, name="latent_attn", **common)
        ys = [ya, yb, yc, yd]
        if not last:
            ya_c = _attention(qa, ka, p, (diff_lambda[l], sub_g), mode="diff", heads=DA_HEADS, dq=LANES,
                              v_col0=COL_DA_V, latent=False, lam_init=lam_init, name="diff_attn_ctx", **common)
            yc_c = _attention(qn, kn, p, (), mode="pair", heads=NA_HEADS // 2, dq=LANES, v_col0=COL_NA_V,
                              latent=False, name="ctx_attn", **common)
            yd_c = _attention(qm, km, vm, (), mode="plain", heads=MLA_HEADS, dq=2 * LANES, v_col0=0,
                              latent=False, scale=mla_scale, name="latent_attn_ctx", **common)
            ys = [jnp.concatenate([a, b], axis=0) for a, b in zip(ys, (ya_c, yb_c, yc_c, yd_c))]

        m = _merge1(h1, ys, wg_bf, wb_bf, b_gate[l], n_tok)
        x_mid, h2t = _merge2(m, wo_bf, xa, xb, xb_row0, NL, mod, norm2_g[l].reshape(1, D), n_tok, mod_idx)
        s1, th, e1, e2 = _route(h2t, wqt, sk, n_tok)
        pt = _experts(h2t, u_bf, vt_bf, s1, e1, th, e2, n_tok)
        xa = xb = _peer_residual(x_mid, pt, mod, n_tok, mod_idx)
        xb_row0 = NL

    return xa.reshape(B, S, D)
```

```python
import functools
import math

import jax
import jax.numpy as jnp
import numpy as np
from jax import lax
from jax.experimental import pallas as pl
from jax.experimental.pallas import tpu as pltpu

F32 = jnp.float32
BF16 = jnp.bfloat16

D_MODEL = 2048
CTX_LEN = 256
GRID_W = 64
EPS = 1e-6
ROPE_BASE = 10000.0
N_BRANCH = 4
BRANCH_W = 512
DA_HEADS, DA_HEAD = 4, 64
RT_HEADS, RT_DK, RT_CHUNK = 4, 128, 128
NA_HEADS, NA_HEAD, NA_KH, NA_KW = 8, 64, 8, 16
MLA_HEADS, MLA_Q_LORA, MLA_KV_LORA, MLA_NOPE, MLA_ROPE, MLA_V = 4, 512, 256, 128, 64, 128
PEER_HEADS, PEER_NKEYS, PEER_DQ, PEER_TOPK = 8, 128, 256, 16
PEER_EXPERTS = PEER_NKEYS * PEER_NKEYS

LANES = 128
IN_WIDTH = 5952
IN_PAD = 6144
COL_DA_V = 1024 // LANES
COL_RT_V = 2560 // LANES
COL_RT_G = 3072 // LANES
COL_NA_V = 4608 // LANES
OFF_RT_Q, OFF_RT_K = 1536, 2048
OFF_NA_Q, OFF_NA_K = 3584, 4096
OFF_MLA_CQ, OFF_MLA_CKV, OFF_MLA_KR = 5120, 5632, 5888

TB = 512
TB_IN = 1024
TQ = 1024
TQ_SPLIT = 4
PEER_T = 512
PEER_EC = 512
NA_GROUP = 4
NA_UNION = NA_GROUP + NA_KH
NEG = -1e30
VMEM_LIMIT = 56 * 1024 * 1024


def _cp(*sem):
    return pltpu.CompilerParams(dimension_semantics=sem, vmem_limit_bytes=VMEM_LIMIT)


def _dot(a, b):
    return jnp.dot(a, b, preferred_element_type=F32)


def _dot_nt(a, b):
    return lax.dot_general(a, b, (((1,), (1,)), ((), ())), preferred_element_type=F32)


def _sigmoid(z):
    return 1.0 / (1.0 + jnp.exp(-z))


def _rms_rows(x, g):
    return x * lax.rsqrt(jnp.mean(x * x, axis=-1, keepdims=True) + EPS) * g


def _mod_kernel(c_ref, w_ref, b_ref, o_ref):
    c = c_ref[...]
    a = (c * _sigmoid(c)).astype(BF16)
    o_ref[...] = _dot(a, w_ref[0].astype(BF16)) + b_ref[0]


def _modulation(c_all, w_mod, b_mod, layer):
    rows, d = c_all.shape
    depth, _, n = w_mod.shape
    cb = 1024
    return pl.pallas_call(
        _mod_kernel,
        grid=(n // cb,),
        in_specs=[pl.BlockSpec((rows, d), lambda j: (0, 0)),
                  pl.BlockSpec((1, d, cb), lambda j: (layer, 0, j)),
                  pl.BlockSpec((1, 1, cb), lambda j: (layer, 0, j))],
        out_specs=pl.BlockSpec((rows, cb), lambda j: (0, j)),
        out_shape=jax.ShapeDtypeStruct((rows, n), F32),
        compiler_params=_cp("parallel"),
        name="adaln_mod",
    )(c_all, w_mod, b_mod.reshape(depth, 1, n))


def _split_rows(n_a, b_off):
    return (lambda i, *_: (jnp.minimum(i, n_a - 1), 0),
            lambda i, *_: (jnp.maximum(i - n_a, 0) + b_off, 0))


def _inproj_kernel(xa_ref, xb_ref, mod_ref, g_ref, w_ref, h_ref, p_ref, *, n_a):
    def prologue(x_ref):
        y = _rms_rows(x_ref[...], g_ref[...])
        h_ref[...] = (y * (1.0 + mod_ref[0, 1:2, :]) + mod_ref[0, 0:1, :]).astype(BF16)

    first = pl.program_id(1) == 0
    pl.when(first & (pl.program_id(0) < n_a))(functools.partial(prologue, xa_ref))
    pl.when(first & (pl.program_id(0) >= n_a))(functools.partial(prologue, xb_ref))
    p_ref[...] = _dot(h_ref[...], w_ref[0]).astype(BF16)


def _in_proj(xa, xb, xb_row0, n_lat, mod, g, w_bf, layer, n_rows, mod_idx):
    d = xa.shape[1]
    cb = 512
    amap, bmap = _split_rows(n_lat // TB_IN, xb_row0 // TB_IN)
    return pl.pallas_call(
        functools.partial(_inproj_kernel, n_a=n_lat // TB_IN),
        grid=(n_rows // TB_IN, IN_PAD // cb),
        in_specs=[pl.BlockSpec((TB_IN, d), amap),
                  pl.BlockSpec((TB_IN, d), bmap),
                  pl.BlockSpec((1, 6, d), lambda i, j: (mod_idx(i), 0, 0)),
                  pl.BlockSpec((1, d), lambda i, j: (0, 0)),
                  pl.BlockSpec((1, d, cb), lambda i, j: (layer, 0, j))],
        out_specs=[pl.BlockSpec((TB_IN, d), lambda i, j: (i, 0)),
                   pl.BlockSpec((TB_IN, cb), lambda i, j: (i, j))],
        out_shape=[jax.ShapeDtypeStruct((n_rows, d), BF16),
                   jax.ShapeDtypeStruct((n_rows, IN_PAD), BF16)],
        compiler_params=_cp("parallel", "arbitrary"),
        name="norm1_in_proj",
    )(xa, xb, mod, g, w_bf)


def _rot_partner(x, q):
    lane = lax.broadcasted_iota(jnp.int32, x.shape, 1)
    first = (lane & (2 * q - 1)) < q
    return jnp.where(first, pltpu.roll(x, LANES - q, 1), pltpu.roll(x, q, 1))


def _rope(x, cos, sin_signed, q):
    return x * cos + _rot_partner(x, q) * sin_signed


def _group_sumsq(x, gsz):
    x2 = x * x
    hi = x2.astype(BF16)
    lo = (x2 - hi.astype(F32)).astype(BF16)
    r = lax.broadcasted_iota(jnp.int32, (LANES, LANES), 0) // gsz
    c = lax.broadcasted_iota(jnp.int32, (LANES, LANES), 1) // gsz
    ones = jnp.where(r == c, 1.0, 0.0).astype(BF16)
    return _dot(hi, ones) + _dot(lo, ones)


def _prep_kernel(p_ref, cosa_ref, sina_ref, cosr_ref, sinr_ref, gda_ref, gna_ref, gq_ref, gkv_ref,
                 gmq_ref, gmk_ref, wuq_ref, wuk_ref, wuv_ref,
                 qa_ref, ka_ref, qr_ref, kr_ref, qn_ref, kn_ref, qm_ref, km_ref, vm_ref):
    cosa, sina = cosa_ref[...], sina_ref[...]
    cosr, sinr = cosr_ref[...], sinr_ref[...]
    for t in range(4):
        sl = slice(t * LANES, (t + 1) * LANES)
        for off, gi, oref, scale in ((0, 0, qa_ref, DA_HEAD ** -0.5), (512, 1, ka_ref, 1.0)):
            x = p_ref[:, off + t * LANES: off + (t + 1) * LANES].astype(F32)
            y = x * lax.rsqrt(_group_sumsq(x, DA_HEAD) * (1.0 / DA_HEAD) + EPS) * gda_ref[gi:gi + 1, :]
            oref[:, sl] = (_rope(y, cosa, sina, DA_HEAD // 4) * scale).astype(BF16)
        x = p_ref[:, OFF_RT_Q + t * LANES: OFF_RT_Q + (t + 1) * LANES].astype(F32)
        qr_ref[:, sl] = _rope(x, cosr, sinr, RT_DK // 4).astype(BF16)
        x = p_ref[:, OFF_RT_K + t * LANES: OFF_RT_K + (t + 1) * LANES].astype(F32) * (RT_DK ** -0.5)
        kr_ref[:, sl] = _rope(x, cosr, sinr, RT_DK // 4).astype(BF16)
        for off, gi, oref, scale in ((OFF_NA_Q, 0, qn_ref, NA_HEAD ** -0.5), (OFF_NA_K, 1, kn_ref, 1.0)):
            x = p_ref[:, off + t * LANES: off + (t + 1) * LANES].astype(F32)
            y = x * lax.rsqrt(_group_sumsq(x, NA_HEAD) * (1.0 / NA_HEAD) + EPS) * gna_ref[gi:gi + 1, :]
            oref[:, sl] = (y * scale).astype(BF16)
    cq = p_ref[:, OFF_MLA_CQ:OFF_MLA_CQ + MLA_Q_LORA].astype(F32)
    q = _dot(_rms_rows(cq, gq_ref[...]).astype(BF16), wuq_ref[...])
    ckv = p_ref[:, OFF_MLA_CKV:OFF_MLA_CKV + MLA_KV_LORA].astype(F32)
    ckv_n = _rms_rows(ckv, gkv_ref[...]).astype(BF16)
    k_nope = _dot(ckv_n, wuk_ref[...])
    vm_ref[...] = _dot(ckv_n, wuv_ref[...]).astype(BF16)
    kr = p_ref[:, OFF_MLA_KR:OFF_MLA_KR + LANES].astype(F32)
    kr_ss = jnp.sum(kr * kr, axis=-1, keepdims=True)
    inv_n = 1.0 / (MLA_NOPE + MLA_ROPE)
    for h in range(MLA_HEADS):
        a, b = h * 2 * LANES, h * 2 * LANES + LANES
        q0, q1 = q[:, a:b], q[:, b:b + LANES]
        r = lax.rsqrt((jnp.sum(q0 * q0, axis=-1, keepdims=True)
                       + jnp.sum(q1 * q1, axis=-1, keepdims=True)) * inv_n + EPS)
        qm_ref[:, a:b] = (q0 * r * gmq_ref[:, 0:LANES]).astype(BF16)
        qm_ref[:, b:b + LANES] = _rope(q1 * r * gmq_ref[:, LANES:2 * LANES], cosa, sina,
                                       MLA_ROPE // 4).astype(BF16)
        k0 = k_nope[:, h * LANES:(h + 1) * LANES]
        r = lax.rsqrt((jnp.sum(k0 * k0, axis=-1, keepdims=True) + kr_ss) * inv_n + EPS)
        km_ref[:, a:b] = (k0 * r * gmk_ref[:, 0:LANES]).astype(BF16)
        km_ref[:, b:b + LANES] = _rope(kr * r * gmk_ref[:, LANES:2 * LANES], cosa, sina,
                                       MLA_ROPE // 4).astype(BF16)


def _prep(p, tabs, gains, weights, n_rows, rope_idx):
    row = lambda w: pl.BlockSpec((TB, w), lambda i: (i, 0))
    tab = pl.BlockSpec((TB, LANES), lambda i: (rope_idx(i), 0))
    full = lambda a: pl.BlockSpec(a.shape, lambda i: (0,) * a.ndim)
    outs = [512] * 6 + [1024, 1024, 512]
    return pl.pallas_call(
        _prep_kernel,
        grid=(n_rows // TB,),
        in_specs=[row(IN_PAD)] + [tab] * 4 + [full(a) for a in gains] + [full(a) for a in weights],
        out_specs=[row(w) for w in outs],
        out_shape=[jax.ShapeDtypeStruct((n_rows, w), BF16) for w in outs],
        compiler_params=_cp("parallel"),
        name="mixer_prep",
    )(p, *tabs, *gains, *weights)


def _softmax_parts(scores):
    m = functools.reduce(jnp.maximum, [jnp.max(s, axis=-1, keepdims=True) for s in scores])
    es = [jnp.exp(s - m) for s in scores]
    z = functools.reduce(jnp.add, [jnp.sum(e, axis=-1, keepdims=True) for e in es])
    return es, z


def _attn_kernel(*refs, mode, nseg, scale, lam_init, nsplit):
    q_ref = refs[0]
    ks = [refs[1 + 2 * s] for s in range(nseg)]
    vs = [refs[2 + 2 * s] for s in range(nseg)]
    o_ref = refs[-1]
    rows = q_ref.shape[0] // nsplit

    def scores(g):
        q = q_ref[g * rows:(g + 1) * rows, :]
        if mode == "plain":
            return [[_dot_nt(q, k[...]) * scale for k in ks]]
        lane = lax.broadcasted_iota(jnp.int32, q.shape, 1)
        halves = [jnp.where(lane < LANES // 2, q, jnp.zeros_like(q)),
                  jnp.where(lane >= LANES // 2, q, jnp.zeros_like(q))]
        return [[_dot_nt(qh, k[...]) for k in ks] for qh in halves]

    def weighted(es):
        return functools.reduce(jnp.add, [_dot(e.astype(BF16), v[...]) for e, v in zip(es, vs)])

    if mode == "diff":
        lam_ref, subg_ref = refs[1 + 2 * nseg], refs[2 + 2 * nseg]
        lv = lam_ref[...]
        lam = (jnp.exp(jnp.sum(lv[0:1] * lv[1:2], axis=-1, keepdims=True))
               - jnp.exp(jnp.sum(lv[2:3] * lv[3:4], axis=-1, keepdims=True)) + lam_init)

    def finish(g, sc):
        parts = [_softmax_parts(s) for s in sc]
        if mode == "plain":
            (es, z), = parts
            y = weighted(es) / z
        elif mode == "pair":
            outs = [weighted(es) / z for es, z in parts]
            lo = lax.broadcasted_iota(jnp.int32, outs[0].shape, 1) < LANES // 2
            y = jnp.where(lo, outs[0], outs[1])
        else:
            (e0, z0), (e1, z1) = parts
            w0, w1 = 1.0 / z0, lam / z1
            y = weighted([a * w0 - b * w1 for a, b in zip(e0, e1)])
            y = _rms_rows(y, subg_ref[...]) * (1.0 - lam_init)
        o_ref[g * rows:(g + 1) * rows, :] = y.astype(o_ref.dtype)

    sc = [scores(g) for g in range(nsplit)]
    for g in range(nsplit):
        finish(g, sc[g])


def _attention(q_arr, k_arr, v_arr, extra, *, mode, heads, dq, v_col0, batch, seq, latent, scale=1.0,
               lam_init=0.0, name):
    nl = batch * seq
    cblk = nl // CTX_LEN
    if latent:
        nq = seq // TQ
        grid = (batch, heads, nq)
        q_spec = pl.BlockSpec((TQ, dq), lambda b, h, j: (b * nq + j, h))
        segs = [(CTX_LEN, lambda b, h, j: (cblk + b, h), lambda b, h, j: (cblk + b, v_col0 + h)),
                (seq, lambda b, h, j: (b, h), lambda b, h, j: (b, v_col0 + h))]
        o_spec = pl.BlockSpec((TQ, LANES), lambda b, h, j: (b * nq + j, h))
        rows = nl
    else:
        grid = (batch, heads, 1)
        q_spec = pl.BlockSpec((CTX_LEN, dq), lambda b, h, j: (cblk + b, h))
        segs = [(CTX_LEN, lambda b, h, j: (cblk + b, h), lambda b, h, j: (cblk + b, v_col0 + h))]
        o_spec = pl.BlockSpec((CTX_LEN, LANES), lambda b, h, j: (b, h))
        rows = batch * CTX_LEN
    in_specs, args = [q_spec], [q_arr]
    for n, kmap, vmap in segs:
        in_specs += [pl.BlockSpec((n, dq), kmap), pl.BlockSpec((n, LANES), vmap)]
        args += [k_arr, v_arr]
    for a in extra:
        in_specs.append(pl.BlockSpec(a.shape, lambda b, h, j: (0, 0)))
        args.append(a)
    return pl.pallas_call(
        functools.partial(_attn_kernel, mode=mode, nseg=len(segs), scale=scale, lam_init=lam_init,
                          nsplit=TQ_SPLIT if latent else 1),
        grid=grid, in_specs=in_specs, out_specs=o_spec,
        out_shape=jax.ShapeDtypeStruct((rows, heads * LANES), BF16),
        compiler_params=_cp("parallel", "parallel", "arbitrary"),
        name=name,
    )(*args)


def _ret_kernel(ql_ref, qc_ref, kl_ref, kc_ref, vl_ref, vc_ref, gl_ref, gc_ref, dmat_ref, xi_ref, zeta_ref,
                gch_ref, ng_ref, yl_ref, yc_ref, accl_ref, accc_ref, *, n_lat, n_ctx, ctx_out):
    C = RT_CHUNK

    def chunk(q, k, v, R, d):
        s = _dot_nt(q, k) * dmat_ref[d, 0]
        inner = _dot(s.astype(BF16), v)
        cross = _dot(q, R.astype(BF16)) * xi_ref[d, 0]
        kz = (k.astype(F32) * zeta_ref[d, 0]).T.astype(BF16)
        return inner + cross, gch_ref[d, 0] * R + _dot(kz, v)

    for d in range(2):
        R = jnp.zeros((RT_DK, RT_DK), F32)
        for j in range(n_ctx):
            c = j if d == 0 else n_ctx - 1 - j
            sl = pl.ds(c * C, C)
            o, R = chunk(qc_ref[sl, :], kc_ref[sl, :], vc_ref[sl, :], R, d)
            if ctx_out:
                if d == 0:
                    accc_ref[sl, :] = o
                else:
                    accc_ref[sl, :] += o

        for j in range(n_lat):
            c = j if d == 0 else n_lat - 1 - j
            sl = pl.ds(c * C, C)
            o, R = chunk(ql_ref[sl, :], kl_ref[sl, :], vl_ref[sl, :], R, d)
            if d == 0:
                accl_ref[sl, :] = o
            else:
                accl_ref[sl, :] += o

    def post(acc_ref, g_ref, y_ref):
        g = g_ref[...].astype(F32)
        y_ref[...] = (_rms_rows(acc_ref[...], ng_ref[...]) * (g * _sigmoid(g))).astype(y_ref.dtype)

    post(accl_ref, gl_ref, yl_ref)
    if ctx_out:
        post(accc_ref, gc_ref, yc_ref)
    else:
        yc_ref[...] = jnp.zeros_like(yc_ref)


def _retention(qr, kr, p, tabs, norm_g, *, batch, seq, ctx_out):
    nl = batch * seq
    cblk = nl // CTX_LEN
    lat = lambda c0: pl.BlockSpec((seq, LANES), lambda b, h: (b, c0 + h))
    ctx = lambda c0: pl.BlockSpec((CTX_LEN, LANES), lambda b, h: (cblk + b, c0 + h))
    tab = lambda a: pl.BlockSpec((2, 1) + a.shape[2:], lambda b, h: (0, h, 0, 0))
    return pl.pallas_call(
        functools.partial(_ret_kernel, n_lat=seq // RT_CHUNK, n_ctx=CTX_LEN // RT_CHUNK, ctx_out=ctx_out),
        grid=(batch, RT_HEADS),
        in_specs=[lat(0), ctx(0), lat(0), ctx(0), lat(COL_RT_V), ctx(COL_RT_V), lat(COL_RT_G), ctx(COL_RT_G)]
        + [tab(a) for a in tabs] + [pl.BlockSpec((1, LANES), lambda b, h: (0, 0))],
        out_specs=[pl.BlockSpec((seq, LANES), lambda b, h: (b, h)),
                   pl.BlockSpec((CTX_LEN, LANES), lambda b, h: (b, h))],
        out_shape=[jax.ShapeDtypeStruct((nl, RT_HEADS * LANES), BF16),
                   jax.ShapeDtypeStruct((batch * CTX_LEN, RT_HEADS * LANES), BF16)],
        scratch_shapes=[pltpu.VMEM((seq, LANES), F32), pltpu.VMEM((CTX_LEN, LANES), F32)],
        compiler_params=_cp("parallel", "parallel"),
        name="retention",
    )(qr, qr, kr, kr, p, p, p, p, *tabs, norm_g)


def _na_groups(rows):
    plans, variants = [], []
    for gi in range(rows // NA_GROUP):
        r = gi * NA_GROUP + np.arange(NA_GROUP)
        u0 = int(np.clip(gi * NA_GROUP - NA_KH // 2, 0, rows - NA_UNION))
        rs = np.clip(r - NA_KH // 2, 0, rows - NA_KH)
        krow = u0 + np.arange(NA_UNION)
        valid = (krow[None, :] >= rs[:, None]) & (krow[None, :] < rs[:, None] + NA_KH)
        dr = np.where(valid, krow[None, :] - r[:, None] + (NA_KH - 1), 0)
        key = (valid.tobytes(), dr.tobytes())
        for v, (k, _, _) in enumerate(variants):
            if k == key:
                break
        else:
            v = len(variants)
            variants.append((key, valid, dr))
        plans.append((u0, v))
    return tuple(plans), [(valid, dr) for _, valid, dr in variants]


def _na_kernel(q_ref, kl_ref, kc_ref, vl_ref, vc_ref, bias_ref, o_ref, *, plans):
    gq, gk = NA_GROUP * GRID_W, NA_UNION * GRID_W
    kc, vc = kc_ref[...], vc_ref[...]
    for gi, (u0, var) in enumerate(plans):
        q = q_ref[gi * gq:(gi + 1) * gq, :]
        kw = kl_ref[u0 * GRID_W:u0 * GRID_W + gk, :]
        vw = vl_ref[u0 * GRID_W:u0 * GRID_W + gk, :]
        lane = lax.broadcasted_iota(jnp.int32, q.shape, 1)
        outs = []
        for hh in range(2):
            keep = (lane < LANES // 2) if hh == 0 else (lane >= LANES // 2)
            qh = jnp.where(keep, q, jnp.zeros_like(q))
            (el, ec), z = _softmax_parts([_dot_nt(qh, kw) + bias_ref[hh, var], _dot_nt(qh, kc)])
            outs.append((_dot(el.astype(BF16), vw) + _dot(ec.astype(BF16), vc)) / z)
        lo = lax.broadcasted_iota(jnp.int32, outs[0].shape, 1) < LANES // 2
        o_ref[gi * gq:(gi + 1) * gq, :] = jnp.where(lo, outs[0], outs[1]).astype(o_ref.dtype)


def _na_latent(qn, kn, p, bias, plans, *, batch, seq):
    nl = batch * seq
    cblk = nl // CTX_LEN
    hp = NA_HEADS // 2
    return pl.pallas_call(
        functools.partial(_na_kernel, plans=plans),
        grid=(batch, hp),
        in_specs=[pl.BlockSpec((seq, LANES), lambda b, h: (b, h)),
                  pl.BlockSpec((seq, LANES), lambda b, h: (b, h)),
                  pl.BlockSpec((CTX_LEN, LANES), lambda b, h: (cblk + b, h)),
                  pl.BlockSpec((seq, LANES), lambda b, h: (b, COL_NA_V + h)),
                  pl.BlockSpec((CTX_LEN, LANES), lambda b, h: (cblk + b, COL_NA_V + h)),
                  pl.BlockSpec((2,) + bias.shape[1:], lambda b, h: (h, 0, 0, 0))],
        out_specs=pl.BlockSpec((seq, LANES), lambda b, h: (b, h)),
        out_shape=jax.ShapeDtypeStruct((nl, hp * LANES), BF16),
        compiler_params=_cp("parallel", "parallel"),
        name="neighbourhood_attn",
    )(qn, kn, kn, p, p, bias)


def _merge1_kernel(h_ref, ya_ref, yb_ref, yc_ref, yd_ref, wg_ref, wb_ref, bg_ref, m_ref):
    h = h_ref[...]
    acc = None
    for n, y_ref in enumerate((ya_ref, yb_ref, yc_ref, yd_ref)):
        gate = _sigmoid(_dot(h, wg_ref[0, n]) + bg_ref[0, n:n + 1, :])
        term = gate * _dot(y_ref[...], wb_ref[0, n])
        acc = term if acc is None else acc + term
    m_ref[...] = acc.astype(BF16)


def _merge1(h, ys, wg, wb, bg, layer, n_rows):
    d = h.shape[1]
    cb = 512
    return pl.pallas_call(
        _merge1_kernel,
        grid=(d // cb, n_rows // TB),
        in_specs=[pl.BlockSpec((TB, d), lambda j, i: (i, 0))]
        + [pl.BlockSpec((TB, BRANCH_W), lambda j, i: (i, 0))] * N_BRANCH
        + [pl.BlockSpec((1, N_BRANCH, d, cb), lambda j, i: (layer, 0, 0, j)),
           pl.BlockSpec((1, N_BRANCH, BRANCH_W, cb), lambda j, i: (layer, 0, 0, j)),
           pl.BlockSpec((1, N_BRANCH, cb), lambda j, i: (layer, 0, j))],
        out_specs=pl.BlockSpec((TB, cb), lambda j, i: (i, j)),
        out_shape=jax.ShapeDtypeStruct((n_rows, d), BF16),
        compiler_params=_cp("parallel", "parallel"),
        name="gated_branch_sum",
    )(h, *ys, wg, wb, bg)


def _merge2_kernel(m_ref, wo_ref, xa_ref, xb_ref, mod_ref, g_ref, xo_ref, h2t_ref, *, n_a):
    def body(x_ref):
        x = x_ref[...] + mod_ref[0, 2:3, :] * _dot(m_ref[...], wo_ref[0])
        xo_ref[...] = x
        h2 = _rms_rows(x, g_ref[...]) * (1.0 + mod_ref[0, 4:5, :]) + mod_ref[0, 3:4, :]
        h2t_ref[...] = h2.T.astype(BF16)

    pl.when(pl.program_id(0) < n_a)(functools.partial(body, xa_ref))
    pl.when(pl.program_id(0) >= n_a)(functools.partial(body, xb_ref))


def _merge2(m, wo, layer, xa, xb, xb_row0, n_lat, mod, g, n_rows, mod_idx):
    d = xa.shape[1]
    amap, bmap = _split_rows(n_lat // TB, xb_row0 // TB)
    return pl.pallas_call(
        functools.partial(_merge2_kernel, n_a=n_lat // TB),
        grid=(n_rows // TB,),
        in_specs=[pl.BlockSpec((TB, d), lambda i: (i, 0)),
                  pl.BlockSpec((1, d, d), lambda i: (layer, 0, 0)),
                  pl.BlockSpec((TB, d), amap),
                  pl.BlockSpec((TB, d), bmap),
                  pl.BlockSpec((1, 6, d), lambda i: (mod_idx(i), 0, 0)),
                  pl.BlockSpec((1, d), lambda i: (0, 0))],
        out_specs=[pl.BlockSpec((TB, d), lambda i: (i, 0)),
                   pl.BlockSpec((d, TB), lambda i: (0, i))],
        out_shape=[jax.ShapeDtypeStruct((n_rows, d), F32),
                   jax.ShapeDtypeStruct((d, n_rows), BF16)],
        compiler_params=_cp("parallel"),
        name="out_proj_residual_norm2",
    )(m, wo, xa, xb, mod, g)


def _merge_exchange(n):
    comps, p = [], 1
    while p < n:
        k = p
        while k >= 1:
            for j in range(k % p, n - k, 2 * k):
                for i in range(min(k, n - j - k)):
                    if (i + j) // (2 * p) == (i + j + k) // (2 * p):
                        comps.append((i + j, i + j + k))
            k //= 2
        p *= 2
    return comps


def _top_values(x, k):
    sub = 8
    cols = [x[v * sub:(v + 1) * sub, :] for v in range(x.shape[0] // sub)]
    for i, j in _merge_exchange(len(cols)):
        cols[i], cols[j] = jnp.maximum(cols[i], cols[j]), jnp.minimum(cols[i], cols[j])
    tops = []
    for it in range(k):
        m = jnp.max(cols[0], axis=0, keepdims=True)
        tops.append(m)
        if it == k - 1:
            break
        hit = cols[0] == m
        depth = min(len(cols), k - it)
        for d in range(depth - 1):
            cols[d] = jnp.where(hit, cols[d + 1], cols[d])
        if depth == len(cols):
            cols[depth - 1] = jnp.where(hit, -jnp.inf, cols[depth - 1])
    return tops


def _route_kernel(h2t_ref, wqt_ref, sk_ref, s1_ref, th_ref, e1_ref, e2_ref):
    qt = _dot(wqt_ref[0], h2t_ref[...]).astype(BF16)
    nk, k, half = PEER_NKEYS, PEER_TOPK, PEER_TOPK // 2
    for h in range(PEER_HEADS):
        s1 = _dot(sk_ref[0, 2 * h], qt[(2 * h) * nk:(2 * h + 1) * nk, :])
        s2 = _dot(sk_ref[0, 2 * h + 1], qt[(2 * h + 1) * nk:(2 * h + 2) * nk, :])
        t1, t2 = _top_values(s1, k), _top_values(s2, k)
        t1_hi = jnp.concatenate(t1[half:], axis=0)
        t2_lo, t2_hi = jnp.concatenate(t2[:half], axis=0), jnp.concatenate(t2[half:], axis=0)
        cand = jnp.concatenate([t1[a] + t2_lo for a in range(half)] + [t1[0] + t2_hi, t1_hi + t2[0]], axis=0)
        sel = _top_values(cand, k)
        tau = sel[-1]
        z = functools.reduce(jnp.add, [jnp.exp(c - sel[0]) for c in sel])
        th = jnp.full(s2.shape, jnp.inf, F32)
        for a in range(half):
            th = jnp.where(t1[a] + s2 >= tau, t1[a], th)
        th0 = jnp.full(tau.shape, jnp.inf, F32)
        for a in range(half, k):
            th0 = jnp.where(t1[a] + t2[0] >= tau, t1[a], th0)
        th = jnp.where(s2 == t2[0], jnp.minimum(th, th0), th)
        s1_ref[h] = s1
        th_ref[h] = th
        e1_ref[h] = jnp.exp(s1 - t1[0]) / z
        e2_ref[h] = jnp.exp(s2 - t2[0])


def _route(h2t, wqt, sk, layer, n_rows):
    d = h2t.shape[0]
    t = 256
    big = pl.BlockSpec((PEER_HEADS, PEER_NKEYS, t), lambda i: (0, 0, i))
    return pl.pallas_call(
        _route_kernel,
        grid=(n_rows // t,),
        in_specs=[pl.BlockSpec((d, t), lambda i: (0, i)),
                  pl.BlockSpec((1,) + wqt.shape[1:], lambda i: (layer, 0, 0)),
                  pl.BlockSpec((1,) + sk.shape[1:], lambda i: (layer, 0, 0, 0))],
        out_specs=[big] * 4,
        out_shape=[jax.ShapeDtypeStruct((PEER_HEADS, PEER_NKEYS, n_rows), F32)] * 4,
        compiler_params=_cp("parallel"),
        name="peer_route",
    )(h2t, wqt, sk)


def _experts_kernel(h2t_ref, u_ref, vt_ref, s1_ref, e1_ref, th_ref, e2_ref, out_ref,
                    a_even, a_odd, w_even, w_odd, *, n_items):
    s = pl.program_id(0)
    nk = PEER_NKEYS
    ni = PEER_EC // nk
    nchunk = PEER_EXPERTS // PEER_EC

    @pl.when(s == 0)
    def _():
        a_odd[...] = jnp.zeros_like(a_odd)
        w_odd[...] = jnp.zeros_like(w_odd)

    @pl.when((s == 0) | ((s >= 2) & ((s - 2) % nchunk == 0)))
    def _():
        out_ref[...] = jnp.zeros_like(out_ref)

    half = jnp.where((s >= 1) & (s <= n_items), 0.5, 0.0)
    gi = 4
    assert nchunk % 2 == 0 and (2 * ni) % 8 == 0

    def gate_block(a_ref, w_ref, r0, lt, jg, ig):
        ls = slice(lt * LANES, (lt + 1) * LANES)
        js = slice(jg * 16, (jg + 1) * 16)
        gs = [None] * gi
        for h in range(PEER_HEADS):
            th, e2 = th_ref[h, js, ls], e2_ref[h, js, ls]
            for ii in range(gi):
                r = slice(r0 + ig * gi + ii, r0 + ig * gi + ii + 1)
                term = jnp.where(s1_ref[h, r, ls] >= th, e2 * e1_ref[h, r, ls], 0.0)
                gs[ii] = term if gs[ii] is None else gs[ii] + term
        for ii in range(gi):
            r = ig * gi + ii
            rows = slice(r * nk + jg * 16, r * nk + (jg + 1) * 16)
            a = a_ref[rows, ls]
            act = half * a * (1.0 + lax.erf(a * (2.0 ** -0.5)))
            w_ref[rows, ls] = (gs[ii] * act).astype(BF16)

    def step(a_new, a_old, w_new, w_old, step_parity):
        r0 = (((step_parity - 1) % 2) * ni) % 8
        d, t = out_ref.shape
        n_out, n_pre = 8, 2 * (PEER_EC // 256)

        def out_unit(m):
            rs = slice(m * d // n_out, (m + 1) * d // n_out)
            out_ref[rs, :] += _dot(vt_ref[0, rs, :], w_old[...])

        def pre_unit(q):
            rs = slice((q // 2) * 256, (q // 2 + 1) * 256)
            cs = slice((q % 2) * t // 2, (q % 2 + 1) * t // 2)
            a_new[rs, cs] = _dot(u_ref[0, rs, :], h2t_ref[:, cs])

        blocks = [(lt, jg, ig) for lt in range(t // LANES) for jg in range(nk // 16) for ig in range(ni // gi)]
        units = ([functools.partial(out_unit, m) for m in range(n_out)]
                 + [functools.partial(pre_unit, q) for q in range(n_pre)])
        for k, unit in enumerate(units):
            unit()
            for blk in blocks[k * len(blocks) // len(units):(k + 1) * len(blocks) // len(units)]:
                gate_block(a_old, w_new, r0, *blk)

    @pl.when(s % 2 == 0)
    def _():
        step(a_even, a_odd, w_even, w_odd, 0)

    @pl.when(s % 2 == 1)
    def _():
        step(a_odd, a_even, w_odd, w_even, 1)


def _experts(h2t, u, vt, layer, s1, e1, th, e2, n_rows):
    d = h2t.shape[0]
    t, ec = PEER_T, PEER_EC
    nchunk = PEER_EXPERTS // ec
    n_items = (n_rows // t) * nchunk
    item = lambda s, lag: jnp.clip(s - lag, 0, n_items - 1)
    res = lambda a: pl.BlockSpec(a.shape[:2] + (t,), lambda s: (0, 0, item(s, 1) // nchunk))
    ni = ec // PEER_NKEYS
    row = lambda a: pl.BlockSpec((PEER_HEADS, 8, t),
                                 lambda s: (0, (item(s, 1) % nchunk) * ni // 8, item(s, 1) // nchunk))
    return pl.pallas_call(
        functools.partial(_experts_kernel, n_items=n_items),
        grid=(n_items + 2,),
        in_specs=[pl.BlockSpec((d, t), lambda s: (0, item(s, 0) // nchunk)),
                  pl.BlockSpec((1, ec, d), lambda s: (layer, item(s, 0) % nchunk, 0)),
                  pl.BlockSpec((1, d, ec), lambda s: (layer, 0, item(s, 2) % nchunk)),
                  row(s1), row(e1), res(th), res(e2)],
        out_specs=pl.BlockSpec((d, t), lambda s: (0, item(s, 2) // nchunk)),
        out_shape=jax.ShapeDtypeStruct((d, n_rows), F32),
        scratch_shapes=[pltpu.VMEM((ec, t), F32), pltpu.VMEM((ec, t), F32),
                        pltpu.VMEM((ec, t), BF16), pltpu.VMEM((ec, t), BF16)],
        compiler_params=_cp("arbitrary"),
        name="peer_experts",
    )(h2t, u, vt, s1, e1, th, e2)


def _resid_kernel(x_ref, pt_ref, mod_ref, o_ref):
    o_ref[...] = x_ref[...] + mod_ref[0, 5:6, :] * pt_ref[...].T


def _peer_residual(x, pt, mod, n_rows, mod_idx):
    d = x.shape[1]
    return pl.pallas_call(
        _resid_kernel,
        grid=(n_rows // TB,),
        in_specs=[pl.BlockSpec((TB, d), lambda i: (i, 0)),
                  pl.BlockSpec((d, TB), lambda i: (0, i)),
                  pl.BlockSpec((1, 6, d), lambda i: (mod_idx(i), 0, 0))],
        out_specs=pl.BlockSpec((TB, d), lambda i: (i, 0)),
        out_shape=jax.ShapeDtypeStruct((n_rows, d), F32),
        compiler_params=_cp("parallel"),
        name="peer_residual",
    )(x, pt, mod)


def _rope_table(n_tok, dim):
    m = dim // 2
    inv = ROPE_BASE ** (-jnp.arange(0, m, 2, dtype=F32) / m)
    t = jnp.arange(n_tok)
    row = (t // GRID_W).astype(F32)
    col = (t % GRID_W).astype(F32)
    ar, ac = row[:, None] * inv, col[:, None] * inv
    ang = jnp.concatenate([ar, ar, ac, ac], axis=-1)
    sign = np.where((np.arange(dim) % (dim // 2)) < dim // 4, -1.0, 1.0).astype(np.float32)
    reps = LANES // dim
    cos = jnp.tile(jnp.cos(ang), (1, reps))
    sin = jnp.tile(jnp.sin(ang) * sign, (1, reps))
    return (jnp.concatenate([cos, jnp.ones((TB, LANES), F32)], axis=0),
            jnp.concatenate([sin, jnp.zeros((TB, LANES), F32)], axis=0))


def _na_bias_table(rpb, variants):
    cols = np.arange(GRID_W)
    start = np.clip(cols - NA_KW // 2, 0, GRID_W - NA_KW)
    valid_c = (cols[None, :] >= start[:, None]) & (cols[None, :] < start[:, None] + NA_KW)
    dc = np.clip(cols[None, :] - cols[:, None] + (NA_KW - 1), 0, 2 * NA_KW - 2)
    g = rpb.astype(F32)[:, :, dc]
    tabs = []
    for valid_r, dr in variants:
        ok = valid_r[None, :, :, None, None] & valid_c[None, None, None]
        t = jnp.where(ok, g[:, dr], NEG).transpose(0, 1, 3, 2, 4)
        tabs.append(t.reshape(NA_HEADS, NA_GROUP * GRID_W, NA_UNION * GRID_W))
    return jnp.stack(tabs, axis=1)


def _retention_tables(decay_logit):
    C = RT_CHUNK
    log_g = jax.nn.log_sigmoid(decay_logit.astype(F32))
    n = jnp.arange(C, dtype=F32)
    diff = n[:, None] - n[None, :]
    dm = jnp.where(diff >= 0, jnp.exp(log_g[:, :, None, None] * jnp.maximum(diff, 0.0)), 0.0)
    dmat = jnp.stack([dm[0], jnp.swapaxes(dm[1], -1, -2)])
    xi = jnp.exp(log_g[:, :, None] * (n + 1.0))
    zeta = jnp.exp(log_g[:, :, None] * (C - 1.0 - n))
    xi = jnp.stack([xi[0], xi[1, :, ::-1]])
    zeta = jnp.stack([zeta[0], zeta[1, :, ::-1]])
    bc = lambda a: jnp.broadcast_to(a[..., None], a.shape + (LANES,))
    gch = jnp.broadcast_to(jnp.exp(log_g * C)[:, :, None, None], (2, RT_HEADS, 1, LANES))
    return dmat, bc(xi), bc(zeta), gch


def kernel(x, c, ctx, c_ctx, w_mod, b_mod, norm1_g, norm2_g, w_in, diff_lambda, diff_qk_g, diff_sub_g,
           ret_decay, ret_norm_g, na_qk_g, na_rpb, mla_q_norm_g, mla_kv_norm_g, w_uq, w_ukv, mla_qk_g,
           w_branch, w_gate, b_gate, w_o, peer_w_query, peer_sub_keys, peer_u, peer_v):
    B, S, D = x.shape
    depth = w_mod.shape[0]
    assert D == D_MODEL and ctx.shape[1] == CTX_LEN and S % TB_IN == 0 and (B * CTX_LEN) % TB_IN == 0
    assert S % (GRID_W * NA_GROUP) == 0 and S // GRID_W >= NA_UNION
    NL, NC = B * S, B * CTX_LEN
    NT = NL + NC

    mod_idx = lambda i: jnp.minimum(i * TB // S, B)
    mod_idx_in = lambda i: jnp.minimum(i * TB_IN // S, B)
    nlb, spb = NL // TB, S // TB
    rope_idx = lambda i: jnp.where(i < nlb, i % spb, spb)

    xa, xb, xb_row0 = x.reshape(NL, D), ctx.reshape(NC, D), 0
    c_all = jnp.zeros((16, D), F32).at[:B].set(c).at[B].set(c_ctx)
    cos_a, sin_a = _rope_table(S, DA_HEAD)
    cos_r, sin_r = _rope_table(S, RT_DK)
    na_plans, na_variants = _na_groups(S // GRID_W)
    tile2 = lambda g: jnp.tile(g.astype(F32), (1, 2))
    pad_qk = lambda g: jnp.pad(g.astype(F32), (0, 2 * LANES - g.shape[0])).reshape(1, 2 * LANES)

    w_in_bf = jnp.pad(w_in, ((0, 0), (0, 0), (0, IN_PAD - IN_WIDTH))).astype(BF16)
    wg_bf, wb_bf, wo_bf = w_gate.astype(BF16), w_branch.astype(BF16), w_o.astype(BF16)
    wqt = jnp.swapaxes(peer_w_query, 1, 2).astype(BF16)
    sk = peer_sub_keys.reshape(depth, PEER_HEADS * 2, PEER_NKEYS, PEER_DQ // 2).astype(BF16)
    u_bf = peer_u.astype(BF16)
    vt_bf = jnp.swapaxes(peer_v, 1, 2).astype(BF16)

    for l in range(depth):
        last = l == depth - 1
        lam_init = 0.8 - 0.6 * math.exp(-0.3 * l)
        n_tok = NL if last else NT

        wuq = w_uq[l].reshape(MLA_Q_LORA, MLA_HEADS, MLA_NOPE + MLA_ROPE)
        wuq = jnp.pad(wuq, ((0, 0), (0, 0), (0, 2 * LANES - MLA_NOPE - MLA_ROPE)))
        wuq = wuq.reshape(MLA_Q_LORA, MLA_HEADS * 2 * LANES).astype(BF16)
        wukv = w_ukv[l].reshape(MLA_KV_LORA, MLA_HEADS, MLA_NOPE + MLA_V)
        wuk = wukv[:, :, :MLA_NOPE].reshape(MLA_KV_LORA, MLA_HEADS * MLA_NOPE).astype(BF16)
        wuv = wukv[:, :, MLA_NOPE:].reshape(MLA_KV_LORA, MLA_HEADS * MLA_V).astype(BF16)
        gains = [tile2(diff_qk_g[l]), tile2(na_qk_g[l]), mla_q_norm_g[l].reshape(1, -1),
                 mla_kv_norm_g[l].reshape(1, -1), pad_qk(mla_qk_g[l, 0]), pad_qk(mla_qk_g[l, 1])]

        mod = _modulation(c_all, w_mod, b_mod, l).reshape(16, 6, D)
        h1, p = _in_proj(xa, xb, xb_row0, NL, mod, norm1_g[l].reshape(1, D), w_in_bf, l, NT, mod_idx_in)
        qa, ka, qr, kr, qn, kn, qm, km, vm = _prep(
            p, (cos_a, sin_a, cos_r, sin_r), gains, (wuq, wuk, wuv), NT, rope_idx)

        sub_g = diff_sub_g[l].reshape(1, -1)
        mla_scale = (MLA_NOPE + MLA_ROPE) ** -0.5
        ret_tabs = _retention_tables(ret_decay[l])
        common = dict(batch=B, seq=S)
        ya = _attention(qa, ka, p, (diff_lambda[l], sub_g), mode="diff", heads=DA_HEADS, dq=LANES,
                        v_col0=COL_DA_V, latent=True, lam_init=lam_init, name="diff_attn", **common)
        yb, yb_c = _retention(qr, kr, p, ret_tabs, ret_norm_g[l].reshape(1, -1), ctx_out=not last, **common)
        yc = _na_latent(qn, kn, p, _na_bias_table(na_rpb[l], na_variants), na_plans, **common)
        yd = _attention(qm, km, vm, (), mode="plain", heads=MLA_HEADS, dq=2 * LANES, v_col0=0,
                        latent=True, scale=mla_scale, name="latent_attn", **common)
        ys = [ya, yb, yc, yd]
        if not last:
            ya_c = _attention(qa, ka, p, (diff_lambda[l], sub_g), mode="diff", heads=DA_HEADS, dq=LANES,
                              v_col0=COL_DA_V, latent=False, lam_init=lam_init, name="diff_attn_ctx", **common)
            yc_c = _attention(qn, kn, p, (), mode="pair", heads=NA_HEADS // 2, dq=LANES, v_col0=COL_NA_V,
                              latent=False, name="ctx_attn", **common)
            yd_c = _attention(qm, km, vm, (), mode="plain", heads=MLA_HEADS, dq=2 * LANES, v_col0=0,
                              latent=False, scale=mla_scale, name="latent_attn_ctx", **common)
            ys = [jnp.concatenate([a, b], axis=0) for a, b in zip(ys, (ya_c, yb_c, yc_c, yd_c))]

        m = _merge1(h1, ys, wg_bf, wb_bf, b_gate, l, n_tok)
        x_mid, h2t = _merge2(m, wo_bf, l, xa, xb, xb_row0, NL, mod, norm2_g[l].reshape(1, D), n_tok, mod_idx)
        s1, th, e1, e2 = _route(h2t, wqt, sk, l, n_tok)
        pt = _experts(h2t, u_bf, vt_bf, l, s1, e1, th, e2, n_tok)
        xa = xb = _peer_residual(x_mid, pt, mod, n_tok, mod_idx)
        xb_row0 = NL

    return xa.reshape(B, S, D)
```

```python
import functools
import math

import jax
import jax.numpy as jnp
import numpy as np
from jax import lax
from jax.experimental import pallas as pl
from jax.experimental.pallas import tpu as pltpu

F32 = jnp.float32
BF16 = jnp.bfloat16

D_MODEL = 2048
CTX_LEN = 256
GRID_W = 64
EPS = 1e-6
ROPE_BASE = 10000.0
N_BRANCH = 4
BRANCH_W = 512
DA_HEADS, DA_HEAD = 4, 64
RT_HEADS, RT_DK, RT_CHUNK = 4, 128, 128
NA_HEADS, NA_HEAD, NA_KH, NA_KW = 8, 64, 8, 16
MLA_HEADS, MLA_Q_LORA, MLA_KV_LORA, MLA_NOPE, MLA_ROPE, MLA_V = 4, 512, 256, 128, 64, 128
PEER_HEADS, PEER_NKEYS, PEER_DQ, PEER_TOPK = 8, 128, 256, 16
PEER_EXPERTS = PEER_NKEYS * PEER_NKEYS

LANES = 128
IN_WIDTH = 5952
IN_PAD = 6144
COL_DA_V = 1024 // LANES
COL_RT_V = 2560 // LANES
COL_RT_G = 3072 // LANES
COL_NA_V = 4608 // LANES
OFF_RT_Q, OFF_RT_K = 1536, 2048
OFF_NA_Q, OFF_NA_K = 3584, 4096
OFF_MLA_CQ, OFF_MLA_CKV, OFF_MLA_KR = 5120, 5632, 5888

TB = 512
TB_IN = 1024
TQ = 1024
TQ_SPLIT = 4
PEER_T = 512
PEER_EC = 512
NA_GROUP = 4
NA_UNION = NA_GROUP + NA_KH
NEG = -1e30
VMEM_LIMIT = 56 * 1024 * 1024


def _cp(*sem):
    return pltpu.CompilerParams(dimension_semantics=sem, vmem_limit_bytes=VMEM_LIMIT)


def _dot(a, b):
    return jnp.dot(a, b, preferred_element_type=F32)


def _dot_nt(a, b):
    return lax.dot_general(a, b, (((1,), (1,)), ((), ())), preferred_element_type=F32)


def _sigmoid(z):
    return 1.0 / (1.0 + jnp.exp(-z))


def _rms_rows(x, g):
    return x * lax.rsqrt(jnp.mean(x * x, axis=-1, keepdims=True) + EPS) * g


def _mod_kernel(c_ref, w_ref, b_ref, o_ref):
    c = c_ref[...]
    a = (c * _sigmoid(c)).astype(BF16)
    o_ref[...] = _dot(a, w_ref[0].astype(BF16)) + b_ref[0]


def _modulation(c_all, w_mod, b_mod, layer):
    rows, d = c_all.shape
    depth, _, n = w_mod.shape
    cb = 1024
    return pl.pallas_call(
        _mod_kernel,
        grid=(n // cb,),
        in_specs=[pl.BlockSpec((rows, d), lambda j: (0, 0)),
                  pl.BlockSpec((1, d, cb), lambda j: (layer, 0, j)),
                  pl.BlockSpec((1, 1, cb), lambda j: (layer, 0, j))],
        out_specs=pl.BlockSpec((rows, cb), lambda j: (0, j)),
        out_shape=jax.ShapeDtypeStruct((rows, n), F32),
        compiler_params=_cp("parallel"),
        name="adaln_mod",
    )(c_all, w_mod, b_mod.reshape(depth, 1, n))


def _split_rows(n_a, b_off):
    return (lambda i, *_: (jnp.minimum(i, n_a - 1), 0),
            lambda i, *_: (jnp.maximum(i - n_a, 0) + b_off, 0))


def _inproj_kernel(xa_ref, xb_ref, mod_ref, g_ref, w_ref, h_ref, p_ref, *, n_a):
    def prologue(x_ref):
        y = _rms_rows(x_ref[...], g_ref[...])
        h_ref[...] = (y * (1.0 + mod_ref[0, 1:2, :]) + mod_ref[0, 0:1, :]).astype(BF16)

    first = pl.program_id(1) == 0
    pl.when(first & (pl.program_id(0) < n_a))(functools.partial(prologue, xa_ref))
    pl.when(first & (pl.program_id(0) >= n_a))(functools.partial(prologue, xb_ref))
    p_ref[...] = _dot(h_ref[...], w_ref[0]).astype(BF16)


def _in_proj(xa, xb, xb_row0, n_lat, mod, g, w_bf, layer, n_rows, mod_idx):
    d = xa.shape[1]
    cb = 512
    amap, bmap = _split_rows(n_lat // TB_IN, xb_row0 // TB_IN)
    return pl.pallas_call(
        functools.partial(_inproj_kernel, n_a=n_lat // TB_IN),
        grid=(n_rows // TB_IN, IN_PAD // cb),
        in_specs=[pl.BlockSpec((TB_IN, d), amap),
                  pl.BlockSpec((TB_IN, d), bmap),
                  pl.BlockSpec((1, 6, d), lambda i, j: (mod_idx(i), 0, 0)),
                  pl.BlockSpec((1, d), lambda i, j: (0, 0)),
                  pl.BlockSpec((1, d, cb), lambda i, j: (layer, 0, j))],
        out_specs=[pl.BlockSpec((TB_IN, d), lambda i, j: (i, 0)),
                   pl.BlockSpec((TB_IN, cb), lambda i, j: (i, j))],
        out_shape=[jax.ShapeDtypeStruct((n_rows, d), BF16),
                   jax.ShapeDtypeStruct((n_rows, IN_PAD), BF16)],
        compiler_params=_cp("parallel", "arbitrary"),
        name="norm1_in_proj",
    )(xa, xb, mod, g, w_bf)


def _rot_partner(x, q):
    lane = lax.broadcasted_iota(jnp.int32, x.shape, 1)
    first = (lane & (2 * q - 1)) < q
    return jnp.where(first, pltpu.roll(x, LANES - q, 1), pltpu.roll(x, q, 1))


def _rope(x, cos, sin_signed, q):
    return x * cos + _rot_partner(x, q) * sin_signed


def _group_sumsq(x, gsz):
    x2 = x * x
    hi = x2.astype(BF16)
    lo = (x2 - hi.astype(F32)).astype(BF16)
    r = lax.broadcasted_iota(jnp.int32, (LANES, LANES), 0) // gsz
    c = lax.broadcasted_iota(jnp.int32, (LANES, LANES), 1) // gsz
    ones = jnp.where(r == c, 1.0, 0.0).astype(BF16)
    return _dot(hi, ones) + _dot(lo, ones)


def _prep_kernel(p_ref, cosa_ref, sina_ref, cosr_ref, sinr_ref, gda_ref, gna_ref, gq_ref, gkv_ref,
                 gmq_ref, gmk_ref, wuq_ref, wuk_ref, wuv_ref,
                 qa_ref, ka_ref, qr_ref, kr_ref, qn_ref, kn_ref, qm_ref, km_ref, vm_ref):
    cosa, sina = cosa_ref[...], sina_ref[...]
    cosr, sinr = cosr_ref[...], sinr_ref[...]
    for t in range(4):
        sl = slice(t * LANES, (t + 1) * LANES)
        for off, gi, oref, scale in ((0, 0, qa_ref, DA_HEAD ** -0.5), (512, 1, ka_ref, 1.0)):
            x = p_ref[:, off + t * LANES: off + (t + 1) * LANES].astype(F32)
            y = x * lax.rsqrt(_group_sumsq(x, DA_HEAD) * (1.0 / DA_HEAD) + EPS) * gda_ref[gi:gi + 1, :]
            oref[:, sl] = (_rope(y, cosa, sina, DA_HEAD // 4) * scale).astype(BF16)
        x = p_ref[:, OFF_RT_Q + t * LANES: OFF_RT_Q + (t + 1) * LANES].astype(F32)
        qr_ref[:, sl] = _rope(x, cosr, sinr, RT_DK // 4).astype(BF16)
        x = p_ref[:, OFF_RT_K + t * LANES: OFF_RT_K + (t + 1) * LANES].astype(F32) * (RT_DK ** -0.5)
        kr_ref[:, sl] = _rope(x, cosr, sinr, RT_DK // 4).astype(BF16)
        for off, gi, oref, scale in ((OFF_NA_Q, 0, qn_ref, NA_HEAD ** -0.5), (OFF_NA_K, 1, kn_ref, 1.0)):
            x = p_ref[:, off + t * LANES: off + (t + 1) * LANES].astype(F32)
            y = x * lax.rsqrt(_group_sumsq(x, NA_HEAD) * (1.0 / NA_HEAD) + EPS) * gna_ref[gi:gi + 1, :]
            oref[:, sl] = (y * scale).astype(BF16)
    cq = p_ref[:, OFF_MLA_CQ:OFF_MLA_CQ + MLA_Q_LORA].astype(F32)
    q = _dot(_rms_rows(cq, gq_ref[...]).astype(BF16), wuq_ref[...])
    ckv = p_ref[:, OFF_MLA_CKV:OFF_MLA_CKV + MLA_KV_LORA].astype(F32)
    ckv_n = _rms_rows(ckv, gkv_ref[...]).astype(BF16)
    k_nope = _dot(ckv_n, wuk_ref[...])
    vm_ref[...] = _dot(ckv_n, wuv_ref[...]).astype(BF16)
    kr = p_ref[:, OFF_MLA_KR:OFF_MLA_KR + LANES].astype(F32)
    kr_ss = jnp.sum(kr * kr, axis=-1, keepdims=True)
    inv_n = 1.0 / (MLA_NOPE + MLA_ROPE)
    for h in range(MLA_HEADS):
        a, b = h * 2 * LANES, h * 2 * LANES + LANES
        q0, q1 = q[:, a:b], q[:, b:b + LANES]
        r = lax.rsqrt((jnp.sum(q0 * q0, axis=-1, keepdims=True)
                       + jnp.sum(q1 * q1, axis=-1, keepdims=True)) * inv_n + EPS)
        qm_ref[:, a:b] = (q0 * r * gmq_ref[:, 0:LANES]).astype(BF16)
        qm_ref[:, b:b + LANES] = _rope(q1 * r * gmq_ref[:, LANES:2 * LANES], cosa, sina,
                                       MLA_ROPE // 4).astype(BF16)
        k0 = k_nope[:, h * LANES:(h + 1) * LANES]
        r = lax.rsqrt((jnp.sum(k0 * k0, axis=-1, keepdims=True) + kr_ss) * inv_n + EPS)
        km_ref[:, a:b] = (k0 * r * gmk_ref[:, 0:LANES]).astype(BF16)
        km_ref[:, b:b + LANES] = _rope(kr * r * gmk_ref[:, LANES:2 * LANES], cosa, sina,
                                       MLA_ROPE // 4).astype(BF16)


def _prep(p, tabs, gains, weights, n_rows, rope_idx):
    row = lambda w: pl.BlockSpec((TB, w), lambda i: (i, 0))
    tab = pl.BlockSpec((TB, LANES), lambda i: (rope_idx(i), 0))
    full = lambda a: pl.BlockSpec(a.shape, lambda i: (0,) * a.ndim)
    outs = [512] * 6 + [1024, 1024, 512]
    return pl.pallas_call(
        _prep_kernel,
        grid=(n_rows // TB,),
        in_specs=[row(IN_PAD)] + [tab] * 4 + [full(a) for a in gains] + [full(a) for a in weights],
        out_specs=[row(w) for w in outs],
        out_shape=[jax.ShapeDtypeStruct((n_rows, w), BF16) for w in outs],
        compiler_params=_cp("parallel"),
        name="mixer_prep",
    )(p, *tabs, *gains, *weights)


def _softmax_parts(scores):
    m = functools.reduce(jnp.maximum, [jnp.max(s, axis=-1, keepdims=True) for s in scores])
    es = [jnp.exp(s - m) for s in scores]
    z = functools.reduce(jnp.add, [jnp.sum(e, axis=-1, keepdims=True) for e in es])
    return es, z


def _attn_kernel(*refs, mode, nseg, scale, lam_init, nsplit):
    q_ref = refs[0]
    ks = [refs[1 + 2 * s] for s in range(nseg)]
    vs = [refs[2 + 2 * s] for s in range(nseg)]
    o_ref = refs[-1]
    rows = q_ref.shape[0] // nsplit

    def scores(g):
        q = q_ref[g * rows:(g + 1) * rows, :]
        if mode == "plain":
            return [[_dot_nt(q, k[...]) * scale for k in ks]]
        lane = lax.broadcasted_iota(jnp.int32, q.shape, 1)
        halves = [jnp.where(lane < LANES // 2, q, jnp.zeros_like(q)),
                  jnp.where(lane >= LANES // 2, q, jnp.zeros_like(q))]
        return [[_dot_nt(qh, k[...]) for k in ks] for qh in halves]

    def weighted(es):
        return functools.reduce(jnp.add, [_dot(e.astype(BF16), v[...]) for e, v in zip(es, vs)])

    if mode == "diff":
        lam_ref, subg_ref = refs[1 + 2 * nseg], refs[2 + 2 * nseg]
        lv = lam_ref[...]
        lam = (jnp.exp(jnp.sum(lv[0:1] * lv[1:2], axis=-1, keepdims=True))
               - jnp.exp(jnp.sum(lv[2:3] * lv[3:4], axis=-1, keepdims=True)) + lam_init)

    def finish(g, sc):
        parts = [_softmax_parts(s) for s in sc]
        if mode == "plain":
            (es, z), = parts
            y = weighted(es) / z
        elif mode == "pair":
            outs = [weighted(es) / z for es, z in parts]
            lo = lax.broadcasted_iota(jnp.int32, outs[0].shape, 1) < LANES // 2
            y = jnp.where(lo, outs[0], outs[1])
        else:
            (e0, z0), (e1, z1) = parts
            w0, w1 = 1.0 / z0, lam / z1
            y = weighted([a * w0 - b * w1 for a, b in zip(e0, e1)])
            y = _rms_rows(y, subg_ref[...]) * (1.0 - lam_init)
        o_ref[g * rows:(g + 1) * rows, :] = y.astype(o_ref.dtype)

    sc = [scores(g) for g in range(nsplit)]
    for g in range(nsplit):
        finish(g, sc[g])


def _attention(q_arr, k_arr, v_arr, extra, *, mode, heads, dq, v_col0, batch, seq, latent, scale=1.0,
               lam_init=0.0, name):
    nl = batch * seq
    cblk = nl // CTX_LEN
    if latent:
        nq = seq // TQ
        grid = (batch, heads, nq)
        q_spec = pl.BlockSpec((TQ, dq), lambda b, h, j: (b * nq + j, h))
        segs = [(CTX_LEN, lambda b, h, j: (cblk + b, h), lambda b, h, j: (cblk + b, v_col0 + h)),
                (seq, lambda b, h, j: (b, h), lambda b, h, j: (b, v_col0 + h))]
        o_spec = pl.BlockSpec((TQ, LANES), lambda b, h, j: (b * nq + j, h))
        rows = nl
    else:
        grid = (batch, heads, 1)
        q_spec = pl.BlockSpec((CTX_LEN, dq), lambda b, h, j: (cblk + b, h))
        segs = [(CTX_LEN, lambda b, h, j: (cblk + b, h), lambda b, h, j: (cblk + b, v_col0 + h))]
        o_spec = pl.BlockSpec((CTX_LEN, LANES), lambda b, h, j: (b, h))
        rows = batch * CTX_LEN
    in_specs, args = [q_spec], [q_arr]
    for n, kmap, vmap in segs:
        in_specs += [pl.BlockSpec((n, dq), kmap), pl.BlockSpec((n, LANES), vmap)]
        args += [k_arr, v_arr]
    for a in extra:
        in_specs.append(pl.BlockSpec(a.shape, lambda b, h, j: (0, 0)))
        args.append(a)
    return pl.pallas_call(
        functools.partial(_attn_kernel, mode=mode, nseg=len(segs), scale=scale, lam_init=lam_init,
                          nsplit=TQ_SPLIT if latent else 1),
        grid=grid, in_specs=in_specs, out_specs=o_spec,
        out_shape=jax.ShapeDtypeStruct((rows, heads * LANES), BF16),
        compiler_params=_cp("parallel", "parallel", "arbitrary"),
        name=name,
    )(*args)


def _ret_kernel(ql_ref, qc_ref, kl_ref, kc_ref, vl_ref, vc_ref, gl_ref, gc_ref, dmat_ref, xi_ref, zeta_ref,
                gch_ref, ng_ref, yl_ref, yc_ref, accl_ref, accc_ref, *, n_lat, n_ctx, ctx_out):
    C = RT_CHUNK

    def chunk(q, k, v, R, d):
        s = _dot_nt(q, k) * dmat_ref[d, 0]
        inner = _dot(s.astype(BF16), v)
        cross = _dot(q, R.astype(BF16)) * xi_ref[d, 0]
        kz = (k.astype(F32) * zeta_ref[d, 0]).T.astype(BF16)
        return inner + cross, gch_ref[d, 0] * R + _dot(kz, v)

    for d in range(2):
        R = jnp.zeros((RT_DK, RT_DK), F32)
        for j in range(n_ctx):
            c = j if d == 0 else n_ctx - 1 - j
            sl = pl.ds(c * C, C)
            o, R = chunk(qc_ref[sl, :], kc_ref[sl, :], vc_ref[sl, :], R, d)
            if ctx_out:
                if d == 0:
                    accc_ref[sl, :] = o
                else:
                    accc_ref[sl, :] += o

        for j in range(n_lat):
            c = j if d == 0 else n_lat - 1 - j
            sl = pl.ds(c * C, C)
            o, R = chunk(ql_ref[sl, :], kl_ref[sl, :], vl_ref[sl, :], R, d)
            if d == 0:
                accl_ref[sl, :] = o
            else:
                accl_ref[sl, :] += o

    def post(acc_ref, g_ref, y_ref):
        g = g_ref[...].astype(F32)
        y_ref[...] = (_rms_rows(acc_ref[...], ng_ref[...]) * (g * _sigmoid(g))).astype(y_ref.dtype)

    post(accl_ref, gl_ref, yl_ref)
    if ctx_out:
        post(accc_ref, gc_ref, yc_ref)
    else:
        yc_ref[...] = jnp.zeros_like(yc_ref)


def _retention(qr, kr, p, tabs, norm_g, *, batch, seq, ctx_out):
    nl = batch * seq
    cblk = nl // CTX_LEN
    lat = lambda c0: pl.BlockSpec((seq, LANES), lambda b, h: (b, c0 + h))
    ctx = lambda c0: pl.BlockSpec((CTX_LEN, LANES), lambda b, h: (cblk + b, c0 + h))
    tab = lambda a: pl.BlockSpec((2, 1) + a.shape[2:], lambda b, h: (0, h, 0, 0))
    return pl.pallas_call(
        functools.partial(_ret_kernel, n_lat=seq // RT_CHUNK, n_ctx=CTX_LEN // RT_CHUNK, ctx_out=ctx_out),
        grid=(batch, RT_HEADS),
        in_specs=[lat(0), ctx(0), lat(0), ctx(0), lat(COL_RT_V), ctx(COL_RT_V), lat(COL_RT_G), ctx(COL_RT_G)]
        + [tab(a) for a in tabs] + [pl.BlockSpec((1, LANES), lambda b, h: (0, 0))],
        out_specs=[pl.BlockSpec((seq, LANES), lambda b, h: (b, h)),
                   pl.BlockSpec((CTX_LEN, LANES), lambda b, h: (b, h))],
        out_shape=[jax.ShapeDtypeStruct((nl, RT_HEADS * LANES), BF16),
                   jax.ShapeDtypeStruct((batch * CTX_LEN, RT_HEADS * LANES), BF16)],
        scratch_shapes=[pltpu.VMEM((seq, LANES), F32), pltpu.VMEM((CTX_LEN, LANES), F32)],
        compiler_params=_cp("parallel", "parallel"),
        name="retention",
    )(qr, qr, kr, kr, p, p, p, p, *tabs, norm_g)


def _na_groups(rows):
    plans, variants = [], []
    for gi in range(rows // NA_GROUP):
        r = gi * NA_GROUP + np.arange(NA_GROUP)
        u0 = int(np.clip(gi * NA_GROUP - NA_KH // 2, 0, rows - NA_UNION))
        rs = np.clip(r - NA_KH // 2, 0, rows - NA_KH)
        krow = u0 + np.arange(NA_UNION)
        valid = (krow[None, :] >= rs[:, None]) & (krow[None, :] < rs[:, None] + NA_KH)
        dr = np.where(valid, krow[None, :] - r[:, None] + (NA_KH - 1), 0)
        key = (valid.tobytes(), dr.tobytes())
        for v, (k, _, _) in enumerate(variants):
            if k == key:
                break
        else:
            v = len(variants)
            variants.append((key, valid, dr))
        plans.append((u0, v))
    return tuple(plans), [(valid, dr) for _, valid, dr in variants]


def _na_kernel(q_ref, kl_ref, kc_ref, vl_ref, vc_ref, bias_ref, o_ref, *, plans):
    gq, gk = NA_GROUP * GRID_W, NA_UNION * GRID_W
    kc, vc = kc_ref[...], vc_ref[...]

    def scores(gi):
        u0, var = plans[gi]
        q = q_ref[gi * gq:(gi + 1) * gq, :]
        kw = kl_ref[u0 * GRID_W:u0 * GRID_W + gk, :]
        lane = lax.broadcasted_iota(jnp.int32, q.shape, 1)
        out = []
        for hh in range(2):
            keep = (lane < LANES // 2) if hh == 0 else (lane >= LANES // 2)
            qh = jnp.where(keep, q, jnp.zeros_like(q))
            out.append([_dot_nt(qh, kw) + bias_ref[hh, var], _dot_nt(qh, kc)])
        return out

    def finish(gi, sc):
        u0, _ = plans[gi]
        vw = vl_ref[u0 * GRID_W:u0 * GRID_W + gk, :]
        outs = []
        for s in sc:
            (el, ec), z = _softmax_parts(s)
            outs.append((_dot(el.astype(BF16), vw) + _dot(ec.astype(BF16), vc)) / z)
        lo = lax.broadcasted_iota(jnp.int32, outs[0].shape, 1) < LANES // 2
        o_ref[gi * gq:(gi + 1) * gq, :] = jnp.where(lo, outs[0], outs[1]).astype(o_ref.dtype)

    pending = scores(0)
    for gi in range(len(plans)):
        nxt = scores(gi + 1) if gi + 1 < len(plans) else None
        finish(gi, pending)
        pending = nxt


def _na_latent(qn, kn, p, bias, plans, *, batch, seq):
    nl = batch * seq
    cblk = nl // CTX_LEN
    hp = NA_HEADS // 2
    return pl.pallas_call(
        functools.partial(_na_kernel, plans=plans),
        grid=(batch, hp),
        in_specs=[pl.BlockSpec((seq, LANES), lambda b, h: (b, h)),
                  pl.BlockSpec((seq, LANES), lambda b, h: (b, h)),
                  pl.BlockSpec((CTX_LEN, LANES), lambda b, h: (cblk + b, h)),
                  pl.BlockSpec((seq, LANES), lambda b, h: (b, COL_NA_V + h)),
                  pl.BlockSpec((CTX_LEN, LANES), lambda b, h: (cblk + b, COL_NA_V + h)),
                  pl.BlockSpec((2,) + bias.shape[1:], lambda b, h: (h, 0, 0, 0))],
        out_specs=pl.BlockSpec((seq, LANES), lambda b, h: (b, h)),
        out_shape=jax.ShapeDtypeStruct((nl, hp * LANES), BF16),
        compiler_params=_cp("parallel", "parallel"),
        name="neighbourhood_attn",
    )(qn, kn, kn, p, p, bias)


def _merge1_kernel(h_ref, *refs, n_a):
    lat, ctx = refs[:N_BRANCH], refs[N_BRANCH:2 * N_BRANCH]
    wg_ref, wb_ref, bg_ref, m_ref = refs[2 * N_BRANCH:]

    def body(y_refs):
        h = h_ref[...]
        acc = None
        for n, y_ref in enumerate(y_refs):
            gate = _sigmoid(_dot(h, wg_ref[0, n]) + bg_ref[0, n:n + 1, :])
            term = gate * _dot(y_ref[...], wb_ref[0, n])
            acc = term if acc is None else acc + term
        m_ref[...] = acc.astype(BF16)

    pl.when(pl.program_id(1) < n_a)(functools.partial(body, lat))
    pl.when(pl.program_id(1) >= n_a)(functools.partial(body, ctx))


def _merge1(h, ys_lat, ys_ctx, wg, wb, bg, layer, n_rows):
    d = h.shape[1]
    cb = 512
    n_a = ys_lat[0].shape[0] // TB
    lat_map = lambda j, i: (jnp.minimum(i, n_a - 1), 0)
    ctx_map = lambda j, i: (jnp.maximum(i - n_a, 0), 0)
    return pl.pallas_call(
        functools.partial(_merge1_kernel, n_a=n_a),
        grid=(d // cb, n_rows // TB),
        in_specs=[pl.BlockSpec((TB, d), lambda j, i: (i, 0))]
        + [pl.BlockSpec((TB, BRANCH_W), lat_map)] * N_BRANCH
        + [pl.BlockSpec((TB, BRANCH_W), ctx_map)] * N_BRANCH
        + [pl.BlockSpec((1, N_BRANCH, d, cb), lambda j, i: (layer, 0, 0, j)),
           pl.BlockSpec((1, N_BRANCH, BRANCH_W, cb), lambda j, i: (layer, 0, 0, j)),
           pl.BlockSpec((1, N_BRANCH, cb), lambda j, i: (layer, 0, j))],
        out_specs=pl.BlockSpec((TB, cb), lambda j, i: (i, j)),
        out_shape=jax.ShapeDtypeStruct((n_rows, d), BF16),
        compiler_params=_cp("parallel", "parallel"),
        name="gated_branch_sum",
    )(h, *ys_lat, *ys_ctx, wg, wb, bg)


def _merge2_kernel(m_ref, wo_ref, xa_ref, xb_ref, mod_ref, g_ref, xo_ref, h2t_ref, *, n_a):
    def body(x_ref):
        x = x_ref[...] + mod_ref[0, 2:3, :] * _dot(m_ref[...], wo_ref[0])
        xo_ref[...] = x
        h2 = _rms_rows(x, g_ref[...]) * (1.0 + mod_ref[0, 4:5, :]) + mod_ref[0, 3:4, :]
        h2t_ref[...] = h2.T.astype(BF16)

    pl.when(pl.program_id(0) < n_a)(functools.partial(body, xa_ref))
    pl.when(pl.program_id(0) >= n_a)(functools.partial(body, xb_ref))


def _merge2(m, wo, layer, xa, xb, xb_row0, n_lat, mod, g, n_rows, mod_idx):
    d = xa.shape[1]
    amap, bmap = _split_rows(n_lat // TB, xb_row0 // TB)
    return pl.pallas_call(
        functools.partial(_merge2_kernel, n_a=n_lat // TB),
        grid=(n_rows // TB,),
        in_specs=[pl.BlockSpec((TB, d), lambda i: (i, 0)),
                  pl.BlockSpec((1, d, d), lambda i: (layer, 0, 0)),
                  pl.BlockSpec((TB, d), amap),
                  pl.BlockSpec((TB, d), bmap),
                  pl.BlockSpec((1, 6, d), lambda i: (mod_idx(i), 0, 0)),
                  pl.BlockSpec((1, d), lambda i: (0, 0))],
        out_specs=[pl.BlockSpec((TB, d), lambda i: (i, 0)),
                   pl.BlockSpec((d, TB), lambda i: (0, i))],
        out_shape=[jax.ShapeDtypeStruct((n_rows, d), F32),
                   jax.ShapeDtypeStruct((d, n_rows), BF16)],
        compiler_params=_cp("parallel"),
        name="out_proj_residual_norm2",
    )(m, wo, xa, xb, mod, g)


def _merge_exchange(n):
    comps, p = [], 1
    while p < n:
        k = p
        while k >= 1:
            for j in range(k % p, n - k, 2 * k):
                for i in range(min(k, n - j - k)):
                    if (i + j) // (2 * p) == (i + j + k) // (2 * p):
                        comps.append((i + j, i + j + k))
            k //= 2
        p *= 2
    return comps


def _top_values(x, k):
    sub = 8
    cols = [x[v * sub:(v + 1) * sub, :] for v in range(x.shape[0] // sub)]
    for i, j in _merge_exchange(len(cols)):
        cols[i], cols[j] = jnp.maximum(cols[i], cols[j]), jnp.minimum(cols[i], cols[j])
    tops = []
    for it in range(k):
        m = jnp.max(cols[0], axis=0, keepdims=True)
        tops.append(m)
        if it == k - 1:
            break
        hit = cols[0] == m
        depth = min(len(cols), k - it)
        for d in range(depth - 1):
            cols[d] = jnp.where(hit, cols[d + 1], cols[d])
        if depth == len(cols):
            cols[depth - 1] = jnp.where(hit, -jnp.inf, cols[depth - 1])
    return tops


def _route_kernel(h2t_ref, wqt_ref, sk_ref, s1_ref, th_ref, e1_ref, e2_ref):
    qt = _dot(wqt_ref[0], h2t_ref[...]).astype(BF16)
    nk, k, half = PEER_NKEYS, PEER_TOPK, PEER_TOPK // 2
    for h in range(PEER_HEADS):
        s1 = _dot(sk_ref[0, 2 * h], qt[(2 * h) * nk:(2 * h + 1) * nk, :])
        s2 = _dot(sk_ref[0, 2 * h + 1], qt[(2 * h + 1) * nk:(2 * h + 2) * nk, :])
        t1, t2 = _top_values(s1, k), _top_values(s2, k)
        t1_hi = jnp.concatenate(t1[half:], axis=0)
        t2_lo, t2_hi = jnp.concatenate(t2[:half], axis=0), jnp.concatenate(t2[half:], axis=0)
        cand = jnp.concatenate([t1[a] + t2_lo for a in range(half)] + [t1[0] + t2_hi, t1_hi + t2[0]], axis=0)
        sel = _top_values(cand, k)
        tau = sel[-1]
        z = functools.reduce(jnp.add, [jnp.exp(c - sel[0]) for c in sel])
        th = jnp.full(s2.shape, jnp.inf, F32)
        for a in range(half):
            th = jnp.where(t1[a] + s2 >= tau, t1[a], th)
        th0 = jnp.full(tau.shape, jnp.inf, F32)
        for a in range(half, k):
            th0 = jnp.where(t1[a] + t2[0] >= tau, t1[a], th0)
        th = jnp.where(s2 == t2[0], jnp.minimum(th, th0), th)
        s1_ref[h] = s1
        th_ref[h] = th
        e1_ref[h] = jnp.exp(s1 - t1[0]) / z
        e2_ref[h] = jnp.exp(s2 - t2[0])


def _route(h2t, wqt, sk, layer, n_rows):
    d = h2t.shape[0]
    t = 256
    big = pl.BlockSpec((PEER_HEADS, PEER_NKEYS, t), lambda i: (0, 0, i))
    return pl.pallas_call(
        _route_kernel,
        grid=(n_rows // t,),
        in_specs=[pl.BlockSpec((d, t), lambda i: (0, i)),
                  pl.BlockSpec((1,) + wqt.shape[1:], lambda i: (layer, 0, 0)),
                  pl.BlockSpec((1,) + sk.shape[1:], lambda i: (layer, 0, 0, 0))],
        out_specs=[big] * 4,
        out_shape=[jax.ShapeDtypeStruct((PEER_HEADS, PEER_NKEYS, n_rows), F32)] * 4,
        compiler_params=_cp("parallel"),
        name="peer_route",
    )(h2t, wqt, sk)


def _experts_kernel(h2t_ref, u_ref, vt_ref, s1_ref, e1_ref, th_ref, e2_ref, out_ref,
                    a_even, a_odd, w_even, w_odd, *, n_items):
    s = pl.program_id(0)
    nk = PEER_NKEYS
    ni = PEER_EC // nk
    nchunk = PEER_EXPERTS // PEER_EC

    @pl.when(s == 0)
    def _():
        a_odd[...] = jnp.zeros_like(a_odd)
        w_odd[...] = jnp.zeros_like(w_odd)

    @pl.when((s == 0) | ((s >= 2) & ((s - 2) % nchunk == 0)))
    def _():
        out_ref[...] = jnp.zeros_like(out_ref)

    half = jnp.where((s >= 1) & (s <= n_items), 0.5, 0.0)
    gi = 4
    assert nchunk % 2 == 0 and (2 * ni) % 8 == 0

    def gate_block(a_ref, w_ref, r0, lt, jg, ig):
        ls = slice(lt * LANES, (lt + 1) * LANES)
        js = slice(jg * 16, (jg + 1) * 16)
        gs = [None] * gi
        for h in range(PEER_HEADS):
            th, e2 = th_ref[h, js, ls], e2_ref[h, js, ls]
            for ii in range(gi):
                r = slice(r0 + ig * gi + ii, r0 + ig * gi + ii + 1)
                term = jnp.where(s1_ref[h, r, ls] >= th, e2 * e1_ref[h, r, ls], 0.0)
                gs[ii] = term if gs[ii] is None else gs[ii] + term
        for ii in range(gi):
            r = ig * gi + ii
            rows = slice(r * nk + jg * 16, r * nk + (jg + 1) * 16)
            a = a_ref[rows, ls]
            act = half * a * (1.0 + lax.erf(a * (2.0 ** -0.5)))
            w_ref[rows, ls] = (gs[ii] * act).astype(BF16)

    def step(a_new, a_old, w_new, w_old, step_parity):
        r0 = (((step_parity - 1) % 2) * ni) % 8
        d, t = out_ref.shape
        n_out, n_pre = 8, 2 * (PEER_EC // 256)

        def out_unit(m):
            rs = slice(m * d // n_out, (m + 1) * d // n_out)
            out_ref[rs, :] += _dot(vt_ref[0, rs, :], w_old[...])

        def pre_unit(q):
            rs = slice((q // 2) * 256, (q // 2 + 1) * 256)
            cs = slice((q % 2) * t // 2, (q % 2 + 1) * t // 2)
            a_new[rs, cs] = _dot(u_ref[0, rs, :], h2t_ref[:, cs])

        blocks = [(lt, jg, ig) for lt in range(t // LANES) for jg in range(nk // 16) for ig in range(ni // gi)]
        units = ([functools.partial(out_unit, m) for m in range(n_out)]
                 + [functools.partial(pre_unit, q) for q in range(n_pre)])
        for k, unit in enumerate(units):
            unit()
            for blk in blocks[k * len(blocks) // len(units):(k + 1) * len(blocks) // len(units)]:
                gate_block(a_old, w_new, r0, *blk)

    @pl.when(s % 2 == 0)
    def _():
        step(a_even, a_odd, w_even, w_odd, 0)

    @pl.when(s % 2 == 1)
    def _():
        step(a_odd, a_even, w_odd, w_even, 1)


def _experts(h2t, u, vt, layer, s1, e1, th, e2, n_rows):
    d = h2t.shape[0]
    t, ec = PEER_T, PEER_EC
    nchunk = PEER_EXPERTS // ec
    n_items = (n_rows // t) * nchunk
    item = lambda s, lag: jnp.clip(s - lag, 0, n_items - 1)
    res = lambda a: pl.BlockSpec(a.shape[:2] + (t,), lambda s: (0, 0, item(s, 1) // nchunk))
    ni = ec // PEER_NKEYS
    row = lambda a: pl.BlockSpec((PEER_HEADS, 8, t),
                                 lambda s: (0, (item(s, 1) % nchunk) * ni // 8, item(s, 1) // nchunk))
    return pl.pallas_call(
        functools.partial(_experts_kernel, n_items=n_items),
        grid=(n_items + 2,),
        in_specs=[pl.BlockSpec((d, t), lambda s: (0, item(s, 0) // nchunk)),
                  pl.BlockSpec((1, ec, d), lambda s: (layer, item(s, 0) % nchunk, 0)),
                  pl.BlockSpec((1, d, ec), lambda s: (layer, 0, item(s, 2) % nchunk)),
                  row(s1), row(e1), res(th), res(e2)],
        out_specs=pl.BlockSpec((d, t), lambda s: (0, item(s, 2) // nchunk)),
        out_shape=jax.ShapeDtypeStruct((d, n_rows), F32),
        scratch_shapes=[pltpu.VMEM((ec, t), F32), pltpu.VMEM((ec, t), F32),
                        pltpu.VMEM((ec, t), BF16), pltpu.VMEM((ec, t), BF16)],
        compiler_params=_cp("arbitrary"),
        name="peer_experts",
    )(h2t, u, vt, s1, e1, th, e2)


def _resid_kernel(x_ref, pt_ref, mod_ref, o_ref):
    o_ref[...] = x_ref[...] + mod_ref[0, 5:6, :] * pt_ref[...].T


def _peer_residual(x, pt, mod, n_rows, mod_idx):
    d = x.shape[1]
    return pl.pallas_call(
        _resid_kernel,
        grid=(n_rows // TB,),
        in_specs=[pl.BlockSpec((TB, d), lambda i: (i, 0)),
                  pl.BlockSpec((d, TB), lambda i: (0, i)),
                  pl.BlockSpec((1, 6, d), lambda i: (mod_idx(i), 0, 0))],
        out_specs=pl.BlockSpec((TB, d), lambda i: (i, 0)),
        out_shape=jax.ShapeDtypeStruct((n_rows, d), F32),
        compiler_params=_cp("parallel"),
        name="peer_residual",
    )(x, pt, mod)


def _rope_table(n_tok, dim):
    m = dim // 2
    inv = ROPE_BASE ** (-jnp.arange(0, m, 2, dtype=F32) / m)
    t = jnp.arange(n_tok)
    row = (t // GRID_W).astype(F32)
    col = (t % GRID_W).astype(F32)
    ar, ac = row[:, None] * inv, col[:, None] * inv
    ang = jnp.concatenate([ar, ar, ac, ac], axis=-1)
    sign = np.where((np.arange(dim) % (dim // 2)) < dim // 4, -1.0, 1.0).astype(np.float32)
    reps = LANES // dim
    cos = jnp.tile(jnp.cos(ang), (1, reps))
    sin = jnp.tile(jnp.sin(ang) * sign, (1, reps))
    return (jnp.concatenate([cos, jnp.ones((TB, LANES), F32)], axis=0),
            jnp.concatenate([sin, jnp.zeros((TB, LANES), F32)], axis=0))


def _na_bias_table(rpb, variants):
    cols = np.arange(GRID_W)
    start = np.clip(cols - NA_KW // 2, 0, GRID_W - NA_KW)
    valid_c = (cols[None, :] >= start[:, None]) & (cols[None, :] < start[:, None] + NA_KW)
    dc = np.clip(cols[None, :] - cols[:, None] + (NA_KW - 1), 0, 2 * NA_KW - 2)
    g = rpb.astype(F32)[:, :, dc]
    tabs = []
    for valid_r, dr in variants:
        ok = valid_r[None, :, :, None, None] & valid_c[None, None, None]
        t = jnp.where(ok, g[:, dr], NEG).transpose(0, 1, 3, 2, 4)
        tabs.append(t.reshape(NA_HEADS, NA_GROUP * GRID_W, NA_UNION * GRID_W))
    return jnp.stack(tabs, axis=1)


def _retention_tables(decay_logit):
    C = RT_CHUNK
    log_g = jax.nn.log_sigmoid(decay_logit.astype(F32))
    n = jnp.arange(C, dtype=F32)
    diff = n[:, None] - n[None, :]
    dm = jnp.where(diff >= 0, jnp.exp(log_g[:, :, None, None] * jnp.maximum(diff, 0.0)), 0.0)
    dmat = jnp.stack([dm[0], jnp.swapaxes(dm[1], -1, -2)])
    xi = jnp.exp(log_g[:, :, None] * (n + 1.0))
    zeta = jnp.exp(log_g[:, :, None] * (C - 1.0 - n))
    xi = jnp.stack([xi[0], xi[1, :, ::-1]])
    zeta = jnp.stack([zeta[0], zeta[1, :, ::-1]])
    bc = lambda a: jnp.broadcast_to(a[..., None], a.shape + (LANES,))
    gch = jnp.broadcast_to(jnp.exp(log_g * C)[:, :, None, None], (2, RT_HEADS, 1, LANES))
    return dmat, bc(xi), bc(zeta), gch


def kernel(x, c, ctx, c_ctx, w_mod, b_mod, norm1_g, norm2_g, w_in, diff_lambda, diff_qk_g, diff_sub_g,
           ret_decay, ret_norm_g, na_qk_g, na_rpb, mla_q_norm_g, mla_kv_norm_g, w_uq, w_ukv, mla_qk_g,
           w_branch, w_gate, b_gate, w_o, peer_w_query, peer_sub_keys, peer_u, peer_v):
    B, S, D = x.shape
    depth = w_mod.shape[0]
    assert D == D_MODEL and ctx.shape[1] == CTX_LEN and S % TB_IN == 0 and (B * CTX_LEN) % TB_IN == 0
    assert S % (GRID_W * NA_GROUP) == 0 and S // GRID_W >= NA_UNION
    NL, NC = B * S, B * CTX_LEN
    NT = NL + NC

    mod_idx = lambda i: jnp.minimum(i * TB // S, B)
    mod_idx_in = lambda i: jnp.minimum(i * TB_IN // S, B)
    nlb, spb = NL // TB, S // TB
    rope_idx = lambda i: jnp.where(i < nlb, i % spb, spb)

    xa, xb, xb_row0 = x.reshape(NL, D), ctx.reshape(NC, D), 0
    c_all = jnp.zeros((16, D), F32).at[:B].set(c).at[B].set(c_ctx)
    cos_a, sin_a = _rope_table(S, DA_HEAD)
    cos_r, sin_r = _rope_table(S, RT_DK)
    na_plans, na_variants = _na_groups(S // GRID_W)
    tile2 = lambda g: jnp.tile(g.astype(F32), (1, 2))
    pad_qk = lambda g: jnp.pad(g.astype(F32), (0, 2 * LANES - g.shape[0])).reshape(1, 2 * LANES)

    w_in_bf = jnp.pad(w_in, ((0, 0), (0, 0), (0, IN_PAD - IN_WIDTH))).astype(BF16)
    wg_bf, wb_bf, wo_bf = w_gate.astype(BF16), w_branch.astype(BF16), w_o.astype(BF16)
    wqt = jnp.swapaxes(peer_w_query, 1, 2).astype(BF16)
    sk = peer_sub_keys.reshape(depth, PEER_HEADS * 2, PEER_NKEYS, PEER_DQ // 2).astype(BF16)
    u_bf = peer_u.astype(BF16)
    vt_bf = jnp.swapaxes(peer_v, 1, 2).astype(BF16)

    for l in range(depth):
        last = l == depth - 1
        lam_init = 0.8 - 0.6 * math.exp(-0.3 * l)
        n_tok = NL if last else NT

        wuq = w_uq[l].reshape(MLA_Q_LORA, MLA_HEADS, MLA_NOPE + MLA_ROPE)
        wuq = jnp.pad(wuq, ((0, 0), (0, 0), (0, 2 * LANES - MLA_NOPE - MLA_ROPE)))
        wuq = wuq.reshape(MLA_Q_LORA, MLA_HEADS * 2 * LANES).astype(BF16)
        wukv = w_ukv[l].reshape(MLA_KV_LORA, MLA_HEADS, MLA_NOPE + MLA_V)
        wuk = wukv[:, :, :MLA_NOPE].reshape(MLA_KV_LORA, MLA_HEADS * MLA_NOPE).astype(BF16)
        wuv = wukv[:, :, MLA_NOPE:].reshape(MLA_KV_LORA, MLA_HEADS * MLA_V).astype(BF16)
        gains = [tile2(diff_qk_g[l]), tile2(na_qk_g[l]), mla_q_norm_g[l].reshape(1, -1),
                 mla_kv_norm_g[l].reshape(1, -1), pad_qk(mla_qk_g[l, 0]), pad_qk(mla_qk_g[l, 1])]

        mod = _modulation(c_all, w_mod, b_mod, l).reshape(16, 6, D)
        h1, p = _in_proj(xa, xb, xb_row0, NL, mod, norm1_g[l].reshape(1, D), w_in_bf, l, NT, mod_idx_in)
        qa, ka, qr, kr, qn, kn, qm, km, vm = _prep(
            p, (cos_a, sin_a, cos_r, sin_r), gains, (wuq, wuk, wuv), NT, rope_idx)

        sub_g = diff_sub_g[l].reshape(1, -1)
        mla_scale = (MLA_NOPE + MLA_ROPE) ** -0.5
        ret_tabs = _retention_tables(ret_decay[l])
        common = dict(batch=B, seq=S)
        ya = _attention(qa, ka, p, (diff_lambda[l], sub_g), mode="diff", heads=DA_HEADS, dq=LANES,
                        v_col0=COL_DA_V, latent=True, lam_init=lam_init, name="diff_attn", **common)
        yb, yb_c = _retention(qr, kr, p, ret_tabs, ret_norm_g[l].reshape(1, -1), ctx_out=not last, **common)
        yc = _na_latent(qn, kn, p, _na_bias_table(na_rpb[l], na_variants), na_plans, **common)
        yd = _attention(qm, km, vm, (), mode="plain", heads=MLA_HEADS, dq=2 * LANES, v_col0=0,
                        latent=True, scale=mla_scale, name="latent_attn", **common)
        ys = [ya, yb, yc, yd]
        ys_ctx = ys
        if not last:
            ya_c = _attention(qa, ka, p, (diff_lambda[l], sub_g), mode="diff", heads=DA_HEADS, dq=LANES,
                              v_col0=COL_DA_V, latent=False, lam_init=lam_init, name="diff_attn_ctx", **common)
            yc_c = _attention(qn, kn, p, (), mode="pair", heads=NA_HEADS // 2, dq=LANES, v_col0=COL_NA_V,
                              latent=False, name="ctx_attn", **common)
            yd_c = _attention(qm, km, vm, (), mode="plain", heads=MLA_HEADS, dq=2 * LANES, v_col0=0,
                              latent=False, scale=mla_scale, name="latent_attn_ctx", **common)
            ys_ctx = [ya_c, yb_c, yc_c, yd_c]

        m = _merge1(h1, ys, ys_ctx, wg_bf, wb_bf, b_gate, l, n_tok)
        x_mid, h2t = _merge2(m, wo_bf, l, xa, xb, xb_row0, NL, mod, norm2_g[l].reshape(1, D), n_tok, mod_idx)
        s1, th, e1, e2 = _route(h2t, wqt, sk, l, n_tok)
        pt = _experts(h2t, u_bf, vt_bf, l, s1, e1, th, e2, n_tok)
        xa = xb = _peer_residual(x_mid, pt, mod, n_tok, mod_idx)
        xb_row0 = NL

    return xa.reshape(B, S, D)
```

```python
import functools
import math

import jax
import jax.numpy as jnp
import numpy as np
from jax import lax
from jax.experimental import pallas as pl
from jax.experimental.pallas import tpu as pltpu

F32 = jnp.float32
BF16 = jnp.bfloat16

D_MODEL = 2048
CTX_LEN = 256
GRID_W = 64
EPS = 1e-6
ROPE_BASE = 10000.0
N_BRANCH = 4
BRANCH_W = 512
DA_HEADS, DA_HEAD = 4, 64
RT_HEADS, RT_DK, RT_CHUNK = 4, 128, 128
NA_HEADS, NA_HEAD, NA_KH, NA_KW = 8, 64, 8, 16
MLA_HEADS, MLA_Q_LORA, MLA_KV_LORA, MLA_NOPE, MLA_ROPE, MLA_V = 4, 512, 256, 128, 64, 128
PEER_HEADS, PEER_NKEYS, PEER_DQ, PEER_TOPK = 8, 128, 256, 16
PEER_EXPERTS = PEER_NKEYS * PEER_NKEYS

LANES = 128
IN_WIDTH = 5952
IN_PAD = 6144
COL_DA_V = 1024 // LANES
COL_RT_V = 2560 // LANES
COL_RT_G = 3072 // LANES
COL_NA_V = 4608 // LANES
OFF_RT_Q, OFF_RT_K = 1536, 2048
OFF_NA_Q, OFF_NA_K = 3584, 4096
OFF_MLA_CQ, OFF_MLA_CKV, OFF_MLA_KR = 5120, 5632, 5888

TB = 512
TB_IN = 1024
TQ = 1024
TQ_SPLIT = 4
PEER_T = 512
PEER_EC = 512
NA_GROUP = 4
NA_UNION = NA_GROUP + NA_KH
NEG = -1e30
VMEM_LIMIT = 56 * 1024 * 1024


def _cp(*sem):
    return pltpu.CompilerParams(dimension_semantics=sem, vmem_limit_bytes=VMEM_LIMIT)


def _dot(a, b):
    return jnp.dot(a, b, preferred_element_type=F32)


def _dot_nt(a, b):
    return lax.dot_general(a, b, (((1,), (1,)), ((), ())), preferred_element_type=F32)


def _sigmoid(z):
    return 1.0 / (1.0 + jnp.exp(-z))


def _rms_rows(x, g):
    return x * lax.rsqrt(jnp.mean(x * x, axis=-1, keepdims=True) + EPS) * g


def _mod_kernel(c_ref, w_ref, b_ref, o_ref):
    c = c_ref[...]
    a = (c * _sigmoid(c)).astype(BF16)
    o_ref[...] = _dot(a, w_ref[0].astype(BF16)) + b_ref[0]


def _modulation(c_all, w_mod, b_mod, layer):
    rows, d = c_all.shape
    depth, _, n = w_mod.shape
    cb = 1024
    return pl.pallas_call(
        _mod_kernel,
        grid=(n // cb,),
        in_specs=[pl.BlockSpec((rows, d), lambda j: (0, 0)),
                  pl.BlockSpec((1, d, cb), lambda j: (layer, 0, j)),
                  pl.BlockSpec((1, 1, cb), lambda j: (layer, 0, j))],
        out_specs=pl.BlockSpec((rows, cb), lambda j: (0, j)),
        out_shape=jax.ShapeDtypeStruct((rows, n), F32),
        compiler_params=_cp("parallel"),
        name="adaln_mod",
    )(c_all, w_mod, b_mod.reshape(depth, 1, n))


def _split_rows(n_a, b_off):
    return (lambda i, *_: (jnp.minimum(i, n_a - 1), 0),
            lambda i, *_: (jnp.maximum(i - n_a, 0) + b_off, 0))


def _inproj_kernel(xa_ref, xb_ref, mod_ref, g_ref, w_ref, h_ref, p_ref, *, n_a):
    def prologue(x_ref):
        y = _rms_rows(x_ref[...], g_ref[...])
        h_ref[...] = (y * (1.0 + mod_ref[0, 1:2, :]) + mod_ref[0, 0:1, :]).astype(BF16)

    first = pl.program_id(1) == 0
    pl.when(first & (pl.program_id(0) < n_a))(functools.partial(prologue, xa_ref))
    pl.when(first & (pl.program_id(0) >= n_a))(functools.partial(prologue, xb_ref))
    p_ref[...] = _dot(h_ref[...], w_ref[0]).astype(BF16)


def _in_proj(xa, xb, xb_row0, n_lat, mod, g, w_bf, layer, n_rows, mod_idx):
    d = xa.shape[1]
    cb = 512
    amap, bmap = _split_rows(n_lat // TB_IN, xb_row0 // TB_IN)
    return pl.pallas_call(
        functools.partial(_inproj_kernel, n_a=n_lat // TB_IN),
        grid=(n_rows // TB_IN, IN_PAD // cb),
        in_specs=[pl.BlockSpec((TB_IN, d), amap),
                  pl.BlockSpec((TB_IN, d), bmap),
                  pl.BlockSpec((1, 6, d), lambda i, j: (mod_idx(i), 0, 0)),
                  pl.BlockSpec((1, d), lambda i, j: (0, 0)),
                  pl.BlockSpec((1, d, cb), lambda i, j: (layer, 0, j))],
        out_specs=[pl.BlockSpec((TB_IN, d), lambda i, j: (i, 0)),
                   pl.BlockSpec((TB_IN, cb), lambda i, j: (i, j))],
        out_shape=[jax.ShapeDtypeStruct((n_rows, d), BF16),
                   jax.ShapeDtypeStruct((n_rows, IN_PAD), BF16)],
        compiler_params=_cp("parallel", "arbitrary"),
        name="norm1_in_proj",
    )(xa, xb, mod, g, w_bf)


def _rot_partner(x, q):
    lane = lax.broadcasted_iota(jnp.int32, x.shape, 1)
    first = (lane & (2 * q - 1)) < q
    return jnp.where(first, pltpu.roll(x, LANES - q, 1), pltpu.roll(x, q, 1))


def _rope(x, cos, sin_signed, q):
    return x * cos + _rot_partner(x, q) * sin_signed


def _group_sumsq(x, gsz):
    x2 = x * x
    hi = x2.astype(BF16)
    lo = (x2 - hi.astype(F32)).astype(BF16)
    r = lax.broadcasted_iota(jnp.int32, (LANES, LANES), 0) // gsz
    c = lax.broadcasted_iota(jnp.int32, (LANES, LANES), 1) // gsz
    ones = jnp.where(r == c, 1.0, 0.0).astype(BF16)
    return _dot(hi, ones) + _dot(lo, ones)


def _prep_kernel(p_ref, cosa_ref, sina_ref, cosr_ref, sinr_ref, gda_ref, gna_ref, gq_ref, gkv_ref,
                 gmq_ref, gmk_ref, wuq_ref, wuk_ref, wuv_ref,
                 qa_ref, ka_ref, qr_ref, kr_ref, qn_ref, kn_ref, qm_ref, km_ref, vm_ref):
    cosa, sina = cosa_ref[...], sina_ref[...]
    cosr, sinr = cosr_ref[...], sinr_ref[...]
    for t in range(4):
        sl = slice(t * LANES, (t + 1) * LANES)
        for off, gi, oref, scale in ((0, 0, qa_ref, DA_HEAD ** -0.5), (512, 1, ka_ref, 1.0)):
            x = p_ref[:, off + t * LANES: off + (t + 1) * LANES].astype(F32)
            y = x * lax.rsqrt(_group_sumsq(x, DA_HEAD) * (1.0 / DA_HEAD) + EPS) * gda_ref[gi:gi + 1, :]
            oref[:, sl] = (_rope(y, cosa, sina, DA_HEAD // 4) * scale).astype(BF16)
        x = p_ref[:, OFF_RT_Q + t * LANES: OFF_RT_Q + (t + 1) * LANES].astype(F32)
        qr_ref[:, sl] = _rope(x, cosr, sinr, RT_DK // 4).astype(BF16)
        x = p_ref[:, OFF_RT_K + t * LANES: OFF_RT_K + (t + 1) * LANES].astype(F32) * (RT_DK ** -0.5)
        kr_ref[:, sl] = _rope(x, cosr, sinr, RT_DK // 4).astype(BF16)
        for off, gi, oref, scale in ((OFF_NA_Q, 0, qn_ref, NA_HEAD ** -0.5), (OFF_NA_K, 1, kn_ref, 1.0)):
            x = p_ref[:, off + t * LANES: off + (t + 1) * LANES].astype(F32)
            y = x * lax.rsqrt(_group_sumsq(x, NA_HEAD) * (1.0 / NA_HEAD) + EPS) * gna_ref[gi:gi + 1, :]
            oref[:, sl] = (y * scale).astype(BF16)
    cq = p_ref[:, OFF_MLA_CQ:OFF_MLA_CQ + MLA_Q_LORA].astype(F32)
    q = _dot(_rms_rows(cq, gq_ref[...]).astype(BF16), wuq_ref[...])
    ckv = p_ref[:, OFF_MLA_CKV:OFF_MLA_CKV + MLA_KV_LORA].astype(F32)
    ckv_n = _rms_rows(ckv, gkv_ref[...]).astype(BF16)
    k_nope = _dot(ckv_n, wuk_ref[...])
    vm_ref[...] = _dot(ckv_n, wuv_ref[...]).astype(BF16)
    kr = p_ref[:, OFF_MLA_KR:OFF_MLA_KR + LANES].astype(F32)
    kr_ss = jnp.sum(kr * kr, axis=-1, keepdims=True)
    inv_n = 1.0 / (MLA_NOPE + MLA_ROPE)
    for h in range(MLA_HEADS):
        a, b = h * 2 * LANES, h * 2 * LANES + LANES
        q0, q1 = q[:, a:b], q[:, b:b + LANES]
        r = lax.rsqrt((jnp.sum(q0 * q0, axis=-1, keepdims=True)
                       + jnp.sum(q1 * q1, axis=-1, keepdims=True)) * inv_n + EPS)
        qm_ref[:, a:b] = (q0 * r * gmq_ref[:, 0:LANES]).astype(BF16)
        qm_ref[:, b:b + LANES] = _rope(q1 * r * gmq_ref[:, LANES:2 * LANES], cosa, sina,
                                       MLA_ROPE // 4).astype(BF16)
        k0 = k_nope[:, h * LANES:(h + 1) * LANES]
        r = lax.rsqrt((jnp.sum(k0 * k0, axis=-1, keepdims=True) + kr_ss) * inv_n + EPS)
        km_ref[:, a:b] = (k0 * r * gmk_ref[:, 0:LANES]).astype(BF16)
        km_ref[:, b:b + LANES] = _rope(kr * r * gmk_ref[:, LANES:2 * LANES], cosa, sina,
                                       MLA_ROPE // 4).astype(BF16)


def _prep(p, tabs, gains, weights, n_rows, rope_idx):
    row = lambda w: pl.BlockSpec((TB, w), lambda i: (i, 0))
    tab = pl.BlockSpec((TB, LANES), lambda i: (rope_idx(i), 0))
    full = lambda a: pl.BlockSpec(a.shape, lambda i: (0,) * a.ndim)
    outs = [512] * 6 + [1024, 1024, 512]
    return pl.pallas_call(
        _prep_kernel,
        grid=(n_rows // TB,),
        in_specs=[row(IN_PAD)] + [tab] * 4 + [full(a) for a in gains] + [full(a) for a in weights],
        out_specs=[row(w) for w in outs],
        out_shape=[jax.ShapeDtypeStruct((n_rows, w), BF16) for w in outs],
        compiler_params=_cp("parallel"),
        name="mixer_prep",
    )(p, *tabs, *gains, *weights)


def _softmax_parts(scores):
    m = functools.reduce(jnp.maximum, [jnp.max(s, axis=-1, keepdims=True) for s in scores])
    es = [jnp.exp(s - m) for s in scores]
    z = functools.reduce(jnp.add, [jnp.sum(e, axis=-1, keepdims=True) for e in es])
    return es, z


def _attn_kernel(*refs, mode, nseg, scale, lam_init, nsplit):
    q_ref = refs[0]
    ks = [refs[1 + 2 * s] for s in range(nseg)]
    vs = [refs[2 + 2 * s] for s in range(nseg)]
    o_ref = refs[-1]
    rows = q_ref.shape[0] // nsplit

    def scores(g):
        q = q_ref[g * rows:(g + 1) * rows, :]
        if mode == "plain":
            return [[_dot_nt(q, k[...]) * scale for k in ks]]
        lane = lax.broadcasted_iota(jnp.int32, q.shape, 1)
        halves = [jnp.where(lane < LANES // 2, q, jnp.zeros_like(q)),
                  jnp.where(lane >= LANES // 2, q, jnp.zeros_like(q))]
        return [[_dot_nt(qh, k[...]) for k in ks] for qh in halves]

    def weighted(es):
        return functools.reduce(jnp.add, [_dot(e.astype(BF16), v[...]) for e, v in zip(es, vs)])

    if mode == "diff":
        lam_ref, subg_ref = refs[1 + 2 * nseg], refs[2 + 2 * nseg]
        lv = lam_ref[...]
        lam = (jnp.exp(jnp.sum(lv[0:1] * lv[1:2], axis=-1, keepdims=True))
               - jnp.exp(jnp.sum(lv[2:3] * lv[3:4], axis=-1, keepdims=True)) + lam_init)

    def finish(g, sc):
        parts = [_softmax_parts(s) for s in sc]
        if mode == "plain":
            (es, z), = parts
            y = weighted(es) / z
        elif mode == "pair":
            outs = [weighted(es) / z for es, z in parts]
            lo = lax.broadcasted_iota(jnp.int32, outs[0].shape, 1) < LANES // 2
            y = jnp.where(lo, outs[0], outs[1])
        else:
            (e0, z0), (e1, z1) = parts
            w0, w1 = 1.0 / z0, lam / z1
            y = weighted([a * w0 - b * w1 for a, b in zip(e0, e1)])
            y = _rms_rows(y, subg_ref[...]) * (1.0 - lam_init)
        o_ref[g * rows:(g + 1) * rows, :] = y.astype(o_ref.dtype)

    sc = [scores(g) for g in range(nsplit)]
    for g in range(nsplit):
        finish(g, sc[g])


def _attention(q_arr, k_arr, v_arr, extra, *, mode, heads, dq, v_col0, batch, seq, latent, scale=1.0,
               lam_init=0.0, name):
    nl = batch * seq
    cblk = nl // CTX_LEN
    if latent:
        nq = seq // TQ
        grid = (batch, heads, nq)
        q_spec = pl.BlockSpec((TQ, dq), lambda b, h, j: (b * nq + j, h))
        segs = [(CTX_LEN, lambda b, h, j: (cblk + b, h), lambda b, h, j: (cblk + b, v_col0 + h)),
                (seq, lambda b, h, j: (b, h), lambda b, h, j: (b, v_col0 + h))]
        o_spec = pl.BlockSpec((TQ, LANES), lambda b, h, j: (b * nq + j, h))
        rows = nl
    else:
        grid = (batch, heads, 1)
        q_spec = pl.BlockSpec((CTX_LEN, dq), lambda b, h, j: (cblk + b, h))
        segs = [(CTX_LEN, lambda b, h, j: (cblk + b, h), lambda b, h, j: (cblk + b, v_col0 + h))]
        o_spec = pl.BlockSpec((CTX_LEN, LANES), lambda b, h, j: (b, h))
        rows = batch * CTX_LEN
    in_specs, args = [q_spec], [q_arr]
    for n, kmap, vmap in segs:
        in_specs += [pl.BlockSpec((n, dq), kmap), pl.BlockSpec((n, LANES), vmap)]
        args += [k_arr, v_arr]
    for a in extra:
        in_specs.append(pl.BlockSpec(a.shape, lambda b, h, j: (0, 0)))
        args.append(a)
    return pl.pallas_call(
        functools.partial(_attn_kernel, mode=mode, nseg=len(segs), scale=scale, lam_init=lam_init,
                          nsplit=TQ_SPLIT if latent else 1),
        grid=grid, in_specs=in_specs, out_specs=o_spec,
        out_shape=jax.ShapeDtypeStruct((rows, heads * LANES), BF16),
        compiler_params=_cp("parallel", "parallel", "arbitrary"),
        name=name,
    )(*args)


def _ret_kernel(ql_ref, qc_ref, kl_ref, kc_ref, vl_ref, vc_ref, gl_ref, gc_ref, dmat_ref, xi_ref, zeta_ref,
                gch_ref, ng_ref, yl_ref, yc_ref, accl_ref, accc_ref, *, n_lat, n_ctx, ctx_out):
    C = RT_CHUNK

    def chunk(q, k, v, R, d):
        s = _dot_nt(q, k) * dmat_ref[d, 0]
        inner = _dot(s.astype(BF16), v)
        cross = _dot(q, R.astype(BF16)) * xi_ref[d, 0]
        kz = (k.astype(F32) * zeta_ref[d, 0]).T.astype(BF16)
        return inner + cross, gch_ref[d, 0] * R + _dot(kz, v)

    for d in range(2):
        R = jnp.zeros((RT_DK, RT_DK), F32)
        for j in range(n_ctx):
            c = j if d == 0 else n_ctx - 1 - j
            sl = pl.ds(c * C, C)
            o, R = chunk(qc_ref[sl, :], kc_ref[sl, :], vc_ref[sl, :], R, d)
            if ctx_out:
                if d == 0:
                    accc_ref[sl, :] = o
                else:
                    accc_ref[sl, :] += o

        for j in range(n_lat):
            c = j if d == 0 else n_lat - 1 - j
            sl = pl.ds(c * C, C)
            o, R = chunk(ql_ref[sl, :], kl_ref[sl, :], vl_ref[sl, :], R, d)
            if d == 0:
                accl_ref[sl, :] = o
            else:
                accl_ref[sl, :] += o

    def post(acc_ref, g_ref, y_ref):
        g = g_ref[...].astype(F32)
        y_ref[...] = (_rms_rows(acc_ref[...], ng_ref[...]) * (g * _sigmoid(g))).astype(y_ref.dtype)

    post(accl_ref, gl_ref, yl_ref)
    if ctx_out:
        post(accc_ref, gc_ref, yc_ref)
    else:
        yc_ref[...] = jnp.zeros_like(yc_ref)


def _retention(qr, kr, p, tabs, norm_g, *, batch, seq, ctx_out):
    nl = batch * seq
    cblk = nl // CTX_LEN
    lat = lambda c0: pl.BlockSpec((seq, LANES), lambda b, h: (b, c0 + h))
    ctx = lambda c0: pl.BlockSpec((CTX_LEN, LANES), lambda b, h: (cblk + b, c0 + h))
    tab = lambda a: pl.BlockSpec((2, 1) + a.shape[2:], lambda b, h: (0, h, 0, 0))
    return pl.pallas_call(
        functools.partial(_ret_kernel, n_lat=seq // RT_CHUNK, n_ctx=CTX_LEN // RT_CHUNK, ctx_out=ctx_out),
        grid=(batch, RT_HEADS),
        in_specs=[lat(0), ctx(0), lat(0), ctx(0), lat(COL_RT_V), ctx(COL_RT_V), lat(COL_RT_G), ctx(COL_RT_G)]
        + [tab(a) for a in tabs] + [pl.BlockSpec((1, LANES), lambda b, h: (0, 0))],
        out_specs=[pl.BlockSpec((seq, LANES), lambda b, h: (b, h)),
                   pl.BlockSpec((CTX_LEN, LANES), lambda b, h: (b, h))],
        out_shape=[jax.ShapeDtypeStruct((nl, RT_HEADS * LANES), BF16),
                   jax.ShapeDtypeStruct((batch * CTX_LEN, RT_HEADS * LANES), BF16)],
        scratch_shapes=[pltpu.VMEM((seq, LANES), F32), pltpu.VMEM((CTX_LEN, LANES), F32)],
        compiler_params=_cp("parallel", "parallel"),
        name="retention",
    )(qr, qr, kr, kr, p, p, p, p, *tabs, norm_g)


def _na_groups(rows):
    plans, variants = [], []
    for gi in range(rows // NA_GROUP):
        r = gi * NA_GROUP + np.arange(NA_GROUP)
        u0 = int(np.clip(gi * NA_GROUP - NA_KH // 2, 0, rows - NA_UNION))
        rs = np.clip(r - NA_KH // 2, 0, rows - NA_KH)
        krow = u0 + np.arange(NA_UNION)
        valid = (krow[None, :] >= rs[:, None]) & (krow[None, :] < rs[:, None] + NA_KH)
        dr = np.where(valid, krow[None, :] - r[:, None] + (NA_KH - 1), 0)
        key = (valid.tobytes(), dr.tobytes())
        for v, (k, _, _) in enumerate(variants):
            if k == key:
                break
        else:
            v = len(variants)
            variants.append((key, valid, dr))
        plans.append((u0, v))
    return tuple(plans), [(valid, dr) for _, valid, dr in variants]


def _na_kernel(q_ref, kl_ref, kc_ref, vl_ref, vc_ref, bias_ref, o_ref, *, plans):
    gq, gk = NA_GROUP * GRID_W, NA_UNION * GRID_W
    kc, vc = kc_ref[...], vc_ref[...]

    def scores(gi):
        u0, var = plans[gi]
        q = q_ref[gi * gq:(gi + 1) * gq, :]
        kw = kl_ref[u0 * GRID_W:u0 * GRID_W + gk, :]
        lane = lax.broadcasted_iota(jnp.int32, q.shape, 1)
        out = []
        for hh in range(2):
            keep = (lane < LANES // 2) if hh == 0 else (lane >= LANES // 2)
            qh = jnp.where(keep, q, jnp.zeros_like(q))
            out.append([_dot_nt(qh, kw) + bias_ref[hh, var], _dot_nt(qh, kc)])
        return out

    def finish(gi, sc):
        u0, _ = plans[gi]
        vw = vl_ref[u0 * GRID_W:u0 * GRID_W + gk, :]
        outs = []
        for s in sc:
            (el, ec), z = _softmax_parts(s)
            outs.append((_dot(el.astype(BF16), vw) + _dot(ec.astype(BF16), vc)) / z)
        lo = lax.broadcasted_iota(jnp.int32, outs[0].shape, 1) < LANES // 2
        o_ref[gi * gq:(gi + 1) * gq, :] = jnp.where(lo, outs[0], outs[1]).astype(o_ref.dtype)

    pending = scores(0)
    for gi in range(len(plans)):
        nxt = scores(gi + 1) if gi + 1 < len(plans) else None
        finish(gi, pending)
        pending = nxt


def _na_latent(qn, kn, p, bias, plans, *, batch, seq):
    nl = batch * seq
    cblk = nl // CTX_LEN
    hp = NA_HEADS // 2
    return pl.pallas_call(
        functools.partial(_na_kernel, plans=plans),
        grid=(batch, hp),
        in_specs=[pl.BlockSpec((seq, LANES), lambda b, h: (b, h)),
                  pl.BlockSpec((seq, LANES), lambda b, h: (b, h)),
                  pl.BlockSpec((CTX_LEN, LANES), lambda b, h: (cblk + b, h)),
                  pl.BlockSpec((seq, LANES), lambda b, h: (b, COL_NA_V + h)),
                  pl.BlockSpec((CTX_LEN, LANES), lambda b, h: (cblk + b, COL_NA_V + h)),
                  pl.BlockSpec((2,) + bias.shape[1:], lambda b, h: (h, 0, 0, 0))],
        out_specs=pl.BlockSpec((seq, LANES), lambda b, h: (b, h)),
        out_shape=jax.ShapeDtypeStruct((nl, hp * LANES), BF16),
        compiler_params=_cp("parallel", "parallel"),
        name="neighbourhood_attn",
    )(qn, kn, kn, p, p, bias)


def _merge1_kernel(h_ref, *refs, n_a):
    lat, ctx = refs[:N_BRANCH], refs[N_BRANCH:2 * N_BRANCH]
    wg_ref, wb_ref, bg_ref, m_ref = refs[2 * N_BRANCH:]

    def body(y_refs):
        h = h_ref[...]
        acc = None
        for n, y_ref in enumerate(y_refs):
            gate = _sigmoid(_dot(h, wg_ref[0, n]) + bg_ref[0, n:n + 1, :])
            term = gate * _dot(y_ref[...], wb_ref[0, n])
            acc = term if acc is None else acc + term
        m_ref[...] = acc.astype(BF16)

    pl.when(pl.program_id(1) < n_a)(functools.partial(body, lat))
    pl.when(pl.program_id(1) >= n_a)(functools.partial(body, ctx))


def _merge1(h, ys_lat, ys_ctx, wg, wb, bg, layer, n_rows):
    d = h.shape[1]
    cb = 512
    n_a = ys_lat[0].shape[0] // TB
    lat_map = lambda j, i: (jnp.minimum(i, n_a - 1), 0)
    ctx_map = lambda j, i: (jnp.maximum(i - n_a, 0), 0)
    return pl.pallas_call(
        functools.partial(_merge1_kernel, n_a=n_a),
        grid=(d // cb, n_rows // TB),
        in_specs=[pl.BlockSpec((TB, d), lambda j, i: (i, 0))]
        + [pl.BlockSpec((TB, BRANCH_W), lat_map)] * N_BRANCH
        + [pl.BlockSpec((TB, BRANCH_W), ctx_map)] * N_BRANCH
        + [pl.BlockSpec((1, N_BRANCH, d, cb), lambda j, i: (layer, 0, 0, j)),
           pl.BlockSpec((1, N_BRANCH, BRANCH_W, cb), lambda j, i: (layer, 0, 0, j)),
           pl.BlockSpec((1, N_BRANCH, cb), lambda j, i: (layer, 0, j))],
        out_specs=pl.BlockSpec((TB, cb), lambda j, i: (i, j)),
        out_shape=jax.ShapeDtypeStruct((n_rows, d), BF16),
        compiler_params=_cp("parallel", "parallel"),
        name="gated_branch_sum",
    )(h, *ys_lat, *ys_ctx, wg, wb, bg)


def _merge2_kernel(m_ref, wo_ref, xa_ref, xb_ref, mod_ref, g_ref, xo_ref, h2t_ref, *, n_a):
    def body(x_ref):
        x = x_ref[...] + mod_ref[0, 2:3, :] * _dot(m_ref[...], wo_ref[0])
        xo_ref[...] = x
        h2 = _rms_rows(x, g_ref[...]) * (1.0 + mod_ref[0, 4:5, :]) + mod_ref[0, 3:4, :]
        h2t_ref[...] = h2.T.astype(BF16)

    pl.when(pl.program_id(0) < n_a)(functools.partial(body, xa_ref))
    pl.when(pl.program_id(0) >= n_a)(functools.partial(body, xb_ref))


def _merge2(m, wo, layer, xa, xb, xb_row0, n_lat, mod, g, n_rows, mod_idx):
    d = xa.shape[1]
    amap, bmap = _split_rows(n_lat // TB, xb_row0 // TB)
    return pl.pallas_call(
        functools.partial(_merge2_kernel, n_a=n_lat // TB),
        grid=(n_rows // TB,),
        in_specs=[pl.BlockSpec((TB, d), lambda i: (i, 0)),
                  pl.BlockSpec((1, d, d), lambda i: (layer, 0, 0)),
                  pl.BlockSpec((TB, d), amap),
                  pl.BlockSpec((TB, d), bmap),
                  pl.BlockSpec((1, 6, d), lambda i: (mod_idx(i), 0, 0)),
                  pl.BlockSpec((1, d), lambda i: (0, 0))],
        out_specs=[pl.BlockSpec((TB, d), lambda i: (i, 0)),
                   pl.BlockSpec((d, TB), lambda i: (0, i))],
        out_shape=[jax.ShapeDtypeStruct((n_rows, d), F32),
                   jax.ShapeDtypeStruct((d, n_rows), BF16)],
        compiler_params=_cp("parallel"),
        name="out_proj_residual_norm2",
    )(m, wo, xa, xb, mod, g)


def _merge_exchange(n):
    comps, p = [], 1
    while p < n:
        k = p
        while k >= 1:
            for j in range(k % p, n - k, 2 * k):
                for i in range(min(k, n - j - k)):
                    if (i + j) // (2 * p) == (i + j + k) // (2 * p):
                        comps.append((i + j, i + j + k))
            k //= 2
        p *= 2
    return comps


def _top_values(x, k):
    sub = 8
    cols = [x[v * sub:(v + 1) * sub, :] for v in range(x.shape[0] // sub)]
    for i, j in _merge_exchange(len(cols)):
        cols[i], cols[j] = jnp.maximum(cols[i], cols[j]), jnp.minimum(cols[i], cols[j])
    tops = []
    for it in range(k):
        m = jnp.max(cols[0], axis=0, keepdims=True)
        tops.append(m)
        if it == k - 1:
            break
        hit = cols[0] == m
        depth = min(len(cols), k - it)
        for d in range(depth - 1):
            cols[d] = jnp.where(hit, cols[d + 1], cols[d])
        if depth == len(cols):
            cols[depth - 1] = jnp.where(hit, -jnp.inf, cols[depth - 1])
    return tops


def _route_kernel(h2t_ref, wqt_ref, sk_ref, s1_ref, th_ref, e1_ref, e2_ref):
    qt = _dot(wqt_ref[0], h2t_ref[...]).astype(BF16)
    nk, k, half = PEER_NKEYS, PEER_TOPK, PEER_TOPK // 2
    for h in range(PEER_HEADS):
        s1 = _dot(sk_ref[0, 2 * h], qt[(2 * h) * nk:(2 * h + 1) * nk, :])
        s2 = _dot(sk_ref[0, 2 * h + 1], qt[(2 * h + 1) * nk:(2 * h + 2) * nk, :])
        t1, t2 = _top_values(s1, k), _top_values(s2, k)
        t1_hi = jnp.concatenate(t1[half:], axis=0)
        t2_lo, t2_hi = jnp.concatenate(t2[:half], axis=0), jnp.concatenate(t2[half:], axis=0)
        cand = jnp.concatenate([t1[a] + t2_lo for a in range(half)] + [t1[0] + t2_hi, t1_hi + t2[0]], axis=0)
        sel = _top_values(cand, k)
        tau = sel[-1]
        z = functools.reduce(jnp.add, [jnp.exp(c - sel[0]) for c in sel])
        th = jnp.full(s2.shape, jnp.inf, F32)
        for a in range(half):
            th = jnp.where(t1[a] + s2 >= tau, t1[a], th)
        th0 = jnp.full(tau.shape, jnp.inf, F32)
        for a in range(half, k):
            th0 = jnp.where(t1[a] + t2[0] >= tau, t1[a], th0)
        th = jnp.where(s2 == t2[0], jnp.minimum(th, th0), th)
        s1_ref[h] = s1
        th_ref[h] = th
        e1_ref[h] = jnp.exp(s1 - t1[0]) / z * 0.5
        e2_ref[h] = jnp.exp(s2 - t2[0])


def _route(h2t, wqt, sk, layer, n_rows):
    d = h2t.shape[0]
    t = 256
    big = pl.BlockSpec((PEER_HEADS, PEER_NKEYS, t), lambda i: (0, 0, i))
    return pl.pallas_call(
        _route_kernel,
        grid=(n_rows // t,),
        in_specs=[pl.BlockSpec((d, t), lambda i: (0, i)),
                  pl.BlockSpec((1,) + wqt.shape[1:], lambda i: (layer, 0, 0)),
                  pl.BlockSpec((1,) + sk.shape[1:], lambda i: (layer, 0, 0, 0))],
        out_specs=[big] * 4,
        out_shape=[jax.ShapeDtypeStruct((PEER_HEADS, PEER_NKEYS, n_rows), F32)] * 4,
        compiler_params=_cp("parallel"),
        name="peer_route",
    )(h2t, wqt, sk)


def _experts_kernel(h2t_ref, u_ref, vt_ref, s1_ref, e1_ref, th_ref, e2_ref, out_ref,
                    a_even, a_odd, w_even, w_odd, *, n_items):
    s = pl.program_id(0)
    nk = PEER_NKEYS
    ni = PEER_EC // nk
    nchunk = PEER_EXPERTS // PEER_EC

    @pl.when(s == 0)
    def _():
        a_odd[...] = jnp.zeros_like(a_odd)
        w_odd[...] = jnp.zeros_like(w_odd)

    @pl.when((s == 0) | ((s >= 2) & ((s - 2) % nchunk == 0)))
    def _():
        out_ref[...] = jnp.zeros_like(out_ref)

    gi = 4
    assert nchunk % 2 == 0 and (2 * ni) % 8 == 0

    def gate_block(a_ref, w_ref, r0, lt, jg, ig):
        ls = slice(lt * LANES, (lt + 1) * LANES)
        js = slice(jg * 16, (jg + 1) * 16)
        gs = [None] * gi
        for h in range(PEER_HEADS):
            th, e2 = th_ref[h, js, ls], e2_ref[h, js, ls]
            for ii in range(gi):
                r = slice(r0 + ig * gi + ii, r0 + ig * gi + ii + 1)
                term = jnp.where(s1_ref[h, r, ls] >= th, e2 * e1_ref[h, r, ls], 0.0)
                gs[ii] = term if gs[ii] is None else gs[ii] + term
        for ii in range(gi):
            r = ig * gi + ii
            rows = slice(r * nk + jg * 16, r * nk + (jg + 1) * 16)
            a = a_ref[rows, ls]
            act = a * (1.0 + lax.erf(a * (2.0 ** -0.5)))
            w_ref[rows, ls] = (gs[ii] * act).astype(BF16)

    def step(a_new, a_old, w_new, w_old, step_parity):
        r0 = (((step_parity - 1) % 2) * ni) % 8
        d, t = out_ref.shape
        n_out, n_pre = 8, 2 * (PEER_EC // 256)

        def out_unit(m):
            rs = slice(m * d // n_out, (m + 1) * d // n_out)
            out_ref[rs, :] += _dot(vt_ref[0, 0, rs, :], w_old[...])

        def pre_unit(q):
            rs = slice((q // 2) * 256, (q // 2 + 1) * 256)
            cs = slice((q % 2) * t // 2, (q % 2 + 1) * t // 2)
            a_new[rs, cs] = _dot(u_ref[0, rs, :], h2t_ref[:, cs])

        blocks = [(lt, jg, ig) for lt in range(t // LANES) for jg in range(nk // 16) for ig in range(ni // gi)]
        units = ([functools.partial(out_unit, m) for m in range(n_out)]
                 + [functools.partial(pre_unit, q) for q in range(n_pre)])
        for k, unit in enumerate(units):
            unit()
            for blk in blocks[k * len(blocks) // len(units):(k + 1) * len(blocks) // len(units)]:
                gate_block(a_old, w_new, r0, *blk)

    @pl.when(s % 2 == 0)
    def _():
        step(a_even, a_odd, w_even, w_odd, 0)

    @pl.when(s % 2 == 1)
    def _():
        step(a_odd, a_even, w_odd, w_even, 1)


def _experts(h2t, u, vt, layer, s1, e1, th, e2, n_rows):
    d = h2t.shape[0]
    t, ec = PEER_T, PEER_EC
    nchunk = PEER_EXPERTS // ec
    n_items = (n_rows // t) * nchunk
    item = lambda s, lag: jnp.clip(s - lag, 0, n_items - 1)
    res = lambda a: pl.BlockSpec(a.shape[:2] + (t,), lambda s: (0, 0, item(s, 1) // nchunk))
    ni = ec // PEER_NKEYS
    row = lambda a: pl.BlockSpec((PEER_HEADS, 8, t),
                                 lambda s: (0, (item(s, 1) % nchunk) * ni // 8, item(s, 1) // nchunk))
    return pl.pallas_call(
        functools.partial(_experts_kernel, n_items=n_items),
        grid=(n_items + 2,),
        in_specs=[pl.BlockSpec((d, t), lambda s: (0, item(s, 0) // nchunk)),
                  pl.BlockSpec((1, ec, d), lambda s: (layer, item(s, 0) % nchunk, 0)),
                  pl.BlockSpec((1, 1, d, ec), lambda s: (layer, item(s, 2) % nchunk, 0, 0)),
                  row(s1), row(e1), res(th), res(e2)],
        out_specs=pl.BlockSpec((d, t), lambda s: (0, item(s, 2) // nchunk)),
        out_shape=jax.ShapeDtypeStruct((d, n_rows), F32),
        scratch_shapes=[pltpu.VMEM((ec, t), F32), pltpu.VMEM((ec, t), F32),
                        pltpu.VMEM((ec, t), BF16), pltpu.VMEM((ec, t), BF16)],
        compiler_params=_cp("arbitrary"),
        name="peer_experts",
    )(h2t, u, vt, s1, e1, th, e2)


def _resid_kernel(x_ref, pt_ref, mod_ref, o_ref):
    o_ref[...] = x_ref[...] + mod_ref[0, 5:6, :] * pt_ref[...].T


def _peer_residual(x, pt, mod, n_rows, mod_idx):
    d = x.shape[1]
    return pl.pallas_call(
        _resid_kernel,
        grid=(n_rows // TB,),
        in_specs=[pl.BlockSpec((TB, d), lambda i: (i, 0)),
                  pl.BlockSpec((d, TB), lambda i: (0, i)),
                  pl.BlockSpec((1, 6, d), lambda i: (mod_idx(i), 0, 0))],
        out_specs=pl.BlockSpec((TB, d), lambda i: (i, 0)),
        out_shape=jax.ShapeDtypeStruct((n_rows, d), F32),
        compiler_params=_cp("parallel"),
        name="peer_residual",
    )(x, pt, mod)


def _rope_table(n_tok, dim):
    m = dim // 2
    inv = ROPE_BASE ** (-jnp.arange(0, m, 2, dtype=F32) / m)
    t = jnp.arange(n_tok)
    row = (t // GRID_W).astype(F32)
    col = (t % GRID_W).astype(F32)
    ar, ac = row[:, None] * inv, col[:, None] * inv
    ang = jnp.concatenate([ar, ar, ac, ac], axis=-1)
    sign = np.where((np.arange(dim) % (dim // 2)) < dim // 4, -1.0, 1.0).astype(np.float32)
    reps = LANES // dim
    cos = jnp.tile(jnp.cos(ang), (1, reps))
    sin = jnp.tile(jnp.sin(ang) * sign, (1, reps))
    return (jnp.concatenate([cos, jnp.ones((TB, LANES), F32)], axis=0),
            jnp.concatenate([sin, jnp.zeros((TB, LANES), F32)], axis=0))


def _na_bias_table(rpb, variants):
    cols = np.arange(GRID_W)
    start = np.clip(cols - NA_KW // 2, 0, GRID_W - NA_KW)
    valid_c = (cols[None, :] >= start[:, None]) & (cols[None, :] < start[:, None] + NA_KW)
    dc = np.clip(cols[None, :] - cols[:, None] + (NA_KW - 1), 0, 2 * NA_KW - 2)
    g = rpb.astype(F32)[:, :, dc]
    tabs = []
    for valid_r, dr in variants:
        ok = valid_r[None, :, :, None, None] & valid_c[None, None, None]
        t = jnp.where(ok, g[:, dr], NEG).transpose(0, 1, 3, 2, 4)
        tabs.append(t.reshape(NA_HEADS, NA_GROUP * GRID_W, NA_UNION * GRID_W))
    return jnp.stack(tabs, axis=1)


def _retention_tables(decay_logit):
    C = RT_CHUNK
    log_g = jax.nn.log_sigmoid(decay_logit.astype(F32))
    n = jnp.arange(C, dtype=F32)
    diff = n[:, None] - n[None, :]
    dm = jnp.where(diff >= 0, jnp.exp(log_g[:, :, None, None] * jnp.maximum(diff, 0.0)), 0.0)
    dmat = jnp.stack([dm[0], jnp.swapaxes(dm[1], -1, -2)])
    xi = jnp.exp(log_g[:, :, None] * (n + 1.0))
    zeta = jnp.exp(log_g[:, :, None] * (C - 1.0 - n))
    xi = jnp.stack([xi[0], xi[1, :, ::-1]])
    zeta = jnp.stack([zeta[0], zeta[1, :, ::-1]])
    bc = lambda a: jnp.broadcast_to(a[..., None], a.shape + (LANES,))
    gch = jnp.broadcast_to(jnp.exp(log_g * C)[:, :, None, None], (2, RT_HEADS, 1, LANES))
    return dmat, bc(xi), bc(zeta), gch


def kernel(x, c, ctx, c_ctx, w_mod, b_mod, norm1_g, norm2_g, w_in, diff_lambda, diff_qk_g, diff_sub_g,
           ret_decay, ret_norm_g, na_qk_g, na_rpb, mla_q_norm_g, mla_kv_norm_g, w_uq, w_ukv, mla_qk_g,
           w_branch, w_gate, b_gate, w_o, peer_w_query, peer_sub_keys, peer_u, peer_v):
    B, S, D = x.shape
    depth = w_mod.shape[0]
    assert D == D_MODEL and ctx.shape[1] == CTX_LEN and S % TB_IN == 0 and (B * CTX_LEN) % TB_IN == 0
    assert S % (GRID_W * NA_GROUP) == 0 and S // GRID_W >= NA_UNION
    NL, NC = B * S, B * CTX_LEN
    NT = NL + NC

    mod_idx = lambda i: jnp.minimum(i * TB // S, B)
    mod_idx_in = lambda i: jnp.minimum(i * TB_IN // S, B)
    nlb, spb = NL // TB, S // TB
    rope_idx = lambda i: jnp.where(i < nlb, i % spb, spb)

    xa, xb, xb_row0 = x.reshape(NL, D), ctx.reshape(NC, D), 0
    c_all = jnp.zeros((16, D), F32).at[:B].set(c).at[B].set(c_ctx)
    cos_a, sin_a = _rope_table(S, DA_HEAD)
    cos_r, sin_r = _rope_table(S, RT_DK)
    na_plans, na_variants = _na_groups(S // GRID_W)
    tile2 = lambda g: jnp.tile(g.astype(F32), (1, 2))
    pad_qk = lambda g: jnp.pad(g.astype(F32), (0, 2 * LANES - g.shape[0])).reshape(1, 2 * LANES)

    w_in_bf = jnp.pad(w_in, ((0, 0), (0, 0), (0, IN_PAD - IN_WIDTH))).astype(BF16)
    wg_bf, wb_bf, wo_bf = w_gate.astype(BF16), w_branch.astype(BF16), w_o.astype(BF16)
    wqt = jnp.swapaxes(peer_w_query, 1, 2).astype(BF16)
    sk = peer_sub_keys.reshape(depth, PEER_HEADS * 2, PEER_NKEYS, PEER_DQ // 2).astype(BF16)
    u_bf = peer_u.astype(BF16)
    vt_bf = jnp.swapaxes(peer_v.reshape(depth, PEER_EXPERTS // PEER_EC, PEER_EC, D), 2, 3).astype(BF16)

    for l in range(depth):
        last = l == depth - 1
        lam_init = 0.8 - 0.6 * math.exp(-0.3 * l)
        n_tok = NL if last else NT

        wuq = w_uq[l].reshape(MLA_Q_LORA, MLA_HEADS, MLA_NOPE + MLA_ROPE)
        wuq = jnp.pad(wuq, ((0, 0), (0, 0), (0, 2 * LANES - MLA_NOPE - MLA_ROPE)))
        wuq = wuq.reshape(MLA_Q_LORA, MLA_HEADS * 2 * LANES).astype(BF16)
        wukv = w_ukv[l].reshape(MLA_KV_LORA, MLA_HEADS, MLA_NOPE + MLA_V)
        wuk = wukv[:, :, :MLA_NOPE].reshape(MLA_KV_LORA, MLA_HEADS * MLA_NOPE).astype(BF16)
        wuv = wukv[:, :, MLA_NOPE:].reshape(MLA_KV_LORA, MLA_HEADS * MLA_V).astype(BF16)
        gains = [tile2(diff_qk_g[l]), tile2(na_qk_g[l]), mla_q_norm_g[l].reshape(1, -1),
                 mla_kv_norm_g[l].reshape(1, -1), pad_qk(mla_qk_g[l, 0]), pad_qk(mla_qk_g[l, 1])]

        mod = _modulation(c_all, w_mod, b_mod, l).reshape(16, 6, D)
        h1, p = _in_proj(xa, xb, xb_row0, NL, mod, norm1_g[l].reshape(1, D), w_in_bf, l, NT, mod_idx_in)
        qa, ka, qr, kr, qn, kn, qm, km, vm = _prep(
            p, (cos_a, sin_a, cos_r, sin_r), gains, (wuq, wuk, wuv), NT, rope_idx)

        sub_g = diff_sub_g[l].reshape(1, -1)
        mla_scale = (MLA_NOPE + MLA_ROPE) ** -0.5
        ret_tabs = _retention_tables(ret_decay[l])
        common = dict(batch=B, seq=S)
        ya = _attention(qa, ka, p, (diff_lambda[l], sub_g), mode="diff", heads=DA_HEADS, dq=LANES,
                        v_col0=COL_DA_V, latent=True, lam_init=lam_init, name="diff_attn", **common)
        yb, yb_c = _retention(qr, kr, p, ret_tabs, ret_norm_g[l].reshape(1, -1), ctx_out=not last, **common)
        yc = _na_latent(qn, kn, p, _na_bias_table(na_rpb[l], na_variants), na_plans, **common)
        yd = _attention(qm, km, vm, (), mode="plain", heads=MLA_HEADS, dq=2 * LANES, v_col0=0,
                        latent=True, scale=mla_scale, name="latent_attn", **common)
        ys = [ya, yb, yc, yd]
        ys_ctx = ys
        if not last:
            ya_c = _attention(qa, ka, p, (diff_lambda[l], sub_g), mode="diff", heads=DA_HEADS, dq=LANES,
                              v_col0=COL_DA_V, latent=False, lam_init=lam_init, name="diff_attn_ctx", **common)
            yc_c = _attention(qn, kn, p, (), mode="pair", heads=NA_HEADS // 2, dq=LANES, v_col0=COL_NA_V,
                              latent=False, name="ctx_attn", **common)
            yd_c = _attention(qm, km, vm, (), mode="plain", heads=MLA_HEADS, dq=2 * LANES, v_col0=0,
                              latent=False, scale=mla_scale, name="latent_attn_ctx", **common)
            ys_ctx = [ya_c, yb_c, yc_c, yd_c]

        m = _merge1(h1, ys, ys_ctx, wg_bf, wb_bf, b_gate, l, n_tok)
        x_mid, h2t = _merge2(m, wo_bf, l, xa, xb, xb_row0, NL, mod, norm2_g[l].reshape(1, D), n_tok, mod_idx)
        s1, th, e1, e2 = _route(h2t, wqt, sk, l, n_tok)
        pt = _experts(h2t, u_bf, vt_bf, l, s1, e1, th, e2, n_tok)
        xa = xb = _peer_residual(x_mid, pt, mod, n_tok, mod_idx)
        xb_row0 = NL

    return xa.reshape(B, S, D)
```

```python
import functools
import math

import jax
import jax.numpy as jnp
import numpy as np
from jax import lax
from jax.experimental import pallas as pl
from jax.experimental.pallas import tpu as pltpu

F32 = jnp.float32
BF16 = jnp.bfloat16

D_MODEL = 2048
CTX_LEN = 256
GRID_W = 64
EPS = 1e-6
ROPE_BASE = 10000.0
N_BRANCH = 4
BRANCH_W = 512
DA_HEADS, DA_HEAD = 4, 64
RT_HEADS, RT_DK, RT_CHUNK = 4, 128, 128
NA_HEADS, NA_HEAD, NA_KH, NA_KW = 8, 64, 8, 16
MLA_HEADS, MLA_Q_LORA, MLA_KV_LORA, MLA_NOPE, MLA_ROPE, MLA_V = 4, 512, 256, 128, 64, 128
PEER_HEADS, PEER_NKEYS, PEER_DQ, PEER_TOPK = 8, 128, 256, 16
PEER_EXPERTS = PEER_NKEYS * PEER_NKEYS

LANES = 128
IN_WIDTH = 5952
IN_PAD = 6144
COL_DA_V = 1024 // LANES
COL_RT_V = 2560 // LANES
COL_RT_G = 3072 // LANES
COL_NA_V = 4608 // LANES
OFF_RT_Q, OFF_RT_K = 1536, 2048
OFF_NA_Q, OFF_NA_K = 3584, 4096
OFF_MLA_CQ, OFF_MLA_CKV, OFF_MLA_KR = 5120, 5632, 5888

TB = 512
TB_IN = 1024
TQ = 1024
TQ_SPLIT = 4
PEER_T = 512
PEER_EC = 512
NA_GROUP = 4
NA_UNION = NA_GROUP + NA_KH
NEG = -1e30
VMEM_LIMIT = 56 * 1024 * 1024


def _cp(*sem):
    return pltpu.CompilerParams(dimension_semantics=sem, vmem_limit_bytes=VMEM_LIMIT)


def _dot(a, b):
    return jnp.dot(a, b, preferred_element_type=F32)


def _dot_nt(a, b):
    return lax.dot_general(a, b, (((1,), (1,)), ((), ())), preferred_element_type=F32)


def _sigmoid(z):
    return 1.0 / (1.0 + jnp.exp(-z))


def _rms_rows(x, g):
    return x * lax.rsqrt(jnp.mean(x * x, axis=-1, keepdims=True) + EPS) * g


def _mod_kernel(c_ref, w_ref, b_ref, o_ref):
    c = c_ref[...]
    a = (c * _sigmoid(c)).astype(BF16)
    o_ref[...] = _dot(a, w_ref[0].astype(BF16)) + b_ref[0]


def _modulation(c_all, w_mod, b_mod, layer):
    rows, d = c_all.shape
    depth, _, n = w_mod.shape
    cb = 1024
    return pl.pallas_call(
        _mod_kernel,
        grid=(n // cb,),
        in_specs=[pl.BlockSpec((rows, d), lambda j: (0, 0)),
                  pl.BlockSpec((1, d, cb), lambda j: (layer, 0, j)),
                  pl.BlockSpec((1, 1, cb), lambda j: (layer, 0, j))],
        out_specs=pl.BlockSpec((rows, cb), lambda j: (0, j)),
        out_shape=jax.ShapeDtypeStruct((rows, n), F32),
        compiler_params=_cp("parallel"),
        name="adaln_mod",
    )(c_all, w_mod, b_mod.reshape(depth, 1, n))


def _split_rows(n_a, b_off):
    return (lambda i, *_: (jnp.minimum(i, n_a - 1), 0),
            lambda i, *_: (jnp.maximum(i - n_a, 0) + b_off, 0))


def _inproj_kernel(xa_ref, xb_ref, mod_ref, g_ref, w_ref, h_ref, p_ref, *, n_a):
    def prologue(x_ref):
        y = _rms_rows(x_ref[...], g_ref[...])
        h_ref[...] = (y * (1.0 + mod_ref[0, 1:2, :]) + mod_ref[0, 0:1, :]).astype(BF16)

    first = pl.program_id(1) == 0
    pl.when(first & (pl.program_id(0) < n_a))(functools.partial(prologue, xa_ref))
    pl.when(first & (pl.program_id(0) >= n_a))(functools.partial(prologue, xb_ref))
    p_ref[...] = _dot(h_ref[...], w_ref[0]).astype(BF16)


def _in_proj(xa, xb, xb_row0, n_lat, mod, g, w_bf, layer, n_rows, mod_idx):
    d = xa.shape[1]
    cb = 512
    amap, bmap = _split_rows(n_lat // TB_IN, xb_row0 // TB_IN)
    return pl.pallas_call(
        functools.partial(_inproj_kernel, n_a=n_lat // TB_IN),
        grid=(n_rows // TB_IN, IN_PAD // cb),
        in_specs=[pl.BlockSpec((TB_IN, d), amap),
                  pl.BlockSpec((TB_IN, d), bmap),
                  pl.BlockSpec((1, 6, d), lambda i, j: (mod_idx(i), 0, 0)),
                  pl.BlockSpec((1, d), lambda i, j: (0, 0)),
                  pl.BlockSpec((1, d, cb), lambda i, j: (layer, 0, j))],
        out_specs=[pl.BlockSpec((TB_IN, d), lambda i, j: (i, 0)),
                   pl.BlockSpec((TB_IN, cb), lambda i, j: (i, j))],
        out_shape=[jax.ShapeDtypeStruct((n_rows, d), BF16),
                   jax.ShapeDtypeStruct((n_rows, IN_PAD), BF16)],
        compiler_params=_cp("parallel", "arbitrary"),
        name="norm1_in_proj",
    )(xa, xb, mod, g, w_bf)


def _rot_partner(x, q):
    lane = lax.broadcasted_iota(jnp.int32, x.shape, 1)
    first = (lane & (2 * q - 1)) < q
    return jnp.where(first, pltpu.roll(x, LANES - q, 1), pltpu.roll(x, q, 1))


def _rope(x, cos, sin_signed, q):
    return x * cos + _rot_partner(x, q) * sin_signed


def _group_sumsq(x, gsz):
    x2 = x * x
    hi = x2.astype(BF16)
    lo = (x2 - hi.astype(F32)).astype(BF16)
    r = lax.broadcasted_iota(jnp.int32, (LANES, LANES), 0) // gsz
    c = lax.broadcasted_iota(jnp.int32, (LANES, LANES), 1) // gsz
    ones = jnp.where(r == c, 1.0, 0.0).astype(BF16)
    return _dot(hi, ones) + _dot(lo, ones)


def _prep_kernel(p_ref, cosa_ref, sina_ref, cosr_ref, sinr_ref, gda_ref, gna_ref, gq_ref, gkv_ref,
                 gmq_ref, gmk_ref, wuq_ref, wuk_ref, wuv_ref,
                 qa_ref, ka_ref, qr_ref, kr_ref, qn_ref, kn_ref, qm_ref, km_ref, vm_ref):
    cosa, sina = cosa_ref[...], sina_ref[...]
    cosr, sinr = cosr_ref[...], sinr_ref[...]
    for t in range(4):
        sl = slice(t * LANES, (t + 1) * LANES)
        for off, gi, oref, scale in ((0, 0, qa_ref, DA_HEAD ** -0.5), (512, 1, ka_ref, 1.0)):
            x = p_ref[:, off + t * LANES: off + (t + 1) * LANES].astype(F32)
            y = x * lax.rsqrt(_group_sumsq(x, DA_HEAD) * (1.0 / DA_HEAD) + EPS) * gda_ref[gi:gi + 1, :]
            oref[:, sl] = (_rope(y, cosa, sina, DA_HEAD // 4) * scale).astype(BF16)
        x = p_ref[:, OFF_RT_Q + t * LANES: OFF_RT_Q + (t + 1) * LANES].astype(F32)
        qr_ref[:, sl] = _rope(x, cosr, sinr, RT_DK // 4).astype(BF16)
        x = p_ref[:, OFF_RT_K + t * LANES: OFF_RT_K + (t + 1) * LANES].astype(F32) * (RT_DK ** -0.5)
        kr_ref[:, sl] = _rope(x, cosr, sinr, RT_DK // 4).astype(BF16)
        for off, gi, oref, scale in ((OFF_NA_Q, 0, qn_ref, NA_HEAD ** -0.5), (OFF_NA_K, 1, kn_ref, 1.0)):
            x = p_ref[:, off + t * LANES: off + (t + 1) * LANES].astype(F32)
            y = x * lax.rsqrt(_group_sumsq(x, NA_HEAD) * (1.0 / NA_HEAD) + EPS) * gna_ref[gi:gi + 1, :]
            oref[:, sl] = (y * scale).astype(BF16)
    cq = p_ref[:, OFF_MLA_CQ:OFF_MLA_CQ + MLA_Q_LORA].astype(F32)
    q = _dot(_rms_rows(cq, gq_ref[...]).astype(BF16), wuq_ref[...])
    ckv = p_ref[:, OFF_MLA_CKV:OFF_MLA_CKV + MLA_KV_LORA].astype(F32)
    ckv_n = _rms_rows(ckv, gkv_ref[...]).astype(BF16)
    k_nope = _dot(ckv_n, wuk_ref[...])
    vm_ref[...] = _dot(ckv_n, wuv_ref[...]).astype(BF16)
    kr = p_ref[:, OFF_MLA_KR:OFF_MLA_KR + LANES].astype(F32)
    kr_ss = jnp.sum(kr * kr, axis=-1, keepdims=True)
    inv_n = 1.0 / (MLA_NOPE + MLA_ROPE)
    for h in range(MLA_HEADS):
        a, b = h * 2 * LANES, h * 2 * LANES + LANES
        q0, q1 = q[:, a:b], q[:, b:b + LANES]
        r = lax.rsqrt((jnp.sum(q0 * q0, axis=-1, keepdims=True)
                       + jnp.sum(q1 * q1, axis=-1, keepdims=True)) * inv_n + EPS)
        qm_ref[:, a:b] = (q0 * r * gmq_ref[:, 0:LANES]).astype(BF16)
        qm_ref[:, b:b + LANES] = _rope(q1 * r * gmq_ref[:, LANES:2 * LANES], cosa, sina,
                                       MLA_ROPE // 4).astype(BF16)
        k0 = k_nope[:, h * LANES:(h + 1) * LANES]
        r = lax.rsqrt((jnp.sum(k0 * k0, axis=-1, keepdims=True) + kr_ss) * inv_n + EPS)
        km_ref[:, a:b] = (k0 * r * gmk_ref[:, 0:LANES]).astype(BF16)
        km_ref[:, b:b + LANES] = _rope(kr * r * gmk_ref[:, LANES:2 * LANES], cosa, sina,
                                       MLA_ROPE // 4).astype(BF16)


def _prep(p, tabs, gains, weights, n_rows, rope_idx):
    row = lambda w: pl.BlockSpec((TB, w), lambda i: (i, 0))
    tab = pl.BlockSpec((TB, LANES), lambda i: (rope_idx(i), 0))
    full = lambda a: pl.BlockSpec(a.shape, lambda i: (0,) * a.ndim)
    outs = [512] * 6 + [1024, 1024, 512]
    return pl.pallas_call(
        _prep_kernel,
        grid=(n_rows // TB,),
        in_specs=[row(IN_PAD)] + [tab] * 4 + [full(a) for a in gains] + [full(a) for a in weights],
        out_specs=[row(w) for w in outs],
        out_shape=[jax.ShapeDtypeStruct((n_rows, w), BF16) for w in outs],
        compiler_params=_cp("parallel"),
        name="mixer_prep",
    )(p, *tabs, *gains, *weights)


def _softmax_parts(scores):
    m = functools.reduce(jnp.maximum, [jnp.max(s, axis=-1, keepdims=True) for s in scores])
    es = [jnp.exp(s - m) for s in scores]
    z = functools.reduce(jnp.add, [jnp.sum(e, axis=-1, keepdims=True) for e in es])
    return es, z


def _attn_kernel(*refs, mode, nseg, scale, lam_init, nsplit):
    q_ref = refs[0]
    ks = [refs[1 + 2 * s] for s in range(nseg)]
    vs = [refs[2 + 2 * s] for s in range(nseg)]
    o_ref = refs[-1]
    rows = q_ref.shape[0] // nsplit

    def scores(g):
        q = q_ref[g * rows:(g + 1) * rows, :]
        if mode == "plain":
            return [[_dot_nt(q, k[...]) * scale for k in ks]]
        lane = lax.broadcasted_iota(jnp.int32, q.shape, 1)
        halves = [jnp.where(lane < LANES // 2, q, jnp.zeros_like(q)),
                  jnp.where(lane >= LANES // 2, q, jnp.zeros_like(q))]
        return [[_dot_nt(qh, k[...]) for k in ks] for qh in halves]

    def weighted(es):
        return functools.reduce(jnp.add, [_dot(e.astype(BF16), v[...]) for e, v in zip(es, vs)])

    if mode == "diff":
        lam_ref, subg_ref = refs[1 + 2 * nseg], refs[2 + 2 * nseg]
        lv = lam_ref[...]
        lam = (jnp.exp(jnp.sum(lv[0:1] * lv[1:2], axis=-1, keepdims=True))
               - jnp.exp(jnp.sum(lv[2:3] * lv[3:4], axis=-1, keepdims=True)) + lam_init)

    def finish(g, sc):
        parts = [_softmax_parts(s) for s in sc]
        if mode == "plain":
            (es, z), = parts
            y = weighted(es) / z
        elif mode == "pair":
            outs = [weighted(es) / z for es, z in parts]
            lo = lax.broadcasted_iota(jnp.int32, outs[0].shape, 1) < LANES // 2
            y = jnp.where(lo, outs[0], outs[1])
        else:
            (e0, z0), (e1, z1) = parts
            w0, w1 = 1.0 / z0, lam / z1
            y = weighted([a * w0 - b * w1 for a, b in zip(e0, e1)])
            y = _rms_rows(y, subg_ref[...]) * (1.0 - lam_init)
        o_ref[g * rows:(g + 1) * rows, :] = y.astype(o_ref.dtype)

    sc = [scores(g) for g in range(nsplit)]
    for g in range(nsplit):
        finish(g, sc[g])


def _attention(q_arr, k_arr, v_arr, extra, *, mode, heads, dq, v_col0, batch, seq, latent, scale=1.0,
               lam_init=0.0, name):
    nl = batch * seq
    cblk = nl // CTX_LEN
    if latent:
        nq = seq // TQ
        grid = (batch, heads, nq)
        q_spec = pl.BlockSpec((TQ, dq), lambda b, h, j: (b * nq + j, h))
        segs = [(CTX_LEN, lambda b, h, j: (cblk + b, h), lambda b, h, j: (cblk + b, v_col0 + h)),
                (seq, lambda b, h, j: (b, h), lambda b, h, j: (b, v_col0 + h))]
        o_spec = pl.BlockSpec((TQ, LANES), lambda b, h, j: (b * nq + j, h))
        rows = nl
    else:
        grid = (batch, heads, 1)
        q_spec = pl.BlockSpec((CTX_LEN, dq), lambda b, h, j: (cblk + b, h))
        segs = [(CTX_LEN, lambda b, h, j: (cblk + b, h), lambda b, h, j: (cblk + b, v_col0 + h))]
        o_spec = pl.BlockSpec((CTX_LEN, LANES), lambda b, h, j: (b, h))
        rows = batch * CTX_LEN
    in_specs, args = [q_spec], [q_arr]
    for n, kmap, vmap in segs:
        in_specs += [pl.BlockSpec((n, dq), kmap), pl.BlockSpec((n, LANES), vmap)]
        args += [k_arr, v_arr]
    for a in extra:
        in_specs.append(pl.BlockSpec(a.shape, lambda b, h, j: (0, 0)))
        args.append(a)
    return pl.pallas_call(
        functools.partial(_attn_kernel, mode=mode, nseg=len(segs), scale=scale, lam_init=lam_init,
                          nsplit=TQ_SPLIT if latent else 1),
        grid=grid, in_specs=in_specs, out_specs=o_spec,
        out_shape=jax.ShapeDtypeStruct((rows, heads * LANES), BF16),
        compiler_params=_cp("parallel", "parallel", "arbitrary"),
        name=name,
    )(*args)


def _ret_kernel(ql_ref, qc_ref, kl_ref, kc_ref, vl_ref, vc_ref, gl_ref, gc_ref, dmat_ref, xi_ref, zeta_ref,
                gch_ref, ng_ref, yl_ref, yc_ref, accl_ref, accc_ref, *, n_lat, n_ctx, ctx_out):
    C = RT_CHUNK

    def chunk(q, k, v, R, d):
        s = _dot_nt(q, k) * dmat_ref[d, 0]
        inner = _dot(s.astype(BF16), v)
        cross = _dot(q, R.astype(BF16)) * xi_ref[d, 0]
        kz = (k.astype(F32) * zeta_ref[d, 0]).T.astype(BF16)
        return inner + cross, gch_ref[d, 0] * R + _dot(kz, v)

    for d in range(2):
        R = jnp.zeros((RT_DK, RT_DK), F32)
        for j in range(n_ctx):
            c = j if d == 0 else n_ctx - 1 - j
            sl = pl.ds(c * C, C)
            o, R = chunk(qc_ref[sl, :], kc_ref[sl, :], vc_ref[sl, :], R, d)
            if ctx_out:
                if d == 0:
                    accc_ref[sl, :] = o
                else:
                    accc_ref[sl, :] += o

        for j in range(n_lat):
            c = j if d == 0 else n_lat - 1 - j
            sl = pl.ds(c * C, C)
            o, R = chunk(ql_ref[sl, :], kl_ref[sl, :], vl_ref[sl, :], R, d)
            if d == 0:
                accl_ref[sl, :] = o
            else:
                accl_ref[sl, :] += o

    def post(acc_ref, g_ref, y_ref):
        g = g_ref[...].astype(F32)
        y_ref[...] = (_rms_rows(acc_ref[...], ng_ref[...]) * (g * _sigmoid(g))).astype(y_ref.dtype)

    post(accl_ref, gl_ref, yl_ref)
    if ctx_out:
        post(accc_ref, gc_ref, yc_ref)
    else:
        yc_ref[...] = jnp.zeros_like(yc_ref)


def _retention(qr, kr, p, tabs, norm_g, *, batch, seq, ctx_out):
    nl = batch * seq
    cblk = nl // CTX_LEN
    lat = lambda c0: pl.BlockSpec((seq, LANES), lambda b, h: (b, c0 + h))
    ctx = lambda c0: pl.BlockSpec((CTX_LEN, LANES), lambda b, h: (cblk + b, c0 + h))
    tab = lambda a: pl.BlockSpec((2, 1) + a.shape[2:], lambda b, h: (0, h, 0, 0))
    return pl.pallas_call(
        functools.partial(_ret_kernel, n_lat=seq // RT_CHUNK, n_ctx=CTX_LEN // RT_CHUNK, ctx_out=ctx_out),
        grid=(batch, RT_HEADS),
        in_specs=[lat(0), ctx(0), lat(0), ctx(0), lat(COL_RT_V), ctx(COL_RT_V), lat(COL_RT_G), ctx(COL_RT_G)]
        + [tab(a) for a in tabs] + [pl.BlockSpec((1, LANES), lambda b, h: (0, 0))],
        out_specs=[pl.BlockSpec((seq, LANES), lambda b, h: (b, h)),
                   pl.BlockSpec((CTX_LEN, LANES), lambda b, h: (b, h))],
        out_shape=[jax.ShapeDtypeStruct((nl, RT_HEADS * LANES), BF16),
                   jax.ShapeDtypeStruct((batch * CTX_LEN, RT_HEADS * LANES), BF16)],
        scratch_shapes=[pltpu.VMEM((seq, LANES), F32), pltpu.VMEM((CTX_LEN, LANES), F32)],
        compiler_params=_cp("parallel", "parallel"),
        name="retention",
    )(qr, qr, kr, kr, p, p, p, p, *tabs, norm_g)


def _na_groups(rows):
    plans, variants = [], []
    for gi in range(rows // NA_GROUP):
        r = gi * NA_GROUP + np.arange(NA_GROUP)
        u0 = int(np.clip(gi * NA_GROUP - NA_KH // 2, 0, rows - NA_UNION))
        rs = np.clip(r - NA_KH // 2, 0, rows - NA_KH)
        krow = u0 + np.arange(NA_UNION)
        valid = (krow[None, :] >= rs[:, None]) & (krow[None, :] < rs[:, None] + NA_KH)
        dr = np.where(valid, krow[None, :] - r[:, None] + (NA_KH - 1), 0)
        key = (valid.tobytes(), dr.tobytes())
        for v, (k, _, _) in enumerate(variants):
            if k == key:
                break
        else:
            v = len(variants)
            variants.append((key, valid, dr))
        plans.append((u0, v))
    return tuple(plans), [(valid, dr) for _, valid, dr in variants]


def _na_kernel(q_ref, kl_ref, kc_ref, vl_ref, vc_ref, bias_ref, o_ref, *, plans):
    gq, gk = NA_GROUP * GRID_W, NA_UNION * GRID_W
    kc, vc = kc_ref[...], vc_ref[...]

    def scores(gi):
        u0, var = plans[gi]
        q = q_ref[gi * gq:(gi + 1) * gq, :]
        kw = kl_ref[u0 * GRID_W:u0 * GRID_W + gk, :]
        lane = lax.broadcasted_iota(jnp.int32, q.shape, 1)
        out = []
        for hh in range(2):
            keep = (lane < LANES // 2) if hh == 0 else (lane >= LANES // 2)
            qh = jnp.where(keep, q, jnp.zeros_like(q))
            out.append([_dot_nt(qh, kw) + bias_ref[hh, var], _dot_nt(qh, kc)])
        return out

    def finish(gi, sc):
        u0, _ = plans[gi]
        vw = vl_ref[u0 * GRID_W:u0 * GRID_W + gk, :]
        outs = []
        for s in sc:
            (el, ec), z = _softmax_parts(s)
            outs.append((_dot(el.astype(BF16), vw) + _dot(ec.astype(BF16), vc)) / z)
        lo = lax.broadcasted_iota(jnp.int32, outs[0].shape, 1) < LANES // 2
        o_ref[gi * gq:(gi + 1) * gq, :] = jnp.where(lo, outs[0], outs[1]).astype(o_ref.dtype)

    pending = scores(0)
    for gi in range(len(plans)):
        nxt = scores(gi + 1) if gi + 1 < len(plans) else None
        finish(gi, pending)
        pending = nxt


def _na_latent(qn, kn, p, bias, plans, *, batch, seq):
    nl = batch * seq
    cblk = nl // CTX_LEN
    hp = NA_HEADS // 2
    return pl.pallas_call(
        functools.partial(_na_kernel, plans=plans),
        grid=(batch, hp),
        in_specs=[pl.BlockSpec((seq, LANES), lambda b, h: (b, h)),
                  pl.BlockSpec((seq, LANES), lambda b, h: (b, h)),
                  pl.BlockSpec((CTX_LEN, LANES), lambda b, h: (cblk + b, h)),
                  pl.BlockSpec((seq, LANES), lambda b, h: (b, COL_NA_V + h)),
                  pl.BlockSpec((CTX_LEN, LANES), lambda b, h: (cblk + b, COL_NA_V + h)),
                  pl.BlockSpec((2,) + bias.shape[1:], lambda b, h: (h, 0, 0, 0))],
        out_specs=pl.BlockSpec((seq, LANES), lambda b, h: (b, h)),
        out_shape=jax.ShapeDtypeStruct((nl, hp * LANES), BF16),
        compiler_params=_cp("parallel", "parallel"),
        name="neighbourhood_attn",
    )(qn, kn, kn, p, p, bias)


def _merge1_kernel(h_ref, *refs, n_a):
    lat, ctx = refs[:N_BRANCH], refs[N_BRANCH:2 * N_BRANCH]
    wg_ref, wb_ref, bg_ref, m_ref = refs[2 * N_BRANCH:]

    def body(y_refs):
        h = h_ref[...]
        acc = None
        for n, y_ref in enumerate(y_refs):
            gate = _sigmoid(_dot(h, wg_ref[0, n]) + bg_ref[0, n:n + 1, :])
            term = gate * _dot(y_ref[...], wb_ref[0, n])
            acc = term if acc is None else acc + term
        m_ref[...] = acc.astype(BF16)

    pl.when(pl.program_id(1) < n_a)(functools.partial(body, lat))
    pl.when(pl.program_id(1) >= n_a)(functools.partial(body, ctx))


def _merge1(h, ys_lat, ys_ctx, wg, wb, bg, layer, n_rows):
    d = h.shape[1]
    cb = 512
    n_a = ys_lat[0].shape[0] // TB
    lat_map = lambda j, i: (jnp.minimum(i, n_a - 1), 0)
    ctx_map = lambda j, i: (jnp.maximum(i - n_a, 0), 0)
    return pl.pallas_call(
        functools.partial(_merge1_kernel, n_a=n_a),
        grid=(d // cb, n_rows // TB),
        in_specs=[pl.BlockSpec((TB, d), lambda j, i: (i, 0))]
        + [pl.BlockSpec((TB, BRANCH_W), lat_map)] * N_BRANCH
        + [pl.BlockSpec((TB, BRANCH_W), ctx_map)] * N_BRANCH
        + [pl.BlockSpec((1, N_BRANCH, d, cb), lambda j, i: (layer, 0, 0, j)),
           pl.BlockSpec((1, N_BRANCH, BRANCH_W, cb), lambda j, i: (layer, 0, 0, j)),
           pl.BlockSpec((1, N_BRANCH, cb), lambda j, i: (layer, 0, j))],
        out_specs=pl.BlockSpec((TB, cb), lambda j, i: (i, j)),
        out_shape=jax.ShapeDtypeStruct((n_rows, d), BF16),
        compiler_params=_cp("parallel", "parallel"),
        name="gated_branch_sum",
    )(h, *ys_lat, *ys_ctx, wg, wb, bg)


def _merge2_kernel(m_ref, wo_ref, xa_ref, xb_ref, mod_ref, g_ref, xo_ref, h2t_ref, *, n_a):
    def body(x_ref):
        x = x_ref[...] + mod_ref[0, 2:3, :] * _dot(m_ref[...], wo_ref[0])
        xo_ref[...] = x
        h2 = _rms_rows(x, g_ref[...]) * (1.0 + mod_ref[0, 4:5, :]) + mod_ref[0, 3:4, :]
        h2t_ref[...] = h2.T.astype(BF16)

    pl.when(pl.program_id(0) < n_a)(functools.partial(body, xa_ref))
    pl.when(pl.program_id(0) >= n_a)(functools.partial(body, xb_ref))


def _merge2(m, wo, layer, xa, xb, xb_row0, n_lat, mod, g, n_rows, mod_idx):
    d = xa.shape[1]
    amap, bmap = _split_rows(n_lat // TB, xb_row0 // TB)
    return pl.pallas_call(
        functools.partial(_merge2_kernel, n_a=n_lat // TB),
        grid=(n_rows // TB,),
        in_specs=[pl.BlockSpec((TB, d), lambda i: (i, 0)),
                  pl.BlockSpec((1, d, d), lambda i: (layer, 0, 0)),
                  pl.BlockSpec((TB, d), amap),
                  pl.BlockSpec((TB, d), bmap),
                  pl.BlockSpec((1, 6, d), lambda i: (mod_idx(i), 0, 0)),
                  pl.BlockSpec((1, d), lambda i: (0, 0))],
        out_specs=[pl.BlockSpec((TB, d), lambda i: (i, 0)),
                   pl.BlockSpec((d, TB), lambda i: (0, i))],
        out_shape=[jax.ShapeDtypeStruct((n_rows, d), F32),
                   jax.ShapeDtypeStruct((d, n_rows), BF16)],
        compiler_params=_cp("parallel"),
        name="out_proj_residual_norm2",
    )(m, wo, xa, xb, mod, g)


def _merge_exchange(n):
    comps, p = [], 1
    while p < n:
        k = p
        while k >= 1:
            for j in range(k % p, n - k, 2 * k):
                for i in range(min(k, n - j - k)):
                    if (i + j) // (2 * p) == (i + j + k) // (2 * p):
                        comps.append((i + j, i + j + k))
            k //= 2
        p *= 2
    return comps


def _top_values(x, k):
    sub = 8
    cols = [x[v * sub:(v + 1) * sub, :] for v in range(x.shape[0] // sub)]
    for i, j in _merge_exchange(len(cols)):
        cols[i], cols[j] = jnp.maximum(cols[i], cols[j]), jnp.minimum(cols[i], cols[j])
    tops = []
    for it in range(k):
        m = jnp.max(cols[0], axis=0, keepdims=True)
        tops.append(m)
        if it == k - 1:
            break
        hit = cols[0] == m
        depth = min(len(cols), k - it)
        for d in range(depth - 1):
            cols[d] = jnp.where(hit, cols[d + 1], cols[d])
        if depth == len(cols):
            cols[depth - 1] = jnp.where(hit, -jnp.inf, cols[depth - 1])
    return tops


def _route_kernel(h2t_ref, wqt_ref, sk_ref, s1_ref, th_ref, e1_ref, e2_ref):
    qt = _dot(wqt_ref[0], h2t_ref[...]).astype(BF16)
    nk, k, half = PEER_NKEYS, PEER_TOPK, PEER_TOPK // 2
    for h in range(PEER_HEADS):
        s1 = _dot(sk_ref[0, 2 * h], qt[(2 * h) * nk:(2 * h + 1) * nk, :])
        s2 = _dot(sk_ref[0, 2 * h + 1], qt[(2 * h + 1) * nk:(2 * h + 2) * nk, :])
        t1, t2 = _top_values(s1, k), _top_values(s2, k)
        t1_hi = jnp.concatenate(t1[half:], axis=0)
        t2_lo, t2_hi = jnp.concatenate(t2[:half], axis=0), jnp.concatenate(t2[half:], axis=0)
        cand = jnp.concatenate([t1[a] + t2_lo for a in range(half)] + [t1[0] + t2_hi, t1_hi + t2[0]], axis=0)
        sel = _top_values(cand, k)
        tau = sel[-1]
        z = functools.reduce(jnp.add, [jnp.exp(c - sel[0]) for c in sel])
        th = jnp.full(s2.shape, jnp.inf, F32)
        for a in range(half):
            th = jnp.where(t1[a] + s2 >= tau, t1[a], th)
        th0 = jnp.full(tau.shape, jnp.inf, F32)
        for a in range(half, k):
            th0 = jnp.where(t1[a] + t2[0] >= tau, t1[a], th0)
        th = jnp.where(s2 == t2[0], jnp.minimum(th, th0), th)
        s1_ref[h] = s1
        th_ref[h] = th
        e1_ref[h] = jnp.exp(s1 - t1[0]) / z * 0.5
        e2_ref[h] = jnp.exp(s2 - t2[0])


def _route(h2t, wqt, sk, layer, n_rows):
    d = h2t.shape[0]
    t = 256
    big = pl.BlockSpec((PEER_HEADS, PEER_NKEYS, t), lambda i: (0, 0, i))
    return pl.pallas_call(
        _route_kernel,
        grid=(n_rows // t,),
        in_specs=[pl.BlockSpec((d, t), lambda i: (0, i)),
                  pl.BlockSpec((1,) + wqt.shape[1:], lambda i: (layer, 0, 0)),
                  pl.BlockSpec((1,) + sk.shape[1:], lambda i: (layer, 0, 0, 0))],
        out_specs=[big] * 4,
        out_shape=[jax.ShapeDtypeStruct((PEER_HEADS, PEER_NKEYS, n_rows), F32)] * 4,
        compiler_params=_cp("parallel"),
        name="peer_route",
    )(h2t, wqt, sk)


def _experts_kernel(h2t_ref, u_ref, vt_ref, s1_ref, e1_ref, th_ref, e2_ref, x_ref, mod_ref, xo_ref,
                    a_even, a_odd, w_even, w_odd, out_ref, *, n_items):
    s = pl.program_id(0)
    nk = PEER_NKEYS
    ni = PEER_EC // nk
    nchunk = PEER_EXPERTS // PEER_EC

    @pl.when(s == 0)
    def _():
        a_odd[...] = jnp.zeros_like(a_odd)
        w_odd[...] = jnp.zeros_like(w_odd)

    @pl.when((s == 0) | ((s >= 2) & ((s - 2) % nchunk == 0)))
    def _():
        out_ref[...] = jnp.zeros_like(out_ref)

    gi = 4
    assert nchunk % 2 == 0 and (2 * ni) % 8 == 0

    def gate_block(a_ref, w_ref, r0, lt, jg, ig):
        ls = slice(lt * LANES, (lt + 1) * LANES)
        js = slice(jg * 16, (jg + 1) * 16)
        gs = [None] * gi
        for h in range(PEER_HEADS):
            th, e2 = th_ref[h, js, ls], e2_ref[h, js, ls]
            for ii in range(gi):
                r = slice(r0 + ig * gi + ii, r0 + ig * gi + ii + 1)
                term = jnp.where(s1_ref[h, r, ls] >= th, e2 * e1_ref[h, r, ls], 0.0)
                gs[ii] = term if gs[ii] is None else gs[ii] + term
        for ii in range(gi):
            r = ig * gi + ii
            rows = slice(r * nk + jg * 16, r * nk + (jg + 1) * 16)
            a = a_ref[rows, ls]
            act = a * (1.0 + lax.erf(a * (2.0 ** -0.5)))
            w_ref[rows, ls] = (gs[ii] * act).astype(BF16)

    def step(a_new, a_old, w_new, w_old, step_parity):
        r0 = (((step_parity - 1) % 2) * ni) % 8
        d, t = out_ref.shape
        n_out, n_pre = 8, 2 * (PEER_EC // 256)

        def out_unit(m):
            rs = slice(m * d // n_out, (m + 1) * d // n_out)
            out_ref[rs, :] += _dot(vt_ref[0, 0, rs, :], w_old[...])

        def pre_unit(q):
            rs = slice((q // 2) * 256, (q // 2 + 1) * 256)
            cs = slice((q % 2) * t // 2, (q % 2 + 1) * t // 2)
            a_new[rs, cs] = _dot(u_ref[0, rs, :], h2t_ref[:, cs])

        blocks = [(lt, jg, ig) for lt in range(t // LANES) for jg in range(nk // 16) for ig in range(ni // gi)]
        units = ([functools.partial(out_unit, m) for m in range(n_out)]
                 + [functools.partial(pre_unit, q) for q in range(n_pre)])
        for k, unit in enumerate(units):
            unit()
            for blk in blocks[k * len(blocks) // len(units):(k + 1) * len(blocks) // len(units)]:
                gate_block(a_old, w_new, r0, *blk)

    @pl.when(s % 2 == 0)
    def _():
        step(a_even, a_odd, w_even, w_odd, 0)

    @pl.when(s % 2 == 1)
    def _():
        step(a_odd, a_even, w_odd, w_even, 1)

    @pl.when((s >= 2) & ((s - 2) % nchunk == nchunk - 1))
    def _():
        xo_ref[...] = x_ref[...] + mod_ref[0, 5:6, :] * out_ref[...].T


def _experts(h2t, u, vt, layer, s1, e1, th, e2, x, mod, n_rows, mod_idx):
    d = h2t.shape[0]
    t, ec = PEER_T, PEER_EC
    assert t == TB
    nchunk = PEER_EXPERTS // ec
    n_items = (n_rows // t) * nchunk
    item = lambda s, lag: jnp.clip(s - lag, 0, n_items - 1)
    res = lambda a: pl.BlockSpec(a.shape[:2] + (t,), lambda s: (0, 0, item(s, 1) // nchunk))
    ni = ec // PEER_NKEYS
    row = lambda a: pl.BlockSpec((PEER_HEADS, 8, t),
                                 lambda s: (0, (item(s, 1) % nchunk) * ni // 8, item(s, 1) // nchunk))
    return pl.pallas_call(
        functools.partial(_experts_kernel, n_items=n_items),
        grid=(n_items + 2,),
        in_specs=[pl.BlockSpec((d, t), lambda s: (0, item(s, 0) // nchunk)),
                  pl.BlockSpec((1, ec, d), lambda s: (layer, item(s, 0) % nchunk, 0)),
                  pl.BlockSpec((1, 1, d, ec), lambda s: (layer, item(s, 2) % nchunk, 0, 0)),
                  row(s1), row(e1), res(th), res(e2),
                  pl.BlockSpec((t, d), lambda s: (item(s, 2) // nchunk, 0)),
                  pl.BlockSpec((1, 6, d), lambda s: (mod_idx(item(s, 2) // nchunk), 0, 0))],
        out_specs=pl.BlockSpec((t, d), lambda s: (item(s, 2) // nchunk, 0)),
        out_shape=jax.ShapeDtypeStruct((n_rows, d), F32),
        scratch_shapes=[pltpu.VMEM((ec, t), F32), pltpu.VMEM((ec, t), F32),
                        pltpu.VMEM((ec, t), BF16), pltpu.VMEM((ec, t), BF16),
                        pltpu.VMEM((d, t), F32)],
        compiler_params=_cp("arbitrary"),
        name="peer_experts",
    )(h2t, u, vt, s1, e1, th, e2, x, mod)


def _rope_table(n_tok, dim):
    m = dim // 2
    inv = ROPE_BASE ** (-jnp.arange(0, m, 2, dtype=F32) / m)
    t = jnp.arange(n_tok)
    row = (t // GRID_W).astype(F32)
    col = (t % GRID_W).astype(F32)
    ar, ac = row[:, None] * inv, col[:, None] * inv
    ang = jnp.concatenate([ar, ar, ac, ac], axis=-1)
    sign = np.where((np.arange(dim) % (dim // 2)) < dim // 4, -1.0, 1.0).astype(np.float32)
    reps = LANES // dim
    cos = jnp.tile(jnp.cos(ang), (1, reps))
    sin = jnp.tile(jnp.sin(ang) * sign, (1, reps))
    return (jnp.concatenate([cos, jnp.ones((TB, LANES), F32)], axis=0),
            jnp.concatenate([sin, jnp.zeros((TB, LANES), F32)], axis=0))


def _na_bias_table(rpb, variants):
    cols = np.arange(GRID_W)
    start = np.clip(cols - NA_KW // 2, 0, GRID_W - NA_KW)
    valid_c = (cols[None, :] >= start[:, None]) & (cols[None, :] < start[:, None] + NA_KW)
    dc = np.clip(cols[None, :] - cols[:, None] + (NA_KW - 1), 0, 2 * NA_KW - 2)
    g = rpb.astype(F32)[:, :, dc]
    tabs = []
    for valid_r, dr in variants:
        ok = valid_r[None, :, :, None, None] & valid_c[None, None, None]
        t = jnp.where(ok, g[:, dr], NEG).transpose(0, 1, 3, 2, 4)
        tabs.append(t.reshape(NA_HEADS, NA_GROUP * GRID_W, NA_UNION * GRID_W))
    return jnp.stack(tabs, axis=1)


def _retention_tables(decay_logit):
    C = RT_CHUNK
    log_g = jax.nn.log_sigmoid(decay_logit.astype(F32))
    n = jnp.arange(C, dtype=F32)
    diff = n[:, None] - n[None, :]
    dm = jnp.where(diff >= 0, jnp.exp(log_g[:, :, None, None] * jnp.maximum(diff, 0.0)), 0.0)
    dmat = jnp.stack([dm[0], jnp.swapaxes(dm[1], -1, -2)])
    xi = jnp.exp(log_g[:, :, None] * (n + 1.0))
    zeta = jnp.exp(log_g[:, :, None] * (C - 1.0 - n))
    xi = jnp.stack([xi[0], xi[1, :, ::-1]])
    zeta = jnp.stack([zeta[0], zeta[1, :, ::-1]])
    bc = lambda a: jnp.broadcast_to(a[..., None], a.shape + (LANES,))
    gch = jnp.broadcast_to(jnp.exp(log_g * C)[:, :, None, None], (2, RT_HEADS, 1, LANES))
    return dmat, bc(xi), bc(zeta), gch


def kernel(x, c, ctx, c_ctx, w_mod, b_mod, norm1_g, norm2_g, w_in, diff_lambda, diff_qk_g, diff_sub_g,
           ret_decay, ret_norm_g, na_qk_g, na_rpb, mla_q_norm_g, mla_kv_norm_g, w_uq, w_ukv, mla_qk_g,
           w_branch, w_gate, b_gate, w_o, peer_w_query, peer_sub_keys, peer_u, peer_v):
    B, S, D = x.shape
    depth = w_mod.shape[0]
    assert D == D_MODEL and ctx.shape[1] == CTX_LEN and S % TB_IN == 0 and (B * CTX_LEN) % TB_IN == 0
    assert S % (GRID_W * NA_GROUP) == 0 and S // GRID_W >= NA_UNION
    NL, NC = B * S, B * CTX_LEN
    NT = NL + NC

    mod_idx = lambda i: jnp.minimum(i * TB // S, B)
    mod_idx_in = lambda i: jnp.minimum(i * TB_IN // S, B)
    nlb, spb = NL // TB, S // TB
    rope_idx = lambda i: jnp.where(i < nlb, i % spb, spb)

    xa, xb, xb_row0 = x.reshape(NL, D), ctx.reshape(NC, D), 0
    c_all = jnp.zeros((16, D), F32).at[:B].set(c).at[B].set(c_ctx)
    cos_a, sin_a = _rope_table(S, DA_HEAD)
    cos_r, sin_r = _rope_table(S, RT_DK)
    na_plans, na_variants = _na_groups(S // GRID_W)
    tile2 = lambda g: jnp.tile(g.astype(F32), (1, 2))
    pad_qk = lambda g: jnp.pad(g.astype(F32), (0, 2 * LANES - g.shape[0])).reshape(1, 2 * LANES)

    w_in_bf = jnp.pad(w_in, ((0, 0), (0, 0), (0, IN_PAD - IN_WIDTH))).astype(BF16)
    wg_bf, wb_bf, wo_bf = w_gate.astype(BF16), w_branch.astype(BF16), w_o.astype(BF16)
    wqt = jnp.swapaxes(peer_w_query, 1, 2).astype(BF16)
    sk = peer_sub_keys.reshape(depth, PEER_HEADS * 2, PEER_NKEYS, PEER_DQ // 2).astype(BF16)
    u_bf = peer_u.astype(BF16)
    vt_bf = jnp.swapaxes(peer_v.reshape(depth, PEER_EXPERTS // PEER_EC, PEER_EC, D), 2, 3).astype(BF16)

    for l in range(depth):
        last = l == depth - 1
        lam_init = 0.8 - 0.6 * math.exp(-0.3 * l)
        n_tok = NL if last else NT

        wuq = w_uq[l].reshape(MLA_Q_LORA, MLA_HEADS, MLA_NOPE + MLA_ROPE)
        wuq = jnp.pad(wuq, ((0, 0), (0, 0), (0, 2 * LANES - MLA_NOPE - MLA_ROPE)))
        wuq = wuq.reshape(MLA_Q_LORA, MLA_HEADS * 2 * LANES).astype(BF16)
        wukv = w_ukv[l].reshape(MLA_KV_LORA, MLA_HEADS, MLA_NOPE + MLA_V)
        wuk = wukv[:, :, :MLA_NOPE].reshape(MLA_KV_LORA, MLA_HEADS * MLA_NOPE).astype(BF16)
        wuv = wukv[:, :, MLA_NOPE:].reshape(MLA_KV_LORA, MLA_HEADS * MLA_V).astype(BF16)
        gains = [tile2(diff_qk_g[l]), tile2(na_qk_g[l]), mla_q_norm_g[l].reshape(1, -1),
                 mla_kv_norm_g[l].reshape(1, -1), pad_qk(mla_qk_g[l, 0]), pad_qk(mla_qk_g[l, 1])]

        mod = _modulation(c_all, w_mod, b_mod, l).reshape(16, 6, D)
        h1, p = _in_proj(xa, xb, xb_row0, NL, mod, norm1_g[l].reshape(1, D), w_in_bf, l, NT, mod_idx_in)
        qa, ka, qr, kr, qn, kn, qm, km, vm = _prep(
            p, (cos_a, sin_a, cos_r, sin_r), gains, (wuq, wuk, wuv), NT, rope_idx)

        sub_g = diff_sub_g[l].reshape(1, -1)
        mla_scale = (MLA_NOPE + MLA_ROPE) ** -0.5
        ret_tabs = _retention_tables(ret_decay[l])
        common = dict(batch=B, seq=S)
        ya = _attention(qa, ka, p, (diff_lambda[l], sub_g), mode="diff", heads=DA_HEADS, dq=LANES,
                        v_col0=COL_DA_V, latent=True, lam_init=lam_init, name="diff_attn", **common)
        yb, yb_c = _retention(qr, kr, p, ret_tabs, ret_norm_g[l].reshape(1, -1), ctx_out=not last, **common)
        yc = _na_latent(qn, kn, p, _na_bias_table(na_rpb[l], na_variants), na_plans, **common)
        yd = _attention(qm, km, vm, (), mode="plain", heads=MLA_HEADS, dq=2 * LANES, v_col0=0,
                        latent=True, scale=mla_scale, name="latent_attn", **common)
        ys = [ya, yb, yc, yd]
        ys_ctx = ys
        if not last:
            ya_c = _attention(qa, ka, p, (diff_lambda[l], sub_g), mode="diff", heads=DA_HEADS, dq=LANES,
                              v_col0=COL_DA_V, latent=False, lam_init=lam_init, name="diff_attn_ctx", **common)
            yc_c = _attention(qn, kn, p, (), mode="pair", heads=NA_HEADS // 2, dq=LANES, v_col0=COL_NA_V,
                              latent=False, name="ctx_attn", **common)
            yd_c = _attention(qm, km, vm, (), mode="plain", heads=MLA_HEADS, dq=2 * LANES, v_col0=0,
                              latent=False, scale=mla_scale, name="latent_attn_ctx", **common)
            ys_ctx = [ya_c, yb_c, yc_c, yd_c]

        m = _merge1(h1, ys, ys_ctx, wg_bf, wb_bf, b_gate, l, n_tok)
        x_mid, h2t = _merge2(m, wo_bf, l, xa, xb, xb_row0, NL, mod, norm2_g[l].reshape(1, D), n_tok, mod_idx)
        s1, th, e1, e2 = _route(h2t, wqt, sk, l, n_tok)
        xa = xb = _experts(h2t, u_bf, vt_bf, l, s1, e1, th, e2, x_mid, mod, n_tok, mod_idx)
        xb_row0 = NL

    return xa.reshape(B, S, D)
```

```python
import functools
import math

import jax
import jax.numpy as jnp
import numpy as np
from jax import lax
from jax.experimental import pallas as pl
from jax.experimental.pallas import tpu as pltpu

F32 = jnp.float32
BF16 = jnp.bfloat16

D_MODEL = 2048
CTX_LEN = 256
GRID_W = 64
EPS = 1e-6
ROPE_BASE = 10000.0
N_BRANCH = 4
BRANCH_W = 512
DA_HEADS, DA_HEAD = 4, 64
RT_HEADS, RT_DK, RT_CHUNK = 4, 128, 128
NA_HEADS, NA_HEAD, NA_KH, NA_KW = 8, 64, 8, 16
MLA_HEADS, MLA_Q_LORA, MLA_KV_LORA, MLA_NOPE, MLA_ROPE, MLA_V = 4, 512, 256, 128, 64, 128
PEER_HEADS, PEER_NKEYS, PEER_DQ, PEER_TOPK = 8, 128, 256, 16
PEER_EXPERTS = PEER_NKEYS * PEER_NKEYS

LANES = 128
IN_WIDTH = 5952
IN_PAD = 6144
COL_DA_V = 1024 // LANES
COL_RT_V = 2560 // LANES
COL_RT_G = 3072 // LANES
COL_NA_V = 4608 // LANES
OFF_RT_Q, OFF_RT_K = 1536, 2048
OFF_NA_Q, OFF_NA_K = 3584, 4096
OFF_MLA_CQ, OFF_MLA_CKV, OFF_MLA_KR = 5120, 5632, 5888

TB = 512
TB_IN = 1024
TQ = 1024
TQ_SPLIT = 4
PEER_T = 512
PEER_EC = 512
NA_GROUP = 4
NA_UNION = NA_GROUP + NA_KH
NEG = -1e30
VMEM_LIMIT = 56 * 1024 * 1024


def _cp(*sem):
    return pltpu.CompilerParams(dimension_semantics=sem, vmem_limit_bytes=VMEM_LIMIT)


def _dot(a, b):
    return jnp.dot(a, b, preferred_element_type=F32)


def _dot_nt(a, b):
    return lax.dot_general(a, b, (((1,), (1,)), ((), ())), preferred_element_type=F32)


def _sigmoid(z):
    return 1.0 / (1.0 + jnp.exp(-z))


def _rms_rows(x, g):
    return x * lax.rsqrt(jnp.mean(x * x, axis=-1, keepdims=True) + EPS) * g


def _mod_kernel(c_ref, w_ref, b_ref, o_ref):
    c = c_ref[...]
    a = (c * _sigmoid(c)).astype(BF16)
    o_ref[...] = _dot(a, w_ref[0].astype(BF16)) + b_ref[0]


def _modulation(c_all, w_mod, b_mod, layer):
    rows, d = c_all.shape
    depth, _, n = w_mod.shape
    cb = 1024
    return pl.pallas_call(
        _mod_kernel,
        grid=(n // cb,),
        in_specs=[pl.BlockSpec((rows, d), lambda j: (0, 0)),
                  pl.BlockSpec((1, d, cb), lambda j: (layer, 0, j)),
                  pl.BlockSpec((1, 1, cb), lambda j: (layer, 0, j))],
        out_specs=pl.BlockSpec((rows, cb), lambda j: (0, j)),
        out_shape=jax.ShapeDtypeStruct((rows, n), F32),
        compiler_params=_cp("parallel"),
        name="adaln_mod",
    )(c_all, w_mod, b_mod.reshape(depth, 1, n))


def _split_rows(n_a, b_off):
    return (lambda i, *_: (jnp.minimum(i, n_a - 1), 0),
            lambda i, *_: (jnp.maximum(i - n_a, 0) + b_off, 0))


def _inproj_kernel(xa_ref, xb_ref, mod_ref, g_ref, w_ref, h_ref, p_ref, *, n_a):
    def prologue(x_ref):
        y = _rms_rows(x_ref[...], g_ref[...])
        h_ref[...] = (y * (1.0 + mod_ref[0, 1:2, :]) + mod_ref[0, 0:1, :]).astype(BF16)

    first = pl.program_id(1) == 0
    pl.when(first & (pl.program_id(0) < n_a))(functools.partial(prologue, xa_ref))
    pl.when(first & (pl.program_id(0) >= n_a))(functools.partial(prologue, xb_ref))
    p_ref[...] = _dot(h_ref[...], w_ref[0]).astype(BF16)


def _in_proj(xa, xb, xb_row0, n_lat, mod, g, w_bf, layer, n_rows, mod_idx):
    d = xa.shape[1]
    cb = 512
    amap, bmap = _split_rows(n_lat // TB_IN, xb_row0 // TB_IN)
    return pl.pallas_call(
        functools.partial(_inproj_kernel, n_a=n_lat // TB_IN),
        grid=(n_rows // TB_IN, IN_PAD // cb),
        in_specs=[pl.BlockSpec((TB_IN, d), amap),
                  pl.BlockSpec((TB_IN, d), bmap),
                  pl.BlockSpec((1, 6, d), lambda i, j: (mod_idx(i), 0, 0)),
                  pl.BlockSpec((1, d), lambda i, j: (0, 0)),
                  pl.BlockSpec((1, d, cb), lambda i, j: (layer, 0, j))],
        out_specs=[pl.BlockSpec((TB_IN, d), lambda i, j: (i, 0)),
                   pl.BlockSpec((TB_IN, cb), lambda i, j: (i, j))],
        out_shape=[jax.ShapeDtypeStruct((n_rows, d), BF16),
                   jax.ShapeDtypeStruct((n_rows, IN_PAD), BF16)],
        compiler_params=_cp("parallel", "arbitrary"),
        name="norm1_in_proj",
    )(xa, xb, mod, g, w_bf)


def _rot_partner(x, q):
    lane = lax.broadcasted_iota(jnp.int32, x.shape, 1)
    first = (lane & (2 * q - 1)) < q
    return jnp.where(first, pltpu.roll(x, LANES - q, 1), pltpu.roll(x, q, 1))


def _rope(x, cos, sin_signed, q):
    return x * cos + _rot_partner(x, q) * sin_signed


def _group_sumsq(x, gsz):
    x2 = x * x
    hi = x2.astype(BF16)
    lo = (x2 - hi.astype(F32)).astype(BF16)
    r = lax.broadcasted_iota(jnp.int32, (LANES, LANES), 0) // gsz
    c = lax.broadcasted_iota(jnp.int32, (LANES, LANES), 1) // gsz
    ones = jnp.where(r == c, 1.0, 0.0).astype(BF16)
    return _dot(hi, ones) + _dot(lo, ones)


def _prep_kernel(p_ref, cosa_ref, sina_ref, cosr_ref, sinr_ref, gda_ref, gna_ref, gq_ref, gkv_ref,
                 gmq_ref, gmk_ref, wuq_ref, wuk_ref, wuv_ref,
                 qa_ref, ka_ref, qr_ref, kr_ref, qn_ref, kn_ref, qm_ref, km_ref, vm_ref):
    cosa, sina = cosa_ref[...], sina_ref[...]
    cosr, sinr = cosr_ref[...], sinr_ref[...]
    for t in range(4):
        sl = slice(t * LANES, (t + 1) * LANES)
        for off, gi, oref, scale in ((0, 0, qa_ref, DA_HEAD ** -0.5), (512, 1, ka_ref, 1.0)):
            x = p_ref[:, off + t * LANES: off + (t + 1) * LANES].astype(F32)
            y = x * lax.rsqrt(_group_sumsq(x, DA_HEAD) * (1.0 / DA_HEAD) + EPS) * gda_ref[gi:gi + 1, :]
            oref[:, sl] = (_rope(y, cosa, sina, DA_HEAD // 4) * scale).astype(BF16)
        x = p_ref[:, OFF_RT_Q + t * LANES: OFF_RT_Q + (t + 1) * LANES].astype(F32)
        qr_ref[:, sl] = _rope(x, cosr, sinr, RT_DK // 4).astype(BF16)
        x = p_ref[:, OFF_RT_K + t * LANES: OFF_RT_K + (t + 1) * LANES].astype(F32) * (RT_DK ** -0.5)
        kr_ref[:, sl] = _rope(x, cosr, sinr, RT_DK // 4).astype(BF16)
        for off, gi, oref, scale in ((OFF_NA_Q, 0, qn_ref, NA_HEAD ** -0.5), (OFF_NA_K, 1, kn_ref, 1.0)):
            x = p_ref[:, off + t * LANES: off + (t + 1) * LANES].astype(F32)
            y = x * lax.rsqrt(_group_sumsq(x, NA_HEAD) * (1.0 / NA_HEAD) + EPS) * gna_ref[gi:gi + 1, :]
            oref[:, sl] = (y * scale).astype(BF16)
    cq = p_ref[:, OFF_MLA_CQ:OFF_MLA_CQ + MLA_Q_LORA].astype(F32)
    q = _dot(_rms_rows(cq, gq_ref[...]).astype(BF16), wuq_ref[...])
    ckv = p_ref[:, OFF_MLA_CKV:OFF_MLA_CKV + MLA_KV_LORA].astype(F32)
    ckv_n = _rms_rows(ckv, gkv_ref[...]).astype(BF16)
    k_nope = _dot(ckv_n, wuk_ref[...])
    vm_ref[...] = _dot(ckv_n, wuv_ref[...]).astype(BF16)
    kr = p_ref[:, OFF_MLA_KR:OFF_MLA_KR + LANES].astype(F32)
    kr_ss = jnp.sum(kr * kr, axis=-1, keepdims=True)
    inv_n = 1.0 / (MLA_NOPE + MLA_ROPE)
    for h in range(MLA_HEADS):
        a, b = h * 2 * LANES, h * 2 * LANES + LANES
        q0, q1 = q[:, a:b], q[:, b:b + LANES]
        r = lax.rsqrt((jnp.sum(q0 * q0, axis=-1, keepdims=True)
                       + jnp.sum(q1 * q1, axis=-1, keepdims=True)) * inv_n + EPS)
        qm_ref[:, a:b] = (q0 * r * gmq_ref[:, 0:LANES]).astype(BF16)
        qm_ref[:, b:b + LANES] = _rope(q1 * r * gmq_ref[:, LANES:2 * LANES], cosa, sina,
                                       MLA_ROPE // 4).astype(BF16)
        k0 = k_nope[:, h * LANES:(h + 1) * LANES]
        r = lax.rsqrt((jnp.sum(k0 * k0, axis=-1, keepdims=True) + kr_ss) * inv_n + EPS)
        km_ref[:, a:b] = (k0 * r * gmk_ref[:, 0:LANES]).astype(BF16)
        km_ref[:, b:b + LANES] = _rope(kr * r * gmk_ref[:, LANES:2 * LANES], cosa, sina,
                                       MLA_ROPE // 4).astype(BF16)


def _prep(p, tabs, gains, weights, n_rows, rope_idx):
    row = lambda w: pl.BlockSpec((TB, w), lambda i: (i, 0))
    tab = pl.BlockSpec((TB, LANES), lambda i: (rope_idx(i), 0))
    full = lambda a: pl.BlockSpec(a.shape, lambda i: (0,) * a.ndim)
    outs = [512] * 6 + [1024, 1024, 512]
    return pl.pallas_call(
        _prep_kernel,
        grid=(n_rows // TB,),
        in_specs=[row(IN_PAD)] + [tab] * 4 + [full(a) for a in gains] + [full(a) for a in weights],
        out_specs=[row(w) for w in outs],
        out_shape=[jax.ShapeDtypeStruct((n_rows, w), BF16) for w in outs],
        compiler_params=_cp("parallel"),
        name="mixer_prep",
    )(p, *tabs, *gains, *weights)


def _softmax_parts(scores):
    m = functools.reduce(jnp.maximum, [jnp.max(s, axis=-1, keepdims=True) for s in scores])
    es = [jnp.exp(s - m) for s in scores]
    z = functools.reduce(jnp.add, [jnp.sum(e, axis=-1, keepdims=True) for e in es])
    return es, z


def _attn_kernel(*refs, mode, nseg, scale, lam_init, nsplit):
    q_ref = refs[0]
    ks = [refs[1 + 2 * s] for s in range(nseg)]
    vs = [refs[2 + 2 * s] for s in range(nseg)]
    o_ref = refs[-1]
    rows = q_ref.shape[0] // nsplit

    def scores(g):
        q = q_ref[g * rows:(g + 1) * rows, :]
        if mode == "plain":
            return [[_dot_nt(q, k[...]) * scale for k in ks]]
        lane = lax.broadcasted_iota(jnp.int32, q.shape, 1)
        halves = [jnp.where(lane < LANES // 2, q, jnp.zeros_like(q)),
                  jnp.where(lane >= LANES // 2, q, jnp.zeros_like(q))]
        return [[_dot_nt(qh, k[...]) for k in ks] for qh in halves]

    def weighted(es):
        return functools.reduce(jnp.add, [_dot(e.astype(BF16), v[...]) for e, v in zip(es, vs)])

    if mode == "diff":
        lam_ref, subg_ref = refs[1 + 2 * nseg], refs[2 + 2 * nseg]
        lv = lam_ref[...]
        lam = (jnp.exp(jnp.sum(lv[0:1] * lv[1:2], axis=-1, keepdims=True))
               - jnp.exp(jnp.sum(lv[2:3] * lv[3:4], axis=-1, keepdims=True)) + lam_init)

    def finish(g, sc):
        parts = [_softmax_parts(s) for s in sc]
        if mode == "plain":
            (es, z), = parts
            y = weighted(es) / z
        elif mode == "pair":
            outs = [weighted(es) / z for es, z in parts]
            lo = lax.broadcasted_iota(jnp.int32, outs[0].shape, 1) < LANES // 2
            y = jnp.where(lo, outs[0], outs[1])
        else:
            (e0, z0), (e1, z1) = parts
            w0, w1 = 1.0 / z0, lam / z1
            y = weighted([a * w0 - b * w1 for a, b in zip(e0, e1)])
            y = _rms_rows(y, subg_ref[...]) * (1.0 - lam_init)
        o_ref[g * rows:(g + 1) * rows, :] = y.astype(o_ref.dtype)

    sc = [scores(g) for g in range(nsplit)]
    for g in range(nsplit):
        finish(g, sc[g])


def _attention(q_arr, k_arr, v_arr, extra, *, mode, heads, dq, v_col0, batch, seq, latent, scale=1.0,
               lam_init=0.0, name):
    nl = batch * seq
    cblk = nl // CTX_LEN
    if latent:
        nq = seq // TQ
        grid = (batch, heads, nq)
        q_spec = pl.BlockSpec((TQ, dq), lambda b, h, j: (b * nq + j, h))
        segs = [(CTX_LEN, lambda b, h, j: (cblk + b, h), lambda b, h, j: (cblk + b, v_col0 + h)),
                (seq, lambda b, h, j: (b, h), lambda b, h, j: (b, v_col0 + h))]
        o_spec = pl.BlockSpec((TQ, LANES), lambda b, h, j: (b * nq + j, h))
        rows = nl
    else:
        grid = (batch, heads, 1)
        q_spec = pl.BlockSpec((CTX_LEN, dq), lambda b, h, j: (cblk + b, h))
        segs = [(CTX_LEN, lambda b, h, j: (cblk + b, h), lambda b, h, j: (cblk + b, v_col0 + h))]
        o_spec = pl.BlockSpec((CTX_LEN, LANES), lambda b, h, j: (b, h))
        rows = batch * CTX_LEN
    in_specs, args = [q_spec], [q_arr]
    for n, kmap, vmap in segs:
        in_specs += [pl.BlockSpec((n, dq), kmap), pl.BlockSpec((n, LANES), vmap)]
        args += [k_arr, v_arr]
    for a in extra:
        in_specs.append(pl.BlockSpec(a.shape, lambda b, h, j: (0, 0)))
        args.append(a)
    return pl.pallas_call(
        functools.partial(_attn_kernel, mode=mode, nseg=len(segs), scale=scale, lam_init=lam_init,
                          nsplit=TQ_SPLIT if latent else 1),
        grid=grid, in_specs=in_specs, out_specs=o_spec,
        out_shape=jax.ShapeDtypeStruct((rows, heads * LANES), BF16),
        compiler_params=_cp("parallel", "parallel", "arbitrary"),
        name=name,
    )(*args)


def _ret_kernel(ql_ref, qc_ref, kl_ref, kc_ref, vl_ref, vc_ref, gl_ref, gc_ref, dmat_ref, xi_ref, zeta_ref,
                gch_ref, ng_ref, yl_ref, yc_ref, accl_ref, accc_ref, *, n_lat, n_ctx, ctx_out):
    C = RT_CHUNK

    def chunk(q, k, v, R, d):
        s = _dot_nt(q, k) * dmat_ref[d, 0]
        inner = _dot(s.astype(BF16), v)
        cross = _dot(q, R.astype(BF16)) * xi_ref[d, 0]
        kz = (k.astype(F32) * zeta_ref[d, 0]).T.astype(BF16)
        return inner + cross, gch_ref[d, 0] * R + _dot(kz, v)

    for d in range(2):
        R = jnp.zeros((RT_DK, RT_DK), F32)
        for j in range(n_ctx):
            c = j if d == 0 else n_ctx - 1 - j
            sl = pl.ds(c * C, C)
            o, R = chunk(qc_ref[sl, :], kc_ref[sl, :], vc_ref[sl, :], R, d)
            if ctx_out:
                if d == 0:
                    accc_ref[sl, :] = o
                else:
                    accc_ref[sl, :] += o

        for j in range(n_lat):
            c = j if d == 0 else n_lat - 1 - j
            sl = pl.ds(c * C, C)
            o, R = chunk(ql_ref[sl, :], kl_ref[sl, :], vl_ref[sl, :], R, d)
            if d == 0:
                accl_ref[sl, :] = o
            else:
                accl_ref[sl, :] += o

    def post(acc_ref, g_ref, y_ref):
        g = g_ref[...].astype(F32)
        y_ref[...] = (_rms_rows(acc_ref[...], ng_ref[...]) * (g * _sigmoid(g))).astype(y_ref.dtype)

    post(accl_ref, gl_ref, yl_ref)
    if ctx_out:
        post(accc_ref, gc_ref, yc_ref)
    else:
        yc_ref[...] = jnp.zeros_like(yc_ref)


def _retention(qr, kr, p, tabs, norm_g, *, batch, seq, ctx_out):
    nl = batch * seq
    cblk = nl // CTX_LEN
    lat = lambda c0: pl.BlockSpec((seq, LANES), lambda b, h: (b, c0 + h))
    ctx = lambda c0: pl.BlockSpec((CTX_LEN, LANES), lambda b, h: (cblk + b, c0 + h))
    tab = lambda a: pl.BlockSpec((2, 1) + a.shape[2:], lambda b, h: (0, h, 0, 0))
    return pl.pallas_call(
        functools.partial(_ret_kernel, n_lat=seq // RT_CHUNK, n_ctx=CTX_LEN // RT_CHUNK, ctx_out=ctx_out),
        grid=(batch, RT_HEADS),
        in_specs=[lat(0), ctx(0), lat(0), ctx(0), lat(COL_RT_V), ctx(COL_RT_V), lat(COL_RT_G), ctx(COL_RT_G)]
        + [tab(a) for a in tabs] + [pl.BlockSpec((1, LANES), lambda b, h: (0, 0))],
        out_specs=[pl.BlockSpec((seq, LANES), lambda b, h: (b, h)),
                   pl.BlockSpec((CTX_LEN, LANES), lambda b, h: (b, h))],
        out_shape=[jax.ShapeDtypeStruct((nl, RT_HEADS * LANES), BF16),
                   jax.ShapeDtypeStruct((batch * CTX_LEN, RT_HEADS * LANES), BF16)],
        scratch_shapes=[pltpu.VMEM((seq, LANES), F32), pltpu.VMEM((CTX_LEN, LANES), F32)],
        compiler_params=_cp("parallel", "parallel"),
        name="retention",
    )(qr, qr, kr, kr, p, p, p, p, *tabs, norm_g)


def _na_groups(rows):
    plans, variants = [], []
    for gi in range(rows // NA_GROUP):
        r = gi * NA_GROUP + np.arange(NA_GROUP)
        u0 = int(np.clip(gi * NA_GROUP - NA_KH // 2, 0, rows - NA_UNION))
        rs = np.clip(r - NA_KH // 2, 0, rows - NA_KH)
        krow = u0 + np.arange(NA_UNION)
        valid = (krow[None, :] >= rs[:, None]) & (krow[None, :] < rs[:, None] + NA_KH)
        dr = np.where(valid, krow[None, :] - r[:, None] + (NA_KH - 1), 0)
        key = (valid.tobytes(), dr.tobytes())
        for v, (k, _, _) in enumerate(variants):
            if k == key:
                break
        else:
            v = len(variants)
            variants.append((key, valid, dr))
        plans.append((u0, v))
    return tuple(plans), [(valid, dr) for _, valid, dr in variants]


def _na_kernel(q_ref, kl_ref, kc_ref, vl_ref, vc_ref, bias_ref, o_ref, *, plans):
    gq, gk = NA_GROUP * GRID_W, NA_UNION * GRID_W
    kc, vc = kc_ref[...], vc_ref[...]

    def scores(gi):
        u0, var = plans[gi]
        q = q_ref[gi * gq:(gi + 1) * gq, :]
        kw = kl_ref[u0 * GRID_W:u0 * GRID_W + gk, :]
        lane = lax.broadcasted_iota(jnp.int32, q.shape, 1)
        out = []
        for hh in range(2):
            keep = (lane < LANES // 2) if hh == 0 else (lane >= LANES // 2)
            qh = jnp.where(keep, q, jnp.zeros_like(q))
            out.append([_dot_nt(qh, kw) + bias_ref[hh, var], _dot_nt(qh, kc)])
        return out

    def finish(gi, sc):
        u0, _ = plans[gi]
        vw = vl_ref[u0 * GRID_W:u0 * GRID_W + gk, :]
        outs = []
        for s in sc:
            (el, ec), z = _softmax_parts(s)
            outs.append((_dot(el.astype(BF16), vw) + _dot(ec.astype(BF16), vc)) / z)
        lo = lax.broadcasted_iota(jnp.int32, outs[0].shape, 1) < LANES // 2
        o_ref[gi * gq:(gi + 1) * gq, :] = jnp.where(lo, outs[0], outs[1]).astype(o_ref.dtype)

    pending = scores(0)
    for gi in range(len(plans)):
        nxt = scores(gi + 1) if gi + 1 < len(plans) else None
        finish(gi, pending)
        pending = nxt


def _na_latent(qn, kn, p, bias, plans, *, batch, seq):
    nl = batch * seq
    cblk = nl // CTX_LEN
    hp = NA_HEADS // 2
    return pl.pallas_call(
        functools.partial(_na_kernel, plans=plans),
        grid=(batch, hp),
        in_specs=[pl.BlockSpec((seq, LANES), lambda b, h: (b, h)),
                  pl.BlockSpec((seq, LANES), lambda b, h: (b, h)),
                  pl.BlockSpec((CTX_LEN, LANES), lambda b, h: (cblk + b, h)),
                  pl.BlockSpec((seq, LANES), lambda b, h: (b, COL_NA_V + h)),
                  pl.BlockSpec((CTX_LEN, LANES), lambda b, h: (cblk + b, COL_NA_V + h)),
                  pl.BlockSpec((2,) + bias.shape[1:], lambda b, h: (h, 0, 0, 0))],
        out_specs=pl.BlockSpec((seq, LANES), lambda b, h: (b, h)),
        out_shape=jax.ShapeDtypeStruct((nl, hp * LANES), BF16),
        compiler_params=_cp("parallel", "parallel"),
        name="neighbourhood_attn",
    )(qn, kn, kn, p, p, bias)


def _merge1_kernel(h_ref, *refs, n_a):
    lat, ctx = refs[:N_BRANCH], refs[N_BRANCH:2 * N_BRANCH]
    wg_ref, wb_ref, bg_ref, m_ref = refs[2 * N_BRANCH:]

    def body(y_refs):
        h = h_ref[...]
        acc = None
        for n, y_ref in enumerate(y_refs):
            gate = _sigmoid(_dot(h, wg_ref[0, n]) + bg_ref[0, n:n + 1, :])
            term = gate * _dot(y_ref[...], wb_ref[0, n])
            acc = term if acc is None else acc + term
        m_ref[...] = acc.astype(BF16)

    pl.when(pl.program_id(1) < n_a)(functools.partial(body, lat))
    pl.when(pl.program_id(1) >= n_a)(functools.partial(body, ctx))


def _merge1(h, ys_lat, ys_ctx, wg, wb, bg, layer, n_rows):
    d = h.shape[1]
    cb = 512
    n_a = ys_lat[0].shape[0] // TB
    lat_map = lambda j, i: (jnp.minimum(i, n_a - 1), 0)
    ctx_map = lambda j, i: (jnp.maximum(i - n_a, 0), 0)
    return pl.pallas_call(
        functools.partial(_merge1_kernel, n_a=n_a),
        grid=(d // cb, n_rows // TB),
        in_specs=[pl.BlockSpec((TB, d), lambda j, i: (i, 0))]
        + [pl.BlockSpec((TB, BRANCH_W), lat_map)] * N_BRANCH
        + [pl.BlockSpec((TB, BRANCH_W), ctx_map)] * N_BRANCH
        + [pl.BlockSpec((1, N_BRANCH, d, cb), lambda j, i: (layer, 0, 0, j)),
           pl.BlockSpec((1, N_BRANCH, BRANCH_W, cb), lambda j, i: (layer, 0, 0, j)),
           pl.BlockSpec((1, N_BRANCH, cb), lambda j, i: (layer, 0, j))],
        out_specs=pl.BlockSpec((TB, cb), lambda j, i: (i, j)),
        out_shape=jax.ShapeDtypeStruct((n_rows, d), BF16),
        compiler_params=_cp("parallel", "parallel"),
        name="gated_branch_sum",
    )(h, *ys_lat, *ys_ctx, wg, wb, bg)


def _merge2_kernel(m_ref, wo_ref, xa_ref, xb_ref, mod_ref, g_ref, xo_ref, h2t_ref, *, n_a):
    def body(x_ref):
        half = x_ref.shape[0] // 2
        ys = [_dot(m_ref[r * half:(r + 1) * half, :], wo_ref[0]) for r in range(2)]
        for r in range(2):
            rs = slice(r * half, (r + 1) * half)
            x = x_ref[rs, :] + mod_ref[0, 2:3, :] * ys[r]
            xo_ref[rs, :] = x
            h2 = _rms_rows(x, g_ref[...]) * (1.0 + mod_ref[0, 4:5, :]) + mod_ref[0, 3:4, :]
            h2t_ref[:, rs] = h2.T.astype(BF16)

    pl.when(pl.program_id(0) < n_a)(functools.partial(body, xa_ref))
    pl.when(pl.program_id(0) >= n_a)(functools.partial(body, xb_ref))


def _merge2(m, wo, layer, xa, xb, xb_row0, n_lat, mod, g, n_rows, mod_idx):
    d = xa.shape[1]
    amap, bmap = _split_rows(n_lat // TB, xb_row0 // TB)
    return pl.pallas_call(
        functools.partial(_merge2_kernel, n_a=n_lat // TB),
        grid=(n_rows // TB,),
        in_specs=[pl.BlockSpec((TB, d), lambda i: (i, 0)),
                  pl.BlockSpec((1, d, d), lambda i: (layer, 0, 0)),
                  pl.BlockSpec((TB, d), amap),
                  pl.BlockSpec((TB, d), bmap),
                  pl.BlockSpec((1, 6, d), lambda i: (mod_idx(i), 0, 0)),
                  pl.BlockSpec((1, d), lambda i: (0, 0))],
        out_specs=[pl.BlockSpec((TB, d), lambda i: (i, 0)),
                   pl.BlockSpec((d, TB), lambda i: (0, i))],
        out_shape=[jax.ShapeDtypeStruct((n_rows, d), F32),
                   jax.ShapeDtypeStruct((d, n_rows), BF16)],
        compiler_params=_cp("parallel"),
        name="out_proj_residual_norm2",
    )(m, wo, xa, xb, mod, g)


def _merge_exchange(n):
    comps, p = [], 1
    while p < n:
        k = p
        while k >= 1:
            for j in range(k % p, n - k, 2 * k):
                for i in range(min(k, n - j - k)):
                    if (i + j) // (2 * p) == (i + j + k) // (2 * p):
                        comps.append((i + j, i + j + k))
            k //= 2
        p *= 2
    return comps


def _top_values(x, k):
    sub = 8
    cols = [x[v * sub:(v + 1) * sub, :] for v in range(x.shape[0] // sub)]
    for i, j in _merge_exchange(len(cols)):
        cols[i], cols[j] = jnp.maximum(cols[i], cols[j]), jnp.minimum(cols[i], cols[j])
    tops = []
    for it in range(k):
        m = jnp.max(cols[0], axis=0, keepdims=True)
        tops.append(m)
        if it == k - 1:
            break
        hit = cols[0] == m
        depth = min(len(cols), k - it)
        for d in range(depth - 1):
            cols[d] = jnp.where(hit, cols[d + 1], cols[d])
        if depth == len(cols):
            cols[depth - 1] = jnp.where(hit, -jnp.inf, cols[depth - 1])
    return tops


def _route_kernel(h2t_ref, wqt_ref, sk_ref, s1_ref, th_ref, e1_ref, e2_ref):
    qt = _dot(wqt_ref[0], h2t_ref[...]).astype(BF16)
    nk, k, half = PEER_NKEYS, PEER_TOPK, PEER_TOPK // 2
    for h in range(PEER_HEADS):
        s1 = _dot(sk_ref[0, 2 * h], qt[(2 * h) * nk:(2 * h + 1) * nk, :])
        s2 = _dot(sk_ref[0, 2 * h + 1], qt[(2 * h + 1) * nk:(2 * h + 2) * nk, :])
        t1, t2 = _top_values(s1, k), _top_values(s2, k)
        t1_hi = jnp.concatenate(t1[half:], axis=0)
        t2_lo, t2_hi = jnp.concatenate(t2[:half], axis=0), jnp.concatenate(t2[half:], axis=0)
        cand = jnp.concatenate([t1[a] + t2_lo for a in range(half)] + [t1[0] + t2_hi, t1_hi + t2[0]], axis=0)
        sel = _top_values(cand, k)
        tau = sel[-1]
        z = functools.reduce(jnp.add, [jnp.exp(c - sel[0]) for c in sel])
        th = jnp.full(s2.shape, jnp.inf, F32)
        for a in range(half):
            th = jnp.where(t1[a] + s2 >= tau, t1[a], th)
        th0 = jnp.full(tau.shape, jnp.inf, F32)
        for a in range(half, k):
            th0 = jnp.where(t1[a] + t2[0] >= tau, t1[a], th0)
        th = jnp.where(s2 == t2[0], jnp.minimum(th, th0), th)
        s1_ref[h] = s1
        th_ref[h] = th
        e1_ref[h] = jnp.exp(s1 - t1[0]) / z * 0.5
        e2_ref[h] = jnp.exp(s2 - t2[0])


def _route(h2t, wqt, sk, layer, n_rows):
    d = h2t.shape[0]
    t = 256
    big = pl.BlockSpec((PEER_HEADS, PEER_NKEYS, t), lambda i: (0, 0, i))
    return pl.pallas_call(
        _route_kernel,
        grid=(n_rows // t,),
        in_specs=[pl.BlockSpec((d, t), lambda i: (0, i)),
                  pl.BlockSpec((1,) + wqt.shape[1:], lambda i: (layer, 0, 0)),
                  pl.BlockSpec((1,) + sk.shape[1:], lambda i: (layer, 0, 0, 0))],
        out_specs=[big] * 4,
        out_shape=[jax.ShapeDtypeStruct((PEER_HEADS, PEER_NKEYS, n_rows), F32)] * 4,
        compiler_params=_cp("parallel"),
        name="peer_route",
    )(h2t, wqt, sk)


def _experts_kernel(h2t_ref, u_ref, vt_ref, s1_ref, e1_ref, th_ref, e2_ref, x_ref, mod_ref, xo_ref,
                    a_even, a_odd, w_even, w_odd, out_ref, *, n_items):
    s = pl.program_id(0)
    nk = PEER_NKEYS
    ni = PEER_EC // nk
    nchunk = PEER_EXPERTS // PEER_EC

    @pl.when(s == 0)
    def _():
        a_odd[...] = jnp.zeros_like(a_odd)
        w_odd[...] = jnp.zeros_like(w_odd)

    @pl.when((s == 0) | ((s >= 2) & ((s - 2) % nchunk == 0)))
    def _():
        out_ref[...] = jnp.zeros_like(out_ref)

    gi = 4
    assert nchunk % 2 == 0 and (2 * ni) % 8 == 0

    def gate_block(a_ref, w_ref, r0, lt, jg, ig):
        ls = slice(lt * LANES, (lt + 1) * LANES)
        js = slice(jg * 16, (jg + 1) * 16)
        gs = [None] * gi
        for h in range(PEER_HEADS):
            th, e2 = th_ref[h, js, ls], e2_ref[h, js, ls]
            for ii in range(gi):
                r = slice(r0 + ig * gi + ii, r0 + ig * gi + ii + 1)
                term = jnp.where(s1_ref[h, r, ls] >= th, e2 * e1_ref[h, r, ls], 0.0)
                gs[ii] = term if gs[ii] is None else gs[ii] + term
        for ii in range(gi):
            r = ig * gi + ii
            rows = slice(r * nk + jg * 16, r * nk + (jg + 1) * 16)
            a = a_ref[rows, ls]
            act = a * (1.0 + lax.erf(a * (2.0 ** -0.5)))
            w_ref[rows, ls] = (gs[ii] * act).astype(BF16)

    def step(a_new, a_old, w_new, w_old, step_parity):
        r0 = (((step_parity - 1) % 2) * ni) % 8
        d, t = out_ref.shape
        n_out, n_pre = 8, 2 * (PEER_EC // 256)

        def out_unit(m):
            rs = slice(m * d // n_out, (m + 1) * d // n_out)
            out_ref[rs, :] += _dot(vt_ref[0, 0, rs, :], w_old[...])

        def pre_unit(q):
            rs = slice((q // 2) * 256, (q // 2 + 1) * 256)
            cs = slice((q % 2) * t // 2, (q % 2 + 1) * t // 2)
            a_new[rs, cs] = _dot(u_ref[0, rs, :], h2t_ref[:, cs])

        blocks = [(lt, jg, ig) for lt in range(t // LANES) for jg in range(nk // 16) for ig in range(ni // gi)]
        units = ([functools.partial(out_unit, m) for m in range(n_out)]
                 + [functools.partial(pre_unit, q) for q in range(n_pre)])
        for k, unit in enumerate(units):
            unit()
            for blk in blocks[k * len(blocks) // len(units):(k + 1) * len(blocks) // len(units)]:
                gate_block(a_old, w_new, r0, *blk)

    @pl.when(s % 2 == 0)
    def _():
        step(a_even, a_odd, w_even, w_odd, 0)

    @pl.when(s % 2 == 1)
    def _():
        step(a_odd, a_even, w_odd, w_even, 1)

    @pl.when((s >= 2) & ((s - 2) % nchunk == nchunk - 1))
    def _():
        xo_ref[...] = x_ref[...] + mod_ref[0, 5:6, :] * out_ref[...].T


def _experts(h2t, u, vt, layer, s1, e1, th, e2, x, mod, n_rows, mod_idx):
    d = h2t.shape[0]
    t, ec = PEER_T, PEER_EC
    assert t == TB
    nchunk = PEER_EXPERTS // ec
    n_items = (n_rows // t) * nchunk
    item = lambda s, lag: jnp.clip(s - lag, 0, n_items - 1)
    res = lambda a: pl.BlockSpec(a.shape[:2] + (t,), lambda s: (0, 0, item(s, 1) // nchunk))
    ni = ec // PEER_NKEYS
    row = lambda a: pl.BlockSpec((PEER_HEADS, 8, t),
                                 lambda s: (0, (item(s, 1) % nchunk) * ni // 8, item(s, 1) // nchunk))
    return pl.pallas_call(
        functools.partial(_experts_kernel, n_items=n_items),
        grid=(n_items + 2,),
        in_specs=[pl.BlockSpec((d, t), lambda s: (0, item(s, 0) // nchunk)),
                  pl.BlockSpec((1, ec, d), lambda s: (layer, item(s, 0) % nchunk, 0)),
                  pl.BlockSpec((1, 1, d, ec), lambda s: (layer, item(s, 2) % nchunk, 0, 0)),
                  row(s1), row(e1), res(th), res(e2),
                  pl.BlockSpec((t, d), lambda s: (item(s, 2) // nchunk, 0)),
                  pl.BlockSpec((1, 6, d), lambda s: (mod_idx(item(s, 2) // nchunk), 0, 0))],
        out_specs=pl.BlockSpec((t, d), lambda s: (item(s, 2) // nchunk, 0)),
        out_shape=jax.ShapeDtypeStruct((n_rows, d), F32),
        scratch_shapes=[pltpu.VMEM((ec, t), F32), pltpu.VMEM((ec, t), F32),
                        pltpu.VMEM((ec, t), BF16), pltpu.VMEM((ec, t), BF16),
                        pltpu.VMEM((d, t), F32)],
        compiler_params=_cp("arbitrary"),
        name="peer_experts",
    )(h2t, u, vt, s1, e1, th, e2, x, mod)


def _rope_table(n_tok, dim):
    m = dim // 2
    inv = ROPE_BASE ** (-jnp.arange(0, m, 2, dtype=F32) / m)
    t = jnp.arange(n_tok)
    row = (t // GRID_W).astype(F32)
    col = (t % GRID_W).astype(F32)
    ar, ac = row[:, None] * inv, col[:, None] * inv
    ang = jnp.concatenate([ar, ar, ac, ac], axis=-1)
    sign = np.where((np.arange(dim) % (dim // 2)) < dim // 4, -1.0, 1.0).astype(np.float32)
    reps = LANES // dim
    cos = jnp.tile(jnp.cos(ang), (1, reps))
    sin = jnp.tile(jnp.sin(ang) * sign, (1, reps))
    return (jnp.concatenate([cos, jnp.ones((TB, LANES), F32)], axis=0),
            jnp.concatenate([sin, jnp.zeros((TB, LANES), F32)], axis=0))


def _na_bias_table(rpb, variants):
    cols = np.arange(GRID_W)
    start = np.clip(cols - NA_KW // 2, 0, GRID_W - NA_KW)
    valid_c = (cols[None, :] >= start[:, None]) & (cols[None, :] < start[:, None] + NA_KW)
    dc = np.clip(cols[None, :] - cols[:, None] + (NA_KW - 1), 0, 2 * NA_KW - 2)
    g = rpb.astype(F32)[:, :, dc]
    tabs = []
    for valid_r, dr in variants:
        ok = valid_r[None, :, :, None, None] & valid_c[None, None, None]
        t = jnp.where(ok, g[:, dr], NEG).transpose(0, 1, 3, 2, 4)
        tabs.append(t.reshape(NA_HEADS, NA_GROUP * GRID_W, NA_UNION * GRID_W))
    return jnp.stack(tabs, axis=1)


def _retention_tables(decay_logit):
    C = RT_CHUNK
    log_g = jax.nn.log_sigmoid(decay_logit.astype(F32))
    n = jnp.arange(C, dtype=F32)
    diff = n[:, None] - n[None, :]
    dm = jnp.where(diff >= 0, jnp.exp(log_g[:, :, None, None] * jnp.maximum(diff, 0.0)), 0.0)
    dmat = jnp.stack([dm[0], jnp.swapaxes(dm[1], -1, -2)])
    xi = jnp.exp(log_g[:, :, None] * (n + 1.0))
    zeta = jnp.exp(log_g[:, :, None] * (C - 1.0 - n))
    xi = jnp.stack([xi[0], xi[1, :, ::-1]])
    zeta = jnp.stack([zeta[0], zeta[1, :, ::-1]])
    bc = lambda a: jnp.broadcast_to(a[..., None], a.shape + (LANES,))
    gch = jnp.broadcast_to(jnp.exp(log_g * C)[:, :, None, None], (2, RT_HEADS, 1, LANES))
    return dmat, bc(xi), bc(zeta), gch


def kernel(x, c, ctx, c_ctx, w_mod, b_mod, norm1_g, norm2_g, w_in, diff_lambda, diff_qk_g, diff_sub_g,
           ret_decay, ret_norm_g, na_qk_g, na_rpb, mla_q_norm_g, mla_kv_norm_g, w_uq, w_ukv, mla_qk_g,
           w_branch, w_gate, b_gate, w_o, peer_w_query, peer_sub_keys, peer_u, peer_v):
    B, S, D = x.shape
    depth = w_mod.shape[0]
    assert D == D_MODEL and ctx.shape[1] == CTX_LEN and S % TB_IN == 0 and (B * CTX_LEN) % TB_IN == 0
    assert S % (GRID_W * NA_GROUP) == 0 and S // GRID_W >= NA_UNION
    NL, NC = B * S, B * CTX_LEN
    NT = NL + NC

    mod_idx = lambda i: jnp.minimum(i * TB // S, B)
    mod_idx_in = lambda i: jnp.minimum(i * TB_IN // S, B)
    nlb, spb = NL // TB, S // TB
    rope_idx = lambda i: jnp.where(i < nlb, i % spb, spb)

    xa, xb, xb_row0 = x.reshape(NL, D), ctx.reshape(NC, D), 0
    c_all = jnp.zeros((16, D), F32).at[:B].set(c).at[B].set(c_ctx)
    cos_a, sin_a = _rope_table(S, DA_HEAD)
    cos_r, sin_r = _rope_table(S, RT_DK)
    na_plans, na_variants = _na_groups(S // GRID_W)
    tile2 = lambda g: jnp.tile(g.astype(F32), (1, 2))
    pad_qk = lambda g: jnp.pad(g.astype(F32), (0, 2 * LANES - g.shape[0])).reshape(1, 2 * LANES)

    w_in_bf = jnp.pad(w_in, ((0, 0), (0, 0), (0, IN_PAD - IN_WIDTH))).astype(BF16)
    wg_bf, wb_bf, wo_bf = w_gate.astype(BF16), w_branch.astype(BF16), w_o.astype(BF16)
    wqt = jnp.swapaxes(peer_w_query, 1, 2).astype(BF16)
    sk = peer_sub_keys.reshape(depth, PEER_HEADS * 2, PEER_NKEYS, PEER_DQ // 2).astype(BF16)
    u_bf = peer_u.astype(BF16)
    vt_bf = jnp.swapaxes(peer_v.reshape(depth, PEER_EXPERTS // PEER_EC, PEER_EC, D), 2, 3).astype(BF16)

    for l in range(depth):
        last = l == depth - 1
        lam_init = 0.8 - 0.6 * math.exp(-0.3 * l)
        n_tok = NL if last else NT

        wuq = w_uq[l].reshape(MLA_Q_LORA, MLA_HEADS, MLA_NOPE + MLA_ROPE)
        wuq = jnp.pad(wuq, ((0, 0), (0, 0), (0, 2 * LANES - MLA_NOPE - MLA_ROPE)))
        wuq = wuq.reshape(MLA_Q_LORA, MLA_HEADS * 2 * LANES).astype(BF16)
        wukv = w_ukv[l].reshape(MLA_KV_LORA, MLA_HEADS, MLA_NOPE + MLA_V)
        wuk = wukv[:, :, :MLA_NOPE].reshape(MLA_KV_LORA, MLA_HEADS * MLA_NOPE).astype(BF16)
        wuv = wukv[:, :, MLA_NOPE:].reshape(MLA_KV_LORA, MLA_HEADS * MLA_V).astype(BF16)
        gains = [tile2(diff_qk_g[l]), tile2(na_qk_g[l]), mla_q_norm_g[l].reshape(1, -1),
                 mla_kv_norm_g[l].reshape(1, -1), pad_qk(mla_qk_g[l, 0]), pad_qk(mla_qk_g[l, 1])]

        mod = _modulation(c_all, w_mod, b_mod, l).reshape(16, 6, D)
        h1, p = _in_proj(xa, xb, xb_row0, NL, mod, norm1_g[l].reshape(1, D), w_in_bf, l, NT, mod_idx_in)
        qa, ka, qr, kr, qn, kn, qm, km, vm = _prep(
            p, (cos_a, sin_a, cos_r, sin_r), gains, (wuq, wuk, wuv), NT, rope_idx)

        sub_g = diff_sub_g[l].reshape(1, -1)
        mla_scale = (MLA_NOPE + MLA_ROPE) ** -0.5
        ret_tabs = _retention_tables(ret_decay[l])
        common = dict(batch=B, seq=S)
        ya = _attention(qa, ka, p, (diff_lambda[l], sub_g), mode="diff", heads=DA_HEADS, dq=LANES,
                        v_col0=COL_DA_V, latent=True, lam_init=lam_init, name="diff_attn", **common)
        yb, yb_c = _retention(qr, kr, p, ret_tabs, ret_norm_g[l].reshape(1, -1), ctx_out=not last, **common)
        yc = _na_latent(qn, kn, p, _na_bias_table(na_rpb[l], na_variants), na_plans, **common)
        yd = _attention(qm, km, vm, (), mode="plain", heads=MLA_HEADS, dq=2 * LANES, v_col0=0,
                        latent=True, scale=mla_scale, name="latent_attn", **common)
        ys = [ya, yb, yc, yd]
        ys_ctx = ys
        if not last:
            ya_c = _attention(qa, ka, p, (diff_lambda[l], sub_g), mode="diff", heads=DA_HEADS, dq=LANES,
                              v_col0=COL_DA_V, latent=False, lam_init=lam_init, name="diff_attn_ctx", **common)
            yc_c = _attention(qn, kn, p, (), mode="pair", heads=NA_HEADS // 2, dq=LANES, v_col0=COL_NA_V,
                              latent=False, name="ctx_attn", **common)
            yd_c = _attention(qm, km, vm, (), mode="plain", heads=MLA_HEADS, dq=2 * LANES, v_col0=0,
                              latent=False, scale=mla_scale, name="latent_attn_ctx", **common)
            ys_ctx = [ya_c, yb_c, yc_c, yd_c]

        m = _merge1(h1, ys, ys_ctx, wg_bf, wb_bf, b_gate, l, n_tok)
        x_mid, h2t = _merge2(m, wo_bf, l, xa, xb, xb_row0, NL, mod, norm2_g[l].reshape(1, D), n_tok, mod_idx)
        s1, th, e1, e2 = _route(h2t, wqt, sk, l, n_tok)
        xa = xb = _experts(h2t, u_bf, vt_bf, l, s1, e1, th, e2, x_mid, mod, n_tok, mod_idx)
        xb_row0 = NL

    return xa.reshape(B, S, D)
```
